```python
import jax, jax.numpy as jnp
from jax import lax
import numpy as np

D_MODEL = 1024
BATCH = 8
SEQ = 2048
DEPTH = 1
DEC_BATCH = 128
DEC_SEQ = 1
PAST_LEN = 16384
PAGE_SIZE = 128

R_HEADS = 8
R_HEAD_DIM = 64
R_WIDTH = R_HEADS * R_HEAD_DIM
W_LORA = 64
A_LORA = 64
G_LORA = 128
R_GN_EPS = 64e-5
M_HEADS = 4
M_HEAD_DIM = 128
M_WIDTH = M_HEADS * M_HEAD_DIM
CONV_W = 4
MLSTM_CHUNK = 64
M_GN_EPS = 1e-5
D_FF = 4 * D_MODEL
RMS_EPS = 1e-6
R_SHIFT_COLS = 3 * R_WIDTH + W_LORA + A_LORA + G_LORA
M_COLS = 3 * M_WIDTH + 2 * M_HEADS
GATE_COLS = 2 * D_MODEL
IN_COLS = R_SHIFT_COLS + M_COLS + GATE_COLS

F32 = jnp.float32

kernel_name = 'rwkv7_mlstm_gated_hybrid_step'


def _rmsnorm(x, g):
    xf = x.astype(F32)
    y = xf * lax.rsqrt(jnp.mean(xf * xf, axis=-1, keepdims=True) + RMS_EPS)
    return (y * g.astype(F32)).astype(x.dtype)


def _head_norm(y, eps):
    yf = y.astype(F32)
    mu = jnp.mean(yf, axis=-1, keepdims=True)
    var = jnp.mean(jnp.square(yf - mu), axis=-1, keepdims=True)
    return (yf - mu) * lax.rsqrt(var + eps)


def _rwkv7_recurrence(r, log_w, k, v, kk, a, S0):
    def step(S, inp):
        r_t, lw_t, k_t, v_t, kk_t, a_t = inp
        s_kk = jnp.einsum('bhvk,bhk->bhv', S, kk_t)
        S = (S * jnp.exp(lw_t)[:, :, None, :]
             - s_kk[..., None] * (kk_t * a_t)[:, :, None, :]
             + v_t[..., None] * k_t[:, :, None, :])
        return S, jnp.einsum('bhvk,bhk->bhv', S, r_t)
    xs = tuple(jnp.swapaxes(t.astype(F32), 0, 1) for t in (r, log_w, k, v, kk, a))
    S, ys = lax.scan(step, S0.astype(F32), xs)
    return jnp.swapaxes(ys, 0, 1), S


def _rwkv7_branch(pr, pr_prev, S0, p):
    B, T, _ = pr.shape
    mixed = pr + (pr_prev - pr) * p['r_mu']
    r, k, v, xw, xa, xg = jnp.split(
        mixed, [R_WIDTH, 2 * R_WIDTH, 3 * R_WIDTH, 3 * R_WIDTH + W_LORA, 3 * R_WIDTH + W_LORA + A_LORA], axis=-1)
    w = (p['r_w0'] + jnp.tanh(xw) @ p['r_w2']).astype(F32)
    log_decay = -jnp.exp(-jax.nn.softplus(-w) - 0.5)
    a = jax.nn.sigmoid(p['r_a0'] + xa @ p['r_a2'])
    g = jax.nn.sigmoid(xg) @ p['r_g2']
    hs = lambda t: t.reshape(B, T, R_HEADS, R_HEAD_DIM)
    kk = hs((k * p['r_kk']).astype(F32))
    kk = kk / jnp.maximum(jnp.sqrt(jnp.sum(kk * kk, axis=-1, keepdims=True)), 1e-12)
    k = k * (1.0 + (a - 1.0) * p['r_ka'])
    r_h, k_h, v_h, a_h = hs(r), hs(k), hs(v), hs(a)
    y, S_new = _rwkv7_recurrence(r_h, hs(log_decay), k_h, v_h, kk, a_h, S0)
    y = _head_norm(y, R_GN_EPS) * p['r_gn_w'].reshape(R_HEADS, R_HEAD_DIM) + p['r_gn_b'].reshape(R_HEADS, R_HEAD_DIM)
    y = y + jnp.sum(r_h * k_h * p['r_rk'], axis=-1, keepdims=True) * v_h
    y = (y.reshape(B, T, R_WIDTH) * g).astype(pr.dtype)
    return y, S_new


def _mlstm_chunked(q, k, v, logi, logf, C0, n0, m0, chunk):
    B, T, H, Dh = q.shape
    nc = T // chunk
    c4 = lambda t: t.astype(F32).reshape(B, nc, chunk, H, Dh).transpose(1, 0, 3, 2, 4)
    c3 = lambda t: t.astype(F32).reshape(B, nc, chunk, H).transpose(1, 0, 3, 2)
    causal = jnp.tril(jnp.ones((chunk, chunk), dtype=bool))

    def step(carry, inp):
        C, n, m = carry
        qb, kb, vb, li, lf = inp
        bcum = jnp.cumsum(lf, axis=-1)
        dmat = bcum[..., :, None] - bcum[..., None, :] + li[..., None, :]
        dmat = jnp.where(causal, dmat, -jnp.inf)
        g = bcum + m[..., None]
        m_t = jnp.maximum(g, jnp.max(dmat, axis=-1))
        w_in = jnp.exp(dmat - m_t[..., None])
        w_st = jnp.exp(g - m_t)
        s = jnp.einsum('bhtd,bhsd->bhts', qb, kb) * w_in
        num = w_st[..., None] * jnp.einsum('bhtd,bhde->bhte', qb, C) + jnp.einsum('bhts,bhse->bhte', s, vb)
        den = w_st * jnp.einsum('bhtd,bhd->bht', qb, n) + jnp.sum(s, axis=-1)
        h = num / jnp.maximum(jnp.abs(den), jnp.exp(-m_t))[..., None]
        b_last = bcum[..., -1]
        lw_end = b_last[..., None] - bcum + li
        g_end = b_last + m
        m_new = jnp.maximum(g_end, jnp.max(lw_end, axis=-1))
        we = jnp.exp(lw_end - m_new[..., None])
        ge = jnp.exp(g_end - m_new)
        C_new = ge[..., None, None] * C + jnp.einsum('bhs,bhsd,bhse->bhde', we, kb, vb)
        n_new = ge[..., None] * n + jnp.einsum('bhs,bhsd->bhd', we, kb)
        return (C_new, n_new, m_new), h

    carry0 = (C0.astype(F32), n0.astype(F32), m0.astype(F32))
    (C, n, m), hs = lax.scan(step, carry0, (c4(q), c4(k), c4(v), c3(logi), c3(logf)))
    h = hs.transpose(1, 0, 3, 2, 4).reshape(B, T, H, Dh)
    return h, C, n, m


def _mlstm_branch(pm, C0, n0, m0, conv0, p):
    B, T, _ = pm.shape
    xm, v, o, ig, fg = jnp.split(pm, [M_WIDTH, 2 * M_WIDTH, 3 * M_WIDTH, 3 * M_WIDTH + M_HEADS], axis=-1)
    xp = jnp.concatenate([conv0.astype(xm.dtype), xm], axis=1)
    xc = p['m_conv_b'] + sum(xp[:, j:j + T] * p['m_conv_w'][j] for j in range(CONV_W))
    xc = jax.nn.silu(xc)
    conv_new = xp[:, xp.shape[1] - (CONV_W - 1):]
    xc_h = xc.reshape(B, T, M_HEADS, M_HEAD_DIM)
    q = jnp.einsum('bthd,hde->bthe', xc_h, p['m_wq']) * (M_HEAD_DIM ** -0.5)
    k = jnp.einsum('bthd,hde->bthe', xc_h, p['m_wk'])
    v_h = v.reshape(B, T, M_HEADS, M_HEAD_DIM)
    logi = (ig + p['m_i_b']).astype(F32)
    logf = jax.nn.log_sigmoid((fg + p['m_f_b']).astype(F32))
    chunk = MLSTM_CHUNK if T % MLSTM_CHUNK == 0 else T
    h, C, n, m = _mlstm_chunked(q, k, v_h, logi, logf, C0, n0, m0, chunk)
    h = (_head_norm(h, M_GN_EPS) * p['m_gn_w'].reshape(M_HEADS, M_HEAD_DIM)
         + p['m_skip'].reshape(M_HEADS, M_HEAD_DIM) * xc_h)
    y = (jax.nn.sigmoid(o) * h.reshape(B, T, M_WIDTH)).astype(pm.dtype)
    return y, C, n, m, conv_new


def _layer(x, st, p):
    shift0, S0, C0, n0, m0, conv0 = st
    dt = x.dtype
    xn = _rmsnorm(x, p['norm_mix_g'])
    proj = xn @ p['w_in']
    pr = proj[..., :R_SHIFT_COLS]
    pm = proj[..., R_SHIFT_COLS:R_SHIFT_COLS + M_COLS]
    pg = proj[..., R_SHIFT_COLS + M_COLS:] + p['gate_b']
    pr_first = shift0.astype(dt) @ p['w_in'][:, :R_SHIFT_COLS]
    pr_prev = jnp.concatenate([pr_first[:, None], pr[:, :-1]], axis=1)
    y_r, S_new = _rwkv7_branch(pr, pr_prev, S0, p)
    y_m, C_new, n_new, m_new, conv_new = _mlstm_branch(pm, C0, n0, m0, conv0, p)
    g_r, g_m = jnp.split(pg, 2, axis=-1)
    merged = jax.nn.sigmoid(g_r) * (y_r @ p['r_up']) + jax.nn.sigmoid(g_m) * (y_m @ p['m_up'])
    x = x + merged @ p['w_out']
    hn = _rmsnorm(x, p['norm_ffn_g'])
    x = x + jnp.square(jax.nn.relu(hn @ p['ffn_w1'])) @ p['ffn_w2']
    new = (xn[:, -1], S_new.astype(dt), C_new.astype(dt), n_new.astype(dt), m_new.astype(dt), conv_new.astype(dt))
    return x, new


def _trunk(x, states, layers, norm_final_g):
    acc = [[] for _ in states]
    for l in range(DEPTH):
        x, new = _layer(x, tuple(s[l] for s in states), layers[l])
        for a_list, s in zip(acc, new):
            a_list.append(s)
    y = _rmsnorm(x, norm_final_g)
    return y, [jnp.stack(a_list) for a_list in acc]


def setup_inputs(seed: int = 0) -> dict:
    key = jax.random.key(seed)
    ks = iter(jax.random.split(key, 64))
    nrm = lambda shape, scale: scale * jax.random.normal(next(ks), shape, F32)
    uni = lambda shape, lo, hi: jax.random.uniform(next(ks), shape, F32, lo, hi)
    L = DEPTH
    d = {}
    d['x_prompt'] = nrm((BATCH, SEQ, D_MODEL), 1.0)
    d['x_sample'] = nrm((DEC_BATCH, DEC_SEQ, D_MODEL), 1.0)
    d['state_rwkv_shift'] = nrm((L, DEC_BATCH, D_MODEL), 1.0)
    d['state_rwkv_wkv'] = nrm((L, DEC_BATCH, R_HEADS, R_HEAD_DIM, R_HEAD_DIM), 0.1)
    d['state_mlstm_C'] = nrm((L, DEC_BATCH, M_HEADS, M_HEAD_DIM, M_HEAD_DIM), 0.1)
    d['state_mlstm_n'] = nrm((L, DEC_BATCH, M_HEADS, M_HEAD_DIM), 0.1)
    d['state_mlstm_m'] = nrm((L, DEC_BATCH, M_HEADS), 1.0)
    d['state_mlstm_conv'] = nrm((L, DEC_BATCH, CONV_W - 1, M_WIDTH), 1.0)
    d['norm_mix_g'] = 1.0 + nrm((L, D_MODEL), 0.02)
    d['w_in'] = nrm((L, D_MODEL, IN_COLS), D_MODEL ** -0.5)
    d['r_mu'] = uni((L, R_SHIFT_COLS), 0.0, 1.0)
    d['r_w0'] = uni((L, R_WIDTH), -6.0, 2.0)
    d['r_w2'] = nrm((L, W_LORA, R_WIDTH), 0.1)
    d['r_a0'] = nrm((L, R_WIDTH), 0.1)
    d['r_a2'] = nrm((L, A_LORA, R_WIDTH), 0.1)
    d['r_g2'] = nrm((L, G_LORA, R_WIDTH), G_LORA ** -0.5)
    d['r_kk'] = 0.85 + nrm((L, R_WIDTH), 0.02)
    d['r_ka'] = 1.0 + nrm((L, R_WIDTH), 0.02)
    d['r_rk'] = nrm((L, R_HEADS, R_HEAD_DIM), 0.1)
    d['r_gn_w'] = 1.0 + nrm((L, R_WIDTH), 0.02)
    d['r_gn_b'] = nrm((L, R_WIDTH), 0.02)
    d['r_up'] = nrm((L, R_WIDTH, D_MODEL), R_WIDTH ** -0.5)
    d['m_conv_w'] = nrm((L, CONV_W, M_WIDTH), CONV_W ** -0.5)
    d['m_conv_b'] = nrm((L, M_WIDTH), 0.02)
    d['m_wq'] = nrm((L, M_HEADS, M_HEAD_DIM, M_HEAD_DIM), M_HEAD_DIM ** -0.5)
    d['m_wk'] = nrm((L, M_HEADS, M_HEAD_DIM, M_HEAD_DIM), M_HEAD_DIM ** -0.5)
    d['m_i_b'] = nrm((L, M_HEADS), 0.1)
    d['m_f_b'] = jnp.linspace(3.0, 6.0, M_HEADS, dtype=F32)[None, :] + nrm((L, M_HEADS), 0.1)
    d['m_gn_w'] = 1.0 + nrm((L, M_WIDTH), 0.02)
    d['m_skip'] = 1.0 + nrm((L, M_WIDTH), 0.02)
    d['m_up'] = nrm((L, M_WIDTH, D_MODEL), M_WIDTH ** -0.5)
    d['gate_b'] = nrm((L, GATE_COLS), 0.02)
    d['w_out'] = nrm((L, D_MODEL, D_MODEL), D_MODEL ** -0.5)
    d['norm_ffn_g'] = 1.0 + nrm((L, D_MODEL), 0.02)
    d['ffn_w1'] = nrm((L, D_MODEL, D_FF), D_MODEL ** -0.5)
    d['ffn_w2'] = nrm((L, D_FF, D_MODEL), D_FF ** -0.5)
    d['norm_final_g'] = 1.0 + nrm((D_MODEL,), 0.02)
    return d


def reference(x_prompt, x_sample, state_rwkv_shift, state_rwkv_wkv, state_mlstm_C, state_mlstm_n,
              state_mlstm_m, state_mlstm_conv, norm_mix_g, w_in, r_mu, r_w0, r_w2, r_a0, r_a2, r_g2,
              r_kk, r_ka, r_rk, r_gn_w, r_gn_b, r_up, m_conv_w, m_conv_b, m_wq, m_wk, m_i_b, m_f_b,
              m_gn_w, m_skip, m_up, gate_b, w_out, norm_ffn_g, ffn_w1, ffn_w2, norm_final_g):
    layers = [dict(norm_mix_g=norm_mix_g[l], w_in=w_in[l], r_mu=r_mu[l], r_w0=r_w0[l], r_w2=r_w2[l],
                   r_a0=r_a0[l], r_a2=r_a2[l], r_g2=r_g2[l], r_kk=r_kk[l], r_ka=r_ka[l], r_rk=r_rk[l],
                   r_gn_w=r_gn_w[l], r_gn_b=r_gn_b[l], r_up=r_up[l], m_conv_w=m_conv_w[l],
                   m_conv_b=m_conv_b[l], m_wq=m_wq[l], m_wk=m_wk[l], m_i_b=m_i_b[l], m_f_b=m_f_b[l],
                   m_gn_w=m_gn_w[l], m_skip=m_skip[l], m_up=m_up[l], gate_b=gate_b[l], w_out=w_out[l],
                   norm_ffn_g=norm_ffn_g[l], ffn_w1=ffn_w1[l], ffn_w2=ffn_w2[l])
              for l in range(DEPTH)]
    dt = x_prompt.dtype
    bp = x_prompt.shape[0]
    zero_states = (jnp.zeros((DEPTH, bp, D_MODEL), dt),
                   jnp.zeros((DEPTH, bp, R_HEADS, R_HEAD_DIM, R_HEAD_DIM), dt),
                   jnp.zeros((DEPTH, bp, M_HEADS, M_HEAD_DIM, M_HEAD_DIM), dt),
                   jnp.zeros((DEPTH, bp, M_HEADS, M_HEAD_DIM), dt),
                   jnp.zeros((DEPTH, bp, M_HEADS), dt),
                   jnp.zeros((DEPTH, bp, CONV_W - 1, M_WIDTH), dt))
    y_prompt, p_st = _trunk(x_prompt, zero_states, layers, norm_final_g)
    sample_states = (state_rwkv_shift, state_rwkv_wkv, state_mlstm_C, state_mlstm_n, state_mlstm_m, state_mlstm_conv)
    y_sample, s_st = _trunk(x_sample, sample_states, layers, norm_final_g)
    return (y_prompt, y_sample, p_st[0], p_st[1], p_st[2], p_st[3], p_st[4], p_st[5],
            s_st[0], s_st[1], s_st[2], s_st[3], s_st[4], s_st[5])
```

```python
import functools
import math

import jax
import jax.numpy as jnp
from jax import lax
from jax.experimental import pallas as pl
from jax.experimental.pallas import tpu as pltpu

F32 = jnp.float32
BF16 = jnp.bfloat16
HIGHEST = lax.Precision.HIGHEST

D_MODEL = 1024
R_HEADS = 8
R_HEAD_DIM = 64
R_WIDTH = R_HEADS * R_HEAD_DIM
W_LORA = 64
A_LORA = 64
G_LORA = 128
R_GN_EPS = 64e-5
M_HEADS = 4
M_HEAD_DIM = 128
M_WIDTH = M_HEADS * M_HEAD_DIM
CONV_W = 4
M_GN_EPS = 1e-5
D_FF = 4 * D_MODEL
RMS_EPS = 1e-6
R_SHIFT_COLS = 3 * R_WIDTH + W_LORA + A_LORA + G_LORA
M_COLS = 3 * M_WIDTH + 2 * M_HEADS
GATE_COLS = 2 * D_MODEL

LANES = 128
SUBLANES = 8
CHUNK = 64
M_COLS_PAD = -(-M_COLS // LANES) * LANES
IN_COLS_PAD = R_SHIFT_COLS + M_COLS_PAD + GATE_COLS
VMEM_LIMIT = 56 * 1024 * 1024


def _mm(a, b, dims="nn", hi=False):
    ca = 1 if dims[0] == "n" else 0
    cb = 0 if dims[1] == "n" else 1
    dn = (((ca,), (cb,)), ((), ()))
    if hi:
        return lax.dot_general(a, b, dn, precision=HIGHEST, preferred_element_type=F32)
    return lax.dot_general(a.astype(BF16), b.astype(BF16), dn, preferred_element_type=F32)


def _rms(x, g):
    return x * lax.rsqrt(jnp.mean(x * x, axis=-1, keepdims=True) + RMS_EPS) * g


def _sigmoid(x):
    return 1.0 / (1.0 + jnp.exp(-x))


def _log_sigmoid(x):
    return jnp.minimum(x, 0.0) - jnp.log(1.0 + jnp.exp(-jnp.abs(x)))


def _iota2(shape, axis):
    return lax.broadcasted_iota(jnp.int32, shape, axis)


def _const_spec(shape):
    nd = len(shape)
    return pl.BlockSpec(shape, lambda *_: (0,) * nd, pipeline_mode=pl.Buffered(1))


def _proj_body(x_ref, g_ref, w_ref, gb_ref, pr_ref, pm_ref, pg_ref):
    xb = _rms(x_ref[...], g_ref[...]).astype(BF16)
    pr_ref[...] = jnp.dot(xb, w_ref[:, 0:R_SHIFT_COLS], preferred_element_type=F32)
    pm_ref[...] = jnp.dot(xb, w_ref[:, R_SHIFT_COLS:R_SHIFT_COLS + M_COLS_PAD], preferred_element_type=F32)
    pg_ref[...] = jnp.dot(xb, w_ref[:, R_SHIFT_COLS + M_COLS_PAD:], preferred_element_type=F32) + gb_ref[...]


def _proj(x2, g, w_pad, gate_b, tm):
    m = x2.shape[0]
    return pl.pallas_call(
        _proj_body,
        grid=(m // tm,),
        in_specs=[pl.BlockSpec((tm, D_MODEL), lambda i: (i, 0)),
                  _const_spec((1, D_MODEL)),
                  _const_spec((D_MODEL, IN_COLS_PAD)),
                  _const_spec((1, GATE_COLS))],
        out_specs=[pl.BlockSpec((tm, R_SHIFT_COLS), lambda i: (i, 0)),
                   pl.BlockSpec((tm, M_COLS_PAD), lambda i: (i, 0)),
                   pl.BlockSpec((tm, GATE_COLS), lambda i: (i, 0))],
        out_shape=[jax.ShapeDtypeStruct((m, R_SHIFT_COLS), F32),
                   jax.ShapeDtypeStruct((m, M_COLS_PAD), F32),
                   jax.ShapeDtypeStruct((m, GATE_COLS), F32)],
        compiler_params=pltpu.CompilerParams(dimension_semantics=("arbitrary",), vmem_limit_bytes=VMEM_LIMIT),
        name="proj",
    )(x2, g, w_pad, gate_b)


def _proj_step_body(xs_ref, sh_ref, xl_ref, g_ref, w_ref, gb_ref,
                    xns_ref, xnl_ref, pr_ref, pm_ref, pg_ref, prev_ref):
    g = g_ref[...]
    xn = _rms(xs_ref[...], g)
    xns_ref[...] = xn
    xnl_ref[...] = _rms(xl_ref[...], g)
    xb = xn.astype(BF16)
    pr_ref[...] = jnp.dot(xb, w_ref[:, 0:R_SHIFT_COLS], preferred_element_type=F32)
    pm_ref[...] = jnp.dot(xb, w_ref[:, R_SHIFT_COLS:R_SHIFT_COLS + M_COLS_PAD], preferred_element_type=F32)
    pg_ref[...] = jnp.dot(xb, w_ref[:, R_SHIFT_COLS + M_COLS_PAD:], preferred_element_type=F32) + gb_ref[...]
    prev_ref[...] = jnp.dot(sh_ref[...].astype(BF16), w_ref[:, 0:R_SHIFT_COLS], preferred_element_type=F32)


def _proj_step(xs, shift0, xlast, g, w_pad, gate_b):
    nb = xs.shape[0]
    nl = xlast.shape[0]
    return pl.pallas_call(
        _proj_step_body,
        out_shape=[jax.ShapeDtypeStruct((nb, D_MODEL), F32),
                   jax.ShapeDtypeStruct((nl, D_MODEL), F32),
                   jax.ShapeDtypeStruct((nb, R_SHIFT_COLS), F32),
                   jax.ShapeDtypeStruct((nb, M_COLS_PAD), F32),
                   jax.ShapeDtypeStruct((nb, GATE_COLS), F32),
                   jax.ShapeDtypeStruct((nb, R_SHIFT_COLS), F32)],
        compiler_params=pltpu.CompilerParams(vmem_limit_bytes=VMEM_LIMIT),
        name="proj_step",
    )(xs, shift0, xlast, g, w_pad, gate_b)


LOG_DECAY_SCALE = -math.exp(-0.5)


def _rwkv_mix(pr, prev, mu, w0, w2, a0, a2, g2, r_kk, r_ka):
    mixed = pr + (prev - pr) * mu
    r = mixed[:, 0:R_WIDTH]
    k = mixed[:, R_WIDTH:2 * R_WIDTH]
    v = mixed[:, 2 * R_WIDTH:3 * R_WIDTH]
    xw = mixed[:, 3 * R_WIDTH:3 * R_WIDTH + W_LORA]
    xa = mixed[:, 3 * R_WIDTH + W_LORA:3 * R_WIDTH + W_LORA + A_LORA]
    xg = mixed[:, 3 * R_WIDTH + W_LORA + A_LORA:]
    w = w0 + _mm(jnp.tanh(xw), w2)
    logw = LOG_DECAY_SCALE * _sigmoid(w)
    a = _sigmoid(a0 + _mm(xa, a2))
    g = _mm(_sigmoid(xg), g2)
    kk0 = k * r_kk
    k2 = k * (1.0 + (a - 1.0) * r_ka)
    return r, k2, v, logw, a, g, kk0


def _rwkv_seq_body(pr_ref, mu_ref, w0_ref, w2_ref, a0_ref, a2_ref, g2_ref, kk_ref, ka_ref, rk_ref,
                   gnw_ref, gnb_ref, y_ref, s_out_ref, s_scr, carry_scr, *, tt, hi):
    t_idx = pl.program_id(1)

    @pl.when(t_idx == 0)
    def _():
        s_scr[...] = jnp.zeros_like(s_scr)
        carry_scr[...] = jnp.zeros_like(carry_scr)

    pr = pr_ref[...]
    row = _iota2(pr.shape, 0)
    prev = jnp.where(row == 0, carry_scr[...], pltpu.roll(pr, 1, axis=0))
    carry_scr[...] = pr[tt - 1:tt, :]

    r, k2, v, logw, a, g, kk0 = _rwkv_mix(pr, prev, mu_ref[...], w0_ref[...], w2_ref[...], a0_ref[...],
                                          a2_ref[...], g2_ref[...], kk_ref[...], ka_ref[...])
    rk = r * k2 * rk_ref[...]
    gnw = gnw_ref[...]
    gnb = gnb_ref[...]

    ri = _iota2((CHUNK, CHUNK), 0)
    ci = _iota2((CHUNK, CHUNK), 1)
    low_incl = ci <= ri
    low_strict = ci < ri
    eye = (ci == ri).astype(F32)
    tril = low_incl.astype(F32)

    for c in range(tt // CHUNK):
        cs = slice(c * CHUNK, (c + 1) * CHUNK)
        lw_c = logw[cs]
        cum = _mm(tril, lw_c, hi=True)
        cum_last = cum[CHUNK - 1:CHUNK, :]
        d_inc = jnp.exp(cum)
        d_exc = jnp.exp(cum - lw_c)
        d_inv = jnp.exp(-cum)
        d_end = jnp.exp(cum_last - cum)
        d_last = jnp.exp(cum_last)
        kk_c = kk0[cs]
        kka_c = kk_c * a[cs]
        k_c = k2[cs]
        at0 = kk_c * d_exc
        rt = r[cs] * d_inc
        bt0 = kka_c * d_inv
        kt = k_c * d_inv
        bh0 = kka_c * d_end
        kh = k_c * d_end
        v_c = v[cs]
        rk_c = rk[cs]
        outs = []
        for h in range(R_HEADS):
            hs = slice(h * R_HEAD_DIM, (h + 1) * R_HEAD_DIM)
            kk_h = kk_c[:, hs]
            inv_n = 1.0 / jnp.maximum(jnp.sqrt(jnp.sum(kk_h * kk_h, axis=1, keepdims=True)), 1e-12)
            a_ = at0[:, hs] * inv_n
            b_ = bt0[:, hs] * inv_n
            bh_ = bh0[:, hs] * inv_n
            r_ = rt[:, hs]
            k_ = kt[:, hs]
            kh_ = kh[:, hs]
            v_ = v_c[:, hs]
            p = _mm(jnp.concatenate([a_, r_], axis=0), jnp.concatenate([b_, k_], axis=0), "nt", hi)
            t_ab = jnp.where(low_strict, p[:CHUNK, :CHUNK], 0.0)
            t_ak = jnp.where(low_strict, p[:CHUNK, CHUNK:], 0.0)
            m_rb = jnp.where(low_incl, p[CHUNK:, :CHUNK], 0.0)
            m_rk = jnp.where(low_incl, p[CHUNK:, CHUNK:], 0.0)
            q_pow = _mm(t_ab, t_ab, "nn", hi)
            x_inv = eye - t_ab
            for _ in range(4):
                z = _mm(q_pow, jnp.concatenate([x_inv, q_pow], axis=1), "nn", hi)
                x_inv = x_inv + z[:, :CHUNK]
                q_pow = z[:, CHUNK:]
            x_inv = x_inv + _mm(q_pow, x_inv, "nn", hi)
            tv = _mm(t_ak, v_, "nn", hi)
            xw = _mm(x_inv, jnp.concatenate([a_, tv], axis=1), "nn", hi)
            mw = _mm(m_rb, xw, "nn", hi)
            w_a = xw[:, :CHUNK]
            u_v = xw[:, CHUNK:]
            q_eff = r_ - mw[:, :CHUNK]
            y_loc = _mm(m_rk, v_, "nn", hi) - mw[:, CHUNK:]
            g_tr = eye * d_last[:, hs] - _mm(w_a, bh_, "tn", hi)
            h_tr = _mm(v_, kh_, "tn", hi) - _mm(u_v, bh_, "tn", hi)
            s0 = s_scr[h]
            y = _mm(q_eff, s0, "nt", hi) + y_loc
            s_scr[h] = _mm(s0, g_tr, "nn", hi) + h_tr
            mu_y = jnp.mean(y, axis=1, keepdims=True)
            yc = y - mu_y
            var = jnp.mean(yc * yc, axis=1, keepdims=True)
            bonus = jnp.sum(rk_c[:, hs], axis=1, keepdims=True)
            outs.append(yc * lax.rsqrt(var + R_GN_EPS) * gnw[:, hs] + gnb[:, hs] + bonus * v_)
        y_ref[cs, :] = jnp.concatenate(outs, axis=1) * g[cs]

    @pl.when(t_idx == pl.num_programs(1) - 1)
    def _():
        s_out_ref[0] = s_scr[...]


def _rwkv_seq(pr, p, nb, nt, tt, hi):
    m = pr.shape[0]
    row = lambda n: _const_spec((1, n))
    return pl.pallas_call(
        functools.partial(_rwkv_seq_body, tt=tt, hi=hi),
        grid=(nb, nt),
        in_specs=[pl.BlockSpec((tt, R_SHIFT_COLS), lambda b, t: (b * nt + t, 0)),
                  row(R_SHIFT_COLS), row(R_WIDTH), _const_spec((W_LORA, R_WIDTH)), row(R_WIDTH),
                  _const_spec((A_LORA, R_WIDTH)), _const_spec((G_LORA, R_WIDTH)), row(R_WIDTH), row(R_WIDTH),
                  row(R_WIDTH), row(R_WIDTH), row(R_WIDTH)],
        out_specs=[pl.BlockSpec((tt, R_WIDTH), lambda b, t: (b * nt + t, 0)),
                   pl.BlockSpec((1, R_HEADS, R_HEAD_DIM, R_HEAD_DIM), lambda b, t: (b, 0, 0, 0))],
        out_shape=[jax.ShapeDtypeStruct((m, R_WIDTH), F32),
                   jax.ShapeDtypeStruct((nb, R_HEADS, R_HEAD_DIM, R_HEAD_DIM), F32)],
        scratch_shapes=[pltpu.VMEM((R_HEADS, R_HEAD_DIM, R_HEAD_DIM), F32),
                        pltpu.VMEM((1, R_SHIFT_COLS), F32)],
        compiler_params=pltpu.CompilerParams(dimension_semantics=("arbitrary", "arbitrary"),
                                             vmem_limit_bytes=VMEM_LIMIT),
        name="rwkv_seq",
    )(pr, p["mu"], p["w0"], p["w2"], p["a0"], p["a2"], p["g2"], p["kk"], p["ka"], p["rk"],
      p["gn_w"], p["gn_b"])


def _rwkv_step_body(pr_ref, prev_ref, s_ref, mu_ref, w0_ref, w2_ref, a0_ref, a2_ref, g2_ref, kk_ref, ka_ref,
                    rk_ref, gnw_ref, gnb_ref, y_ref, s_out_ref, yv_scr, *, bb):
    r, k2, v, logw, a, g, kk0 = _rwkv_mix(pr_ref[...], prev_ref[...], mu_ref[...], w0_ref[...], w2_ref[...],
                                          a0_ref[...], a2_ref[...], g2_ref[...], kk_ref[...], ka_ref[...])
    decay = jnp.exp(logw)
    ri = _iota2((R_HEAD_DIM, R_HEAD_DIM), 0)
    ci = _iota2((R_HEAD_DIM, R_HEAD_DIM), 1)
    eye = ri == ci
    for h in range(R_HEADS):
        hs = slice(h * R_HEAD_DIM, (h + 1) * R_HEAD_DIM)
        kk_h = kk0[:, hs]
        kk_h = kk_h / jnp.maximum(jnp.sqrt(jnp.sum(kk_h * kk_h, axis=1, keepdims=True)), 1e-12)
        kka_h = kk_h * a[:, hs]
        for b in range(bb):
            bs = slice(b, b + 1)
            s0 = s_ref[b, h]
            s_kk = jnp.sum(s0 * kk_h[bs], axis=1, keepdims=True)
            v_col = jnp.sum(jnp.where(eye, v[bs, hs], 0.0), axis=1, keepdims=True)
            s1 = s0 * decay[bs, hs] - s_kk * kka_h[bs] + v_col * k2[bs, hs]
            s_out_ref[b, h] = s1
            y_col = jnp.sum(s1 * r[bs, hs], axis=1, keepdims=True)
            yv_scr[bs, hs] = jnp.sum(jnp.where(eye, y_col, 0.0), axis=0, keepdims=True)
    yv = yv_scr[...]
    rk = r * k2 * rk_ref[...]
    gnw = gnw_ref[...]
    gnb = gnb_ref[...]
    outs = []
    for h in range(R_HEADS):
        hs = slice(h * R_HEAD_DIM, (h + 1) * R_HEAD_DIM)
        y = yv[:, hs]
        yc = y - jnp.mean(y, axis=1, keepdims=True)
        var = jnp.mean(yc * yc, axis=1, keepdims=True)
        bonus = jnp.sum(rk[:, hs], axis=1, keepdims=True)
        outs.append(yc * lax.rsqrt(var + R_GN_EPS) * gnw[:, hs] + gnb[:, hs] + bonus * v[:, hs])
    y_ref[...] = jnp.concatenate(outs, axis=1) * g


def _rwkv_step(pr, prev, s0, p, bb):
    nb = pr.shape[0]
    row = lambda n: _const_spec((1, n))
    state_spec = pl.BlockSpec((bb, R_HEADS, R_HEAD_DIM, R_HEAD_DIM), lambda i: (i, 0, 0, 0))
    return pl.pallas_call(
        functools.partial(_rwkv_step_body, bb=bb),
        grid=(nb // bb,),
        in_specs=[pl.BlockSpec((bb, R_SHIFT_COLS), lambda i: (i, 0)),
                  pl.BlockSpec((bb, R_SHIFT_COLS), lambda i: (i, 0)),
                  state_spec,
                  row(R_SHIFT_COLS), row(R_WIDTH), _const_spec((W_LORA, R_WIDTH)), row(R_WIDTH),
                  _const_spec((A_LORA, R_WIDTH)), _const_spec((G_LORA, R_WIDTH)), row(R_WIDTH), row(R_WIDTH),
                  row(R_WIDTH), row(R_WIDTH), row(R_WIDTH)],
        out_specs=[pl.BlockSpec((bb, R_WIDTH), lambda i: (i, 0)), state_spec],
        out_shape=[jax.ShapeDtypeStruct((nb, R_WIDTH), F32),
                   jax.ShapeDtypeStruct(s0.shape, F32)],
        scratch_shapes=[pltpu.VMEM((bb, R_WIDTH), F32)],
        compiler_params=pltpu.CompilerParams(dimension_semantics=("arbitrary",), vmem_limit_bytes=VMEM_LIMIT),
        name="rwkv_step",
    )(pr, prev, s0, p["mu"], p["w0"], p["w2"], p["a0"], p["a2"], p["g2"], p["kk"], p["ka"], p["rk"],
      p["gn_w"], p["gn_b"])


GATE_LANE0 = 3 * M_WIDTH


def _head_norm_rows(x, eps):
    xc = x - jnp.mean(x, axis=1, keepdims=True)
    var = jnp.mean(xc * xc, axis=1, keepdims=True)
    return xc * lax.rsqrt(var + eps)


def _mlstm_seq_body(pm_ref, cw_ref, cb_ref, wq_ref, wk_ref, ib_ref, fb_ref, gnw_ref, skip_ref,
                    y_ref, c_out_ref, n_out_ref, m_out_ref, c_scr, n_scr, m_scr, carry_scr, *, tt):
    t_idx = pl.program_id(1)

    @pl.when(t_idx == 0)
    def _():
        c_scr[...] = jnp.zeros_like(c_scr)
        n_scr[...] = jnp.zeros_like(n_scr)
        m_scr[...] = jnp.zeros_like(m_scr)
        carry_scr[...] = jnp.zeros_like(carry_scr)

    xm = pm_ref[:, 0:M_WIDTH]
    carry = carry_scr[...]
    carry_scr[...] = xm[tt - SUBLANES:tt, :]
    row8 = _iota2((SUBLANES, M_WIDTH), 0)
    cw = cw_ref[...]
    xc = cb_ref[...] + xm * cw[CONV_W - 1:CONV_W, :]
    for s in range(1, CONV_W):
        rolled = pltpu.roll(xm, s, axis=0)
        top = jnp.where(row8 < s, pltpu.roll(carry, s, axis=0), rolled[0:SUBLANES])
        shifted = jnp.concatenate([top, rolled[SUBLANES:]], axis=0)
        xc = xc + shifted * cw[CONV_W - 1 - s:CONV_W - s, :]
    xc = xc * _sigmoid(xc)

    gt = pm_ref[:, GATE_LANE0:GATE_LANE0 + LANES]
    li_all = gt + ib_ref[...]
    lf_all = _log_sigmoid(gt + fb_ref[...])

    ri = _iota2((CHUNK, CHUNK), 0)
    ci = _iota2((CHUNK, CHUNK), 1)
    causal = ci <= ri
    tril = causal.astype(F32)
    gnw = gnw_ref[...]
    skip = skip_ref[...]

    bc_all = jnp.concatenate(
        [_mm(tril, lf_all[c * CHUNK:(c + 1) * CHUNK], hi=True) for c in range(tt // CHUNK)], axis=0)
    li_t = li_all.T
    bc_t = bc_all.T

    for c in range(tt // CHUNK):
        cs = slice(c * CHUNK, (c + 1) * CHUNK)
        li_c = li_all[cs]
        bc_c = bc_all[cs]
        for h in range(M_HEADS):
            hs = slice(h * M_HEAD_DIM, (h + 1) * M_HEAD_DIM)
            xc_h = xc[cs, hs]
            q = _mm(xc_h, wq_ref[h]) * (M_HEAD_DIM ** -0.5)
            k = _mm(xc_h, wk_ref[h])
            vv = pm_ref[cs, M_WIDTH + h * M_HEAD_DIM:M_WIDTH + (h + 1) * M_HEAD_DIM]
            li_col = li_c[:, h:h + 1]
            bc_col = bc_c[:, M_HEADS + h:M_HEADS + h + 1]
            li_row = li_t[h:h + 1, cs]
            bc_row = bc_t[M_HEADS + h:M_HEADS + h + 1, cs]
            m_prev = m_scr[h][0:1, 0:1]
            c0 = c_scr[h]
            n0 = n_scr[h][0:1, :]
            dmat = jnp.where(causal, (bc_col - bc_row) + li_row, -jnp.inf)
            g_st = bc_col + m_prev
            m_t = jnp.maximum(g_st, jnp.max(dmat, axis=1, keepdims=True))
            w_in = jnp.exp(dmat - m_t)
            w_st = jnp.exp(g_st - m_t)
            s = _mm(q, k, "nt") * w_in
            num = w_st * _mm(q, c0) + _mm(s, vv)
            den = w_st * jnp.sum(q * n0, axis=1, keepdims=True) + jnp.sum(s, axis=1, keepdims=True)
            hh = num / jnp.maximum(jnp.abs(den), jnp.exp(-m_t))
            b_last = bc_col[CHUNK - 1:CHUNK, :]
            lw_end = (b_last - bc_col) + li_col
            g_end = b_last + m_prev
            m_new = jnp.maximum(g_end, jnp.max(lw_end, axis=0, keepdims=True))
            we = jnp.exp(lw_end - m_new)
            ge = jnp.exp(g_end - m_new)
            c_scr[h] = ge * c0 + _mm(k, we * vv, "tn")
            n_new = ge * n0 + jnp.sum(we * k, axis=0, keepdims=True)
            n_scr[h] = jnp.broadcast_to(n_new, (SUBLANES, M_HEAD_DIM))
            m_scr[h] = jnp.broadcast_to(m_new, (SUBLANES, LANES))
            hn = _head_norm_rows(hh, M_GN_EPS) * gnw[:, hs] + skip[:, hs] * xc_h
            o = pm_ref[cs, 2 * M_WIDTH + h * M_HEAD_DIM:2 * M_WIDTH + (h + 1) * M_HEAD_DIM]
            y_ref[cs, hs] = _sigmoid(o) * hn

    @pl.when(t_idx == pl.num_programs(1) - 1)
    def _():
        c_out_ref[0] = c_scr[...]
        n_out_ref[0] = n_scr[...]
        m_out_ref[0] = m_scr[...]


def _mlstm_seq(pm, p, nb, nt, tt):
    m = pm.shape[0]
    row = lambda n: _const_spec((1, n))
    return pl.pallas_call(
        functools.partial(_mlstm_seq_body, tt=tt),
        grid=(nb, nt),
        in_specs=[pl.BlockSpec((tt, M_COLS_PAD), lambda b, t: (b * nt + t, 0)),
                  _const_spec((CONV_W, M_WIDTH)), row(M_WIDTH),
                  _const_spec((M_HEADS, M_HEAD_DIM, M_HEAD_DIM)), _const_spec((M_HEADS, M_HEAD_DIM, M_HEAD_DIM)),
                  row(LANES), row(LANES), row(M_WIDTH), row(M_WIDTH)],
        out_specs=[pl.BlockSpec((tt, M_WIDTH), lambda b, t: (b * nt + t, 0)),
                   pl.BlockSpec((1, M_HEADS, M_HEAD_DIM, M_HEAD_DIM), lambda b, t: (b, 0, 0, 0)),
                   pl.BlockSpec((1, M_HEADS, SUBLANES, M_HEAD_DIM), lambda b, t: (b, 0, 0, 0)),
                   pl.BlockSpec((1, M_HEADS, SUBLANES, LANES), lambda b, t: (b, 0, 0, 0))],
        out_shape=[jax.ShapeDtypeStruct((m, M_WIDTH), F32),
                   jax.ShapeDtypeStruct((nb, M_HEADS, M_HEAD_DIM, M_HEAD_DIM), F32),
                   jax.ShapeDtypeStruct((nb, M_HEADS, SUBLANES, M_HEAD_DIM), F32),
                   jax.ShapeDtypeStruct((nb, M_HEADS, SUBLANES, LANES), F32)],
        scratch_shapes=[pltpu.VMEM((M_HEADS, M_HEAD_DIM, M_HEAD_DIM), F32),
                        pltpu.VMEM((M_HEADS, SUBLANES, M_HEAD_DIM), F32),
                        pltpu.VMEM((M_HEADS, SUBLANES, LANES), F32),
                        pltpu.VMEM((SUBLANES, M_WIDTH), F32)],
        compiler_params=pltpu.CompilerParams(dimension_semantics=("arbitrary", "arbitrary"),
                                             vmem_limit_bytes=VMEM_LIMIT),
        name="mlstm_seq",
    )(pm, p["conv_w"], p["conv_b"], p["wq"], p["wk"], p["i_b"], p["f_b"], p["gn_w"], p["skip"])


def _mlstm_step_body(pm_ref, conv_ref, c_ref, n_ref, m_ref, cw_ref, cb_ref, wq_ref, wk_ref, ib_ref, fb_ref,
                     gnw_ref, skip_ref, y_ref, c_out_ref, n_out_ref, m_out_ref, conv_out_ref, hv_scr, *, bb):
    xm = pm_ref[:, 0:M_WIDTH]
    cw = cw_ref[...]
    xc = cb_ref[...] + xm * cw[CONV_W - 1:CONV_W, :]
    for j in range(CONV_W - 1):
        xc = xc + conv_ref[:, j * M_WIDTH:(j + 1) * M_WIDTH] * cw[j:j + 1, :]
    xc = xc * _sigmoid(xc)
    conv_out_ref[:, 0:(CONV_W - 2) * M_WIDTH] = conv_ref[:, M_WIDTH:(CONV_W - 1) * M_WIDTH]
    conv_out_ref[:, (CONV_W - 2) * M_WIDTH:] = xm

    gt = pm_ref[:, GATE_LANE0:GATE_LANE0 + LANES]
    li = (gt + ib_ref[...])[:, 0:M_HEADS]
    lf = _log_sigmoid(gt + fb_ref[...])[:, M_HEADS:2 * M_HEADS]
    m_prev = m_ref[...]
    g_st = lf + m_prev
    m_t = jnp.maximum(g_st, (lf - lf) + li)
    w_in = jnp.exp(((lf - lf) + li) - m_t)
    w_st = jnp.exp(g_st - m_t)
    floor = jnp.exp(-m_t)
    m_out_ref[...] = m_t

    ri = _iota2((M_HEAD_DIM, M_HEAD_DIM), 0)
    ci = _iota2((M_HEAD_DIM, M_HEAD_DIM), 1)
    eye = ri == ci
    for h in range(M_HEADS):
        hs = slice(h * M_HEAD_DIM, (h + 1) * M_HEAD_DIM)
        xc_h = xc[:, hs]
        q = _mm(xc_h, wq_ref[h]) * (M_HEAD_DIM ** -0.5)
        k = _mm(xc_h, wk_ref[h])
        vv = pm_ref[:, M_WIDTH + h * M_HEAD_DIM:M_WIDTH + (h + 1) * M_HEAD_DIM]
        n0 = n_ref[:, hs]
        qk = jnp.sum(q * k, axis=1, keepdims=True)
        qn = jnp.sum(q * n0, axis=1, keepdims=True)
        s = qk * w_in[:, h:h + 1]
        den = w_st[:, h:h + 1] * qn + s
        inv = 1.0 / jnp.maximum(jnp.abs(den), floor[:, h:h + 1])
        n_out_ref[:, hs] = w_st[:, h:h + 1] * n0 + w_in[:, h:h + 1] * k
        for b in range(bb):
            bs = slice(b, b + 1)
            c0 = c_ref[b, h]
            q_col = jnp.sum(jnp.where(eye, q[bs], 0.0), axis=1, keepdims=True)
            k_col = jnp.sum(jnp.where(eye, k[bs], 0.0), axis=1, keepdims=True)
            qc = jnp.sum(c0 * q_col, axis=0, keepdims=True)
            ge = w_st[bs, h:h + 1]
            we = w_in[bs, h:h + 1]
            c_out_ref[b, h] = ge * c0 + (we * k_col) * vv[bs]
            hv_scr[bs, hs] = (ge * qc + s[bs] * vv[bs]) * inv[bs]
    hv = hv_scr[...]
    gnw = gnw_ref[...]
    skip = skip_ref[...]
    outs = []
    for h in range(M_HEADS):
        hs = slice(h * M_HEAD_DIM, (h + 1) * M_HEAD_DIM)
        outs.append(_head_norm_rows(hv[:, hs], M_GN_EPS) * gnw[:, hs] + skip[:, hs] * xc[:, hs])
    o = pm_ref[:, 2 * M_WIDTH:3 * M_WIDTH]
    y_ref[...] = _sigmoid(o) * jnp.concatenate(outs, axis=1)


def _mlstm_step(pm, conv0, c0, n0, m0, p, bb):
    nb = pm.shape[0]
    row = lambda n: _const_spec((1, n))
    conv_cols = (CONV_W - 1) * M_WIDTH
    c_spec = pl.BlockSpec((bb, M_HEADS, M_HEAD_DIM, M_HEAD_DIM), lambda i: (i, 0, 0, 0))
    n_spec = pl.BlockSpec((bb, M_WIDTH), lambda i: (i, 0))
    m_spec = pl.BlockSpec((bb, M_HEADS), lambda i: (i, 0))
    conv_spec = pl.BlockSpec((bb, conv_cols), lambda i: (i, 0))
    return pl.pallas_call(
        functools.partial(_mlstm_step_body, bb=bb),
        grid=(nb // bb,),
        in_specs=[pl.BlockSpec((bb, M_COLS_PAD), lambda i: (i, 0)), conv_spec, c_spec, n_spec, m_spec,
                  _const_spec((CONV_W, M_WIDTH)), row(M_WIDTH),
                  _const_spec((M_HEADS, M_HEAD_DIM, M_HEAD_DIM)), _const_spec((M_HEADS, M_HEAD_DIM, M_HEAD_DIM)),
                  row(LANES), row(LANES), row(M_WIDTH), row(M_WIDTH)],
        out_specs=[pl.BlockSpec((bb, M_WIDTH), lambda i: (i, 0)), c_spec, n_spec, m_spec, conv_spec],
        out_shape=[jax.ShapeDtypeStruct((nb, M_WIDTH), F32),
                   jax.ShapeDtypeStruct(c0.shape, F32),
                   jax.ShapeDtypeStruct(n0.shape, F32),
                   jax.ShapeDtypeStruct(m0.shape, F32),
                   jax.ShapeDtypeStruct(conv0.shape, F32)],
        scratch_shapes=[pltpu.VMEM((bb, M_WIDTH), F32)],
        compiler_params=pltpu.CompilerParams(dimension_semantics=("arbitrary",), vmem_limit_bytes=VMEM_LIMIT),
        name="mlstm_step",
    )(pm, conv0, c0, n0, m0, p["conv_w"], p["conv_b"], p["wq"], p["wk"], p["i_b"], p["f_b"],
      p["gn_w"], p["skip"])


def _out_body(x_ref, pg_ref, yr_ref, ym_ref, rup_ref, mup_ref, wout_ref, gffn_ref, w1_ref, w2_ref, gfin_ref,
              y_ref):
    up_r = jnp.dot(yr_ref[...].astype(BF16), rup_ref[...], preferred_element_type=F32)
    up_m = jnp.dot(ym_ref[...].astype(BF16), mup_ref[...], preferred_element_type=F32)
    merged = _sigmoid(pg_ref[:, 0:D_MODEL]) * up_r + _sigmoid(pg_ref[:, D_MODEL:]) * up_m
    x1 = x_ref[...] + jnp.dot(merged.astype(BF16), wout_ref[...], preferred_element_type=F32)
    hn = _rms(x1, gffn_ref[...]).astype(BF16)
    hid = jnp.maximum(jnp.dot(hn, w1_ref[...], preferred_element_type=F32), 0.0)
    hid = (hid * hid).astype(BF16)
    x2 = x1 + jnp.dot(hid, w2_ref[...], preferred_element_type=F32)
    y_ref[...] = _rms(x2, gfin_ref[...])


def _out(x2, pg, yr, ym, p, tm):
    m = x2.shape[0]
    tile = lambda n: pl.BlockSpec((tm, n), lambda i: (i, 0))
    return pl.pallas_call(
        _out_body,
        grid=(m // tm,),
        in_specs=[tile(D_MODEL), tile(GATE_COLS), tile(R_WIDTH), tile(M_WIDTH),
                  _const_spec((R_WIDTH, D_MODEL)), _const_spec((M_WIDTH, D_MODEL)),
                  _const_spec((D_MODEL, D_MODEL)), _const_spec((1, D_MODEL)),
                  _const_spec((D_MODEL, D_FF)), _const_spec((D_FF, D_MODEL)), _const_spec((1, D_MODEL))],
        out_specs=tile(D_MODEL),
        out_shape=jax.ShapeDtypeStruct((m, D_MODEL), F32),
        compiler_params=pltpu.CompilerParams(dimension_semantics=("arbitrary",), vmem_limit_bytes=VMEM_LIMIT),
        name="merge_ffn",
    )(x2, pg, yr, ym, p["r_up"], p["m_up"], p["w_out"], p["g_ffn"], p["w1"], p["w2"], p["g_fin"])


PROJ_TM = 512
SEQ_TT = 128
OUT_TM = 256
STEP_BB = 8
RWKV_HI = True


def _pad_lanes(v, start):
    out = jnp.zeros((1, LANES), F32)
    return lax.dynamic_update_slice(out, v.reshape(1, -1), (0, start))


def kernel(x_prompt, x_sample, state_rwkv_shift, state_rwkv_wkv, state_mlstm_C, state_mlstm_n, state_mlstm_m, state_mlstm_conv, norm_mix_g, w_in, r_mu, r_w0, r_w2, r_a0, r_a2, r_g2, r_kk, r_ka, r_rk, r_gn_w, r_gn_b, r_up, m_conv_w, m_conv_b, m_wq, m_wk, m_i_b, m_f_b, m_gn_w, m_skip, m_up, gate_b, w_out, norm_ffn_g, ffn_w1, ffn_w2, norm_final_g):
    nbp, seq, _ = x_prompt.shape
    nbs = x_sample.shape[0]
    w = w_in[0]
    w_pad = jnp.concatenate(
        [w[:, :R_SHIFT_COLS + M_COLS], jnp.zeros((D_MODEL, M_COLS_PAD - M_COLS), F32),
         w[:, R_SHIFT_COLS + M_COLS:]], axis=1).astype(BF16)
    g_mix = norm_mix_g[0].reshape(1, D_MODEL)
    gb = gate_b[0].reshape(1, GATE_COLS)
    rp = dict(mu=r_mu[0].reshape(1, -1), w0=r_w0[0].reshape(1, -1), w2=r_w2[0].astype(BF16),
              a0=r_a0[0].reshape(1, -1), a2=r_a2[0].astype(BF16), g2=r_g2[0].astype(BF16),
              kk=r_kk[0].reshape(1, -1), ka=r_ka[0].reshape(1, -1), rk=r_rk[0].reshape(1, -1),
              gn_w=r_gn_w[0].reshape(1, -1), gn_b=r_gn_b[0].reshape(1, -1))
    mp = dict(conv_w=m_conv_w[0], conv_b=m_conv_b[0].reshape(1, -1), wq=m_wq[0].astype(BF16),
              wk=m_wk[0].astype(BF16), i_b=_pad_lanes(m_i_b[0], 0), f_b=_pad_lanes(m_f_b[0], M_HEADS),
              gn_w=m_gn_w[0].reshape(1, -1), skip=m_skip[0].reshape(1, -1))
    op = dict(r_up=r_up[0].astype(BF16), m_up=m_up[0].astype(BF16), w_out=w_out[0].astype(BF16),
              g_ffn=norm_ffn_g[0].reshape(1, -1), w1=ffn_w1[0].astype(BF16), w2=ffn_w2[0].astype(BF16),
              g_fin=norm_final_g.reshape(1, -1))

    xp = x_prompt.reshape(nbp * seq, D_MODEL)
    pr, pm, pg = _proj(xp, g_mix, w_pad, gb, PROJ_TM)
    nt = seq // SEQ_TT
    y_r, wkv_p = _rwkv_seq(pr, rp, nbp, nt, SEQ_TT, RWKV_HI)
    y_m, c_p, n_p, m_p = _mlstm_seq(pm, mp, nbp, nt, SEQ_TT)
    y_p = _out(xp, pg, y_r, y_m, op, OUT_TM).reshape(nbp, seq, D_MODEL)
    conv_p = pm.reshape(nbp, seq, M_COLS_PAD)[:, seq - (CONV_W - 1):, :M_WIDTH]

    xn_s, xn_last, pr_s, pm_s, pg_s, prev_s = _proj_step(
        x_sample[:, 0], state_rwkv_shift[0], x_prompt[:, seq - 1], g_mix, w_pad, gb)
    yr_s, wkv_s = _rwkv_step(pr_s, prev_s, state_rwkv_wkv[0], rp, STEP_BB)
    conv0 = state_mlstm_conv[0].reshape(nbs, (CONV_W - 1) * M_WIDTH)
    ym_s, c_s, n_s, m_s, conv_s = _mlstm_step(pm_s, conv0, state_mlstm_C[0],
                                              state_mlstm_n[0].reshape(nbs, M_WIDTH),
                                              state_mlstm_m[0], mp, STEP_BB)
    y_s = _out(x_sample[:, 0], pg_s, yr_s, ym_s, op, nbs).reshape(nbs, 1, D_MODEL)

    return (y_p, y_s,
            xn_last[None], wkv_p[None], c_p[None], n_p[:, :, 0, :][None], m_p[:, :, 0, 0][None], conv_p[None],
            xn_s[None], wkv_s[None], c_s[None], n_s.reshape(nbs, M_HEADS, M_HEAD_DIM)[None], m_s[None],
            conv_s.reshape(nbs, CONV_W - 1, M_WIDTH)[None])
```

```python
import functools
import math

import jax
import jax.numpy as jnp
from jax import lax
from jax.experimental import pallas as pl
from jax.experimental.pallas import tpu as pltpu

F32 = jnp.float32
BF16 = jnp.bfloat16
HIGHEST = lax.Precision.HIGHEST

D_MODEL = 1024
R_HEADS = 8
R_HEAD_DIM = 64
R_WIDTH = R_HEADS * R_HEAD_DIM
W_LORA = 64
A_LORA = 64
G_LORA = 128
R_GN_EPS = 64e-5
M_HEADS = 4
M_HEAD_DIM = 128
M_WIDTH = M_HEADS * M_HEAD_DIM
CONV_W = 4
M_GN_EPS = 1e-5
D_FF = 4 * D_MODEL
RMS_EPS = 1e-6
R_SHIFT_COLS = 3 * R_WIDTH + W_LORA + A_LORA + G_LORA
M_COLS = 3 * M_WIDTH + 2 * M_HEADS
GATE_COLS = 2 * D_MODEL

LANES = 128
SUBLANES = 8
CHUNK = 64
M_COLS_PAD = -(-M_COLS // LANES) * LANES
IN_COLS_PAD = R_SHIFT_COLS + M_COLS_PAD + GATE_COLS
VMEM_LIMIT = 56 * 1024 * 1024


def _mm(a, b, dims="nn", hi=False):
    ca = 1 if dims[0] == "n" else 0
    cb = 0 if dims[1] == "n" else 1
    dn = (((ca,), (cb,)), ((), ()))
    if hi:
        return lax.dot_general(a, b, dn, precision=HIGHEST, preferred_element_type=F32)
    return lax.dot_general(a.astype(BF16), b.astype(BF16), dn, preferred_element_type=F32)


def _mm_split(a, b, dims="nn"):
    a_hi = a.astype(BF16)
    b_hi = b.astype(BF16)
    a_lo = a - a_hi.astype(F32)
    b_lo = b - b_hi.astype(F32)
    return _mm(a_hi, b_hi, dims) + (_mm(a_hi, b_lo, dims) + _mm(a_lo, b_hi, dims))


def _mm_mask_lhs(mask, b):
    b_hi = b.astype(BF16)
    rem = b - b_hi.astype(F32)
    b_mid = rem.astype(BF16)
    b_lo = rem - b_mid.astype(F32)
    return _mm(mask, b_hi) + (_mm(mask, b_mid) + _mm(mask, b_lo))


def _rms(x, g):
    return x * lax.rsqrt(jnp.mean(x * x, axis=-1, keepdims=True) + RMS_EPS) * g


def _sigmoid(x):
    return 1.0 / (1.0 + jnp.exp(-x))


def _log_sigmoid(x):
    return jnp.minimum(x, 0.0) - jnp.log(1.0 + jnp.exp(-jnp.abs(x)))


def _iota2(shape, axis):
    return lax.broadcasted_iota(jnp.int32, shape, axis)


def _const_spec(shape):
    nd = len(shape)
    return pl.BlockSpec(shape, lambda *_: (0,) * nd, pipeline_mode=pl.Buffered(1))


def _proj_body(x_ref, g_ref, w_ref, gb_ref, pr_ref, pm_ref, pg_ref):
    xb = _rms(x_ref[...], g_ref[...]).astype(BF16)
    pr_ref[...] = jnp.dot(xb, w_ref[:, 0:R_SHIFT_COLS], preferred_element_type=F32)
    pm_ref[...] = jnp.dot(xb, w_ref[:, R_SHIFT_COLS:R_SHIFT_COLS + M_COLS_PAD], preferred_element_type=F32)
    pg_ref[...] = jnp.dot(xb, w_ref[:, R_SHIFT_COLS + M_COLS_PAD:], preferred_element_type=F32) + gb_ref[...]


def _proj(x2, g, w_pad, gate_b, tm):
    m = x2.shape[0]
    return pl.pallas_call(
        _proj_body,
        grid=(m // tm,),
        in_specs=[pl.BlockSpec((tm, D_MODEL), lambda i: (i, 0)),
                  _const_spec((1, D_MODEL)),
                  _const_spec((D_MODEL, IN_COLS_PAD)),
                  _const_spec((1, GATE_COLS))],
        out_specs=[pl.BlockSpec((tm, R_SHIFT_COLS), lambda i: (i, 0)),
                   pl.BlockSpec((tm, M_COLS_PAD), lambda i: (i, 0)),
                   pl.BlockSpec((tm, GATE_COLS), lambda i: (i, 0))],
        out_shape=[jax.ShapeDtypeStruct((m, R_SHIFT_COLS), F32),
                   jax.ShapeDtypeStruct((m, M_COLS_PAD), F32),
                   jax.ShapeDtypeStruct((m, GATE_COLS), F32)],
        compiler_params=pltpu.CompilerParams(dimension_semantics=("arbitrary",), vmem_limit_bytes=VMEM_LIMIT),
        name="proj",
    )(x2, g, w_pad, gate_b)


def _proj_step_body(xs_ref, sh_ref, xl_ref, g_ref, w_ref, gb_ref,
                    xns_ref, xnl_ref, pr_ref, pm_ref, pg_ref, prev_ref):
    g = g_ref[...]
    xn = _rms(xs_ref[...], g)
    xns_ref[...] = xn
    xnl_ref[...] = _rms(xl_ref[...], g)
    xb = xn.astype(BF16)
    pr_ref[...] = jnp.dot(xb, w_ref[:, 0:R_SHIFT_COLS], preferred_element_type=F32)
    pm_ref[...] = jnp.dot(xb, w_ref[:, R_SHIFT_COLS:R_SHIFT_COLS + M_COLS_PAD], preferred_element_type=F32)
    pg_ref[...] = jnp.dot(xb, w_ref[:, R_SHIFT_COLS + M_COLS_PAD:], preferred_element_type=F32) + gb_ref[...]
    prev_ref[...] = jnp.dot(sh_ref[...].astype(BF16), w_ref[:, 0:R_SHIFT_COLS], preferred_element_type=F32)


def _proj_step(xs, shift0, xlast, g, w_pad, gate_b):
    nb = xs.shape[0]
    nl = xlast.shape[0]
    return pl.pallas_call(
        _proj_step_body,
        out_shape=[jax.ShapeDtypeStruct((nb, D_MODEL), F32),
                   jax.ShapeDtypeStruct((nl, D_MODEL), F32),
                   jax.ShapeDtypeStruct((nb, R_SHIFT_COLS), F32),
                   jax.ShapeDtypeStruct((nb, M_COLS_PAD), F32),
                   jax.ShapeDtypeStruct((nb, GATE_COLS), F32),
                   jax.ShapeDtypeStruct((nb, R_SHIFT_COLS), F32)],
        compiler_params=pltpu.CompilerParams(vmem_limit_bytes=VMEM_LIMIT),
        name="proj_step",
    )(xs, shift0, xlast, g, w_pad, gate_b)


LOG_DECAY_SCALE = -math.exp(-0.5)


def _rwkv_mix(pr, prev, mu, w0, w2, a0, a2, g2, r_kk, r_ka):
    mixed = pr + (prev - pr) * mu
    r = mixed[:, 0:R_WIDTH]
    k = mixed[:, R_WIDTH:2 * R_WIDTH]
    v = mixed[:, 2 * R_WIDTH:3 * R_WIDTH]
    xw = mixed[:, 3 * R_WIDTH:3 * R_WIDTH + W_LORA]
    xa = mixed[:, 3 * R_WIDTH + W_LORA:3 * R_WIDTH + W_LORA + A_LORA]
    xg = mixed[:, 3 * R_WIDTH + W_LORA + A_LORA:]
    w = w0 + _mm(jnp.tanh(xw), w2)
    logw = LOG_DECAY_SCALE * _sigmoid(w)
    a = _sigmoid(a0 + _mm(xa, a2))
    g = _mm(_sigmoid(xg), g2)
    kk0 = k * r_kk
    k2 = k * (1.0 + (a - 1.0) * r_ka)
    return r, k2, v, logw, a, g, kk0


def _rwkv_seq_body(pr_ref, mu_ref, w0_ref, w2_ref, a0_ref, a2_ref, g2_ref, kk_ref, ka_ref, rk_ref,
                   gnw_ref, gnb_ref, y_ref, s_out_ref, s_scr, carry_scr, *, tt, hi):
    t_idx = pl.program_id(1)

    @pl.when(t_idx == 0)
    def _():
        s_scr[...] = jnp.zeros_like(s_scr)
        carry_scr[...] = jnp.zeros_like(carry_scr)

    pr = pr_ref[...]
    row = _iota2(pr.shape, 0)
    prev = jnp.where(row == 0, carry_scr[...], pltpu.roll(pr, 1, axis=0))
    carry_scr[...] = pr[tt - 1:tt, :]

    r, k2, v, logw, a, g, kk0 = _rwkv_mix(pr, prev, mu_ref[...], w0_ref[...], w2_ref[...], a0_ref[...],
                                          a2_ref[...], g2_ref[...], kk_ref[...], ka_ref[...])
    rk = r * k2 * rk_ref[...]
    gnw = gnw_ref[...]
    gnb = gnb_ref[...]

    ri = _iota2((CHUNK, CHUNK), 0)
    ci = _iota2((CHUNK, CHUNK), 1)
    low_incl = ci <= ri
    low_strict = ci < ri
    eye = (ci == ri).astype(F32)
    tril = low_incl.astype(F32)

    nc = tt // CHUNK
    cum = jnp.concatenate([_mm_mask_lhs(tril, logw[c * CHUNK:(c + 1) * CHUNK]) for c in range(nc)], axis=0)
    cum_last = jnp.concatenate(
        [jnp.broadcast_to(cum[(c + 1) * CHUNK - 1:(c + 1) * CHUNK, :], (CHUNK, R_WIDTH)) for c in range(nc)], axis=0)
    d_inv = jnp.exp(-cum)
    d_end = jnp.exp(cum_last - cum)
    d_last = jnp.exp(cum_last)
    kka = kk0 * a
    at0 = kk0 * jnp.exp(cum - logw)
    rt = r * jnp.exp(cum)
    bt0 = kka * d_inv
    kt = k2 * d_inv
    bh0 = kka * d_end
    kh = k2 * d_end

    items = [(c, h) for c in range(nc) for h in range(R_HEADS)]
    blk = lambda x, c, h: x[c * CHUNK:(c + 1) * CHUNK, h * R_HEAD_DIM:(h + 1) * R_HEAD_DIM]
    each = lambda f, *ls: [f(*xs) for xs in zip(*ls)]
    kk_b = [blk(kk0, c, h) for c, h in items]
    inv_n = [1.0 / jnp.maximum(jnp.sqrt(jnp.sum(x * x, axis=1, keepdims=True)), 1e-12) for x in kk_b]
    a_ = [blk(at0, c, h) * n for (c, h), n in zip(items, inv_n)]
    b_ = [blk(bt0, c, h) * n for (c, h), n in zip(items, inv_n)]
    bh_ = [blk(bh0, c, h) * n for (c, h), n in zip(items, inv_n)]
    r_ = [blk(rt, c, h) for c, h in items]
    k_ = [blk(kt, c, h) for c, h in items]
    kh_ = [blk(kh, c, h) for c, h in items]
    v_ = [blk(v, c, h) for c, h in items]
    p = each(lambda a1, r1, b1, k1: _mm(jnp.concatenate([a1, r1], axis=0), jnp.concatenate([b1, k1], axis=0),
                                        "nt", hi), a_, r_, b_, k_)
    t_ab = [jnp.where(low_strict, x[:CHUNK, :CHUNK], 0.0) for x in p]
    t_ak = [jnp.where(low_strict, x[:CHUNK, CHUNK:], 0.0) for x in p]
    m_rb = [jnp.where(low_incl, x[CHUNK:, :CHUNK], 0.0) for x in p]
    m_rk = [jnp.where(low_incl, x[CHUNK:, CHUNK:], 0.0) for x in p]
    q_pow = each(lambda t: _mm(t, t, "nn", hi), t_ab)
    x_inv = [eye - t for t in t_ab]
    for _ in range(4):
        z = each(lambda q, x: _mm(q, jnp.concatenate([x, q], axis=1), "nn", hi), q_pow, x_inv)
        x_inv = [x + zz[:, :CHUNK] for x, zz in zip(x_inv, z)]
        q_pow = [zz[:, CHUNK:] for zz in z]
    x_inv = each(lambda q, x: x + _mm(q, x, "nn", hi), q_pow, x_inv)
    tv = each(lambda t, vv: _mm(t, vv, "nn", hi), t_ak, v_)
    xw = each(lambda x, a1, t: _mm(x, jnp.concatenate([a1, t], axis=1), "nn", hi), x_inv, a_, tv)
    mw = each(lambda m1, x: _mm(m1, x, "nn", hi), m_rb, xw)
    q_eff = [r1 - m1[:, :CHUNK] for r1, m1 in zip(r_, mw)]
    y_loc = each(lambda m1, vv, m2: _mm(m1, vv, "nn", hi) - m2[:, CHUNK:], m_rk, v_, mw)
    g_tr = [eye * blk(d_last, c, h) - _mm(x[:, :CHUNK], b1, "tn", hi)
            for (c, h), x, b1 in zip(items, xw, bh_)]
    h_tr = each(lambda vv, k1, x, b1: _mm(vv, k1, "tn", hi) - _mm(x[:, CHUNK:], b1, "tn", hi),
                v_, kh_, xw, bh_)

    state = [s_scr[h] for h in range(R_HEADS)]
    ys = []
    for i, (c, h) in enumerate(items):
        ys.append(_mm(q_eff[i], state[h], "nt", hi) + y_loc[i])
        state[h] = _mm(state[h], g_tr[i], "nn", hi) + h_tr[i]
    for h in range(R_HEADS):
        s_scr[h] = state[h]

    outs = []
    for i, (c, h) in enumerate(items):
        hs = slice(h * R_HEAD_DIM, (h + 1) * R_HEAD_DIM)
        yc = ys[i] - jnp.mean(ys[i], axis=1, keepdims=True)
        var = jnp.mean(yc * yc, axis=1, keepdims=True)
        bonus = jnp.sum(blk(rk, c, h), axis=1, keepdims=True)
        outs.append(yc * lax.rsqrt(var + R_GN_EPS) * gnw[:, hs] + gnb[:, hs] + bonus * v_[i])
    y_ref[...] = jnp.concatenate(
        [jnp.concatenate(outs[c * R_HEADS:(c + 1) * R_HEADS], axis=1) for c in range(nc)], axis=0) * g

    @pl.when(t_idx == pl.num_programs(1) - 1)
    def _():
        s_out_ref[0] = s_scr[...]


def _rwkv_seq(pr, p, nb, nt, tt, hi):
    m = pr.shape[0]
    row = lambda n: _const_spec((1, n))
    return pl.pallas_call(
        functools.partial(_rwkv_seq_body, tt=tt, hi=hi),
        grid=(nb, nt),
        in_specs=[pl.BlockSpec((tt, R_SHIFT_COLS), lambda b, t: (b * nt + t, 0)),
                  row(R_SHIFT_COLS), row(R_WIDTH), _const_spec((W_LORA, R_WIDTH)), row(R_WIDTH),
                  _const_spec((A_LORA, R_WIDTH)), _const_spec((G_LORA, R_WIDTH)), row(R_WIDTH), row(R_WIDTH),
                  row(R_WIDTH), row(R_WIDTH), row(R_WIDTH)],
        out_specs=[pl.BlockSpec((tt, R_WIDTH), lambda b, t: (b * nt + t, 0)),
                   pl.BlockSpec((1, R_HEADS, R_HEAD_DIM, R_HEAD_DIM), lambda b, t: (b, 0, 0, 0))],
        out_shape=[jax.ShapeDtypeStruct((m, R_WIDTH), F32),
                   jax.ShapeDtypeStruct((nb, R_HEADS, R_HEAD_DIM, R_HEAD_DIM), F32)],
        scratch_shapes=[pltpu.VMEM((R_HEADS, R_HEAD_DIM, R_HEAD_DIM), F32),
                        pltpu.VMEM((1, R_SHIFT_COLS), F32)],
        compiler_params=pltpu.CompilerParams(dimension_semantics=("arbitrary", "arbitrary"),
                                             vmem_limit_bytes=VMEM_LIMIT),
        name="rwkv_seq",
    )(pr, p["mu"], p["w0"], p["w2"], p["a0"], p["a2"], p["g2"], p["kk"], p["ka"], p["rk"],
      p["gn_w"], p["gn_b"])


def _rwkv_step_body(pr_ref, prev_ref, s_ref, mu_ref, w0_ref, w2_ref, a0_ref, a2_ref, g2_ref, kk_ref, ka_ref,
                    rk_ref, gnw_ref, gnb_ref, y_ref, s_out_ref, *, bb):
    r, k2, v, logw, a, g, kk0 = _rwkv_mix(pr_ref[...], prev_ref[...], mu_ref[...], w0_ref[...], w2_ref[...],
                                          a0_ref[...], a2_ref[...], g2_ref[...], kk_ref[...], ka_ref[...])
    decay = jnp.exp(logw)
    dr = decay * r
    rk = r * k2 * rk_ref[...]
    gnw = gnw_ref[...]
    gnb = gnb_ref[...]
    pad = jnp.zeros((SUBLANES - 2, R_HEAD_DIM), F32)
    heads = [slice(h * R_HEAD_DIM, (h + 1) * R_HEAD_DIM) for h in range(R_HEADS)]
    rows = [slice(b, b + 1) for b in range(bb)]
    kk_n = []
    for hs in heads:
        x = kk0[:, hs]
        kk_n.append(x / jnp.maximum(jnp.sqrt(jnp.sum(x * x, axis=1, keepdims=True)), 1e-12))
    kka = [x * a[:, hs] for x, hs in zip(kk_n, heads)]
    s0 = [[s_ref[b, h] for b in range(bb)] for h in range(R_HEADS)]
    red = [[_mm_split(jnp.concatenate([kk_n[h][bs], dr[bs, heads[h]]], axis=0), s0[h][b], "nt")
            for b, bs in enumerate(rows)] for h in range(R_HEADS)]
    s_kk = [jnp.concatenate([x[0:1] for x in red[h]], axis=0) for h in range(R_HEADS)]
    y_dec = [jnp.concatenate([x[1:2] for x in red[h]], axis=0) for h in range(R_HEADS)]
    upd = [[_mm_split(jnp.concatenate([-s_kk[h][bs], v[bs, heads[h]], pad], axis=0),
                      jnp.concatenate([kka[h][bs], k2[bs, heads[h]], pad], axis=0), "tn")
            for bs in rows] for h in range(R_HEADS)]
    for h in range(R_HEADS):
        for b in range(bb):
            s_out_ref[b, h] = s0[h][b] * decay[rows[b], heads[h]] + upd[h][b]
    outs = []
    for h, hs in enumerate(heads):
        r_h, k_h, v_h = r[:, hs], k2[:, hs], v[:, hs]
        y = (y_dec[h] - s_kk[h] * jnp.sum(kka[h] * r_h, axis=1, keepdims=True)
             + v_h * jnp.sum(k_h * r_h, axis=1, keepdims=True))
        yc = y - jnp.mean(y, axis=1, keepdims=True)
        var = jnp.mean(yc * yc, axis=1, keepdims=True)
        bonus = jnp.sum(rk[:, hs], axis=1, keepdims=True)
        outs.append(yc * lax.rsqrt(var + R_GN_EPS) * gnw[:, hs] + gnb[:, hs] + bonus * v_h)
    y_ref[...] = jnp.concatenate(outs, axis=1) * g


def _rwkv_step(pr, prev, s0, p, bb):
    nb = pr.shape[0]
    row = lambda n: _const_spec((1, n))
    state_spec = pl.BlockSpec((bb, R_HEADS, R_HEAD_DIM, R_HEAD_DIM), lambda i: (i, 0, 0, 0))
    return pl.pallas_call(
        functools.partial(_rwkv_step_body, bb=bb),
        grid=(nb // bb,),
        in_specs=[pl.BlockSpec((bb, R_SHIFT_COLS), lambda i: (i, 0)),
                  pl.BlockSpec((bb, R_SHIFT_COLS), lambda i: (i, 0)),
                  state_spec,
                  row(R_SHIFT_COLS), row(R_WIDTH), _const_spec((W_LORA, R_WIDTH)), row(R_WIDTH),
                  _const_spec((A_LORA, R_WIDTH)), _const_spec((G_LORA, R_WIDTH)), row(R_WIDTH), row(R_WIDTH),
                  row(R_WIDTH), row(R_WIDTH), row(R_WIDTH)],
        out_specs=[pl.BlockSpec((bb, R_WIDTH), lambda i: (i, 0)), state_spec],
        out_shape=[jax.ShapeDtypeStruct((nb, R_WIDTH), F32),
                   jax.ShapeDtypeStruct(s0.shape, F32)],
        compiler_params=pltpu.CompilerParams(dimension_semantics=("arbitrary",), vmem_limit_bytes=VMEM_LIMIT),
        name="rwkv_step",
    )(pr, prev, s0, p["mu"], p["w0"], p["w2"], p["a0"], p["a2"], p["g2"], p["kk"], p["ka"], p["rk"],
      p["gn_w"], p["gn_b"])


GATE_LANE0 = 3 * M_WIDTH


def _head_norm_rows(x, eps):
    xc = x - jnp.mean(x, axis=1, keepdims=True)
    var = jnp.mean(xc * xc, axis=1, keepdims=True)
    return xc * lax.rsqrt(var + eps)


def _mlstm_seq_body(pm_ref, cw_ref, cb_ref, wq_ref, wk_ref, ib_ref, fb_ref, gnw_ref, skip_ref,
                    y_ref, c_out_ref, n_out_ref, m_out_ref, c_scr, n_scr, m_scr, carry_scr, *, tt):
    t_idx = pl.program_id(1)

    @pl.when(t_idx == 0)
    def _():
        c_scr[...] = jnp.zeros_like(c_scr)
        n_scr[...] = jnp.zeros_like(n_scr)
        m_scr[...] = jnp.zeros_like(m_scr)
        carry_scr[...] = jnp.zeros_like(carry_scr)

    xm = pm_ref[:, 0:M_WIDTH]
    carry = carry_scr[...]
    carry_scr[...] = xm[tt - SUBLANES:tt, :]
    row8 = _iota2((SUBLANES, M_WIDTH), 0)
    cw = cw_ref[...]
    xc = cb_ref[...] + xm * cw[CONV_W - 1:CONV_W, :]
    for s in range(1, CONV_W):
        rolled = pltpu.roll(xm, s, axis=0)
        top = jnp.where(row8 < s, pltpu.roll(carry, s, axis=0), rolled[0:SUBLANES])
        shifted = jnp.concatenate([top, rolled[SUBLANES:]], axis=0)
        xc = xc + shifted * cw[CONV_W - 1 - s:CONV_W - s, :]
    xc = xc * _sigmoid(xc)

    gt = pm_ref[:, GATE_LANE0:GATE_LANE0 + LANES]
    li_all = gt + ib_ref[...]
    lf_all = _log_sigmoid(gt + fb_ref[...])

    ri = _iota2((CHUNK, CHUNK), 0)
    ci = _iota2((CHUNK, CHUNK), 1)
    causal = ci <= ri
    tril = causal.astype(F32)
    gnw = gnw_ref[...]
    skip = skip_ref[...]

    bc_all = jnp.concatenate(
        [_mm_mask_lhs(tril, lf_all[c * CHUNK:(c + 1) * CHUNK]) for c in range(tt // CHUNK)], axis=0)
    li_t = li_all.T
    bc_t = bc_all.T

    nc = tt // CHUNK
    items = [(c, h) for c in range(nc) for h in range(M_HEADS)]
    rows = lambda c: slice(c * CHUNK, (c + 1) * CHUNK)
    lanes = lambda h: slice(h * M_HEAD_DIM, (h + 1) * M_HEAD_DIM)
    xc_b = [xc[rows(c), lanes(h)] for c, h in items]
    q = [_mm(x, wq_ref[h]) * (M_HEAD_DIM ** -0.5) for (c, h), x in zip(items, xc_b)]
    k = [_mm(x, wk_ref[h]) for (c, h), x in zip(items, xc_b)]
    vv = [pm_ref[rows(c), M_WIDTH + h * M_HEAD_DIM:M_WIDTH + (h + 1) * M_HEAD_DIM] for c, h in items]
    li_col = [li_all[rows(c), h:h + 1] for c, h in items]
    bc_col = [bc_all[rows(c), M_HEADS + h:M_HEADS + h + 1] for c, h in items]
    dmat = [jnp.where(causal, (bcc - bc_t[M_HEADS + h:M_HEADS + h + 1, rows(c)]) + li_t[h:h + 1, rows(c)], -jnp.inf)
            for (c, h), bcc in zip(items, bc_col)]
    d_max = [jnp.max(d, axis=1, keepdims=True) for d in dmat]
    b_last = [x[CHUNK - 1:CHUNK, :] for x in bc_col]
    lw_end = [(bl - bcc) + lic for bl, bcc, lic in zip(b_last, bc_col, li_col)]
    lw_max = [jnp.max(x, axis=0, keepdims=True) for x in lw_end]

    m_run = [m_scr[h][0:1, 0:1] for h in range(M_HEADS)]
    m_t, g_st, g_end, m_new = [], [], [], []
    for i, (c, h) in enumerate(items):
        g_st.append(bc_col[i] + m_run[h])
        m_t.append(jnp.maximum(g_st[i], d_max[i]))
        g_end.append(b_last[i] + m_run[h])
        m_new.append(jnp.maximum(g_end[i], lw_max[i]))
        m_run[h] = m_new[i]

    w_st = [jnp.exp(g - m) for g, m in zip(g_st, m_t)]
    s = [_mm(qq, kk, "nt") * jnp.exp(d - m) for qq, kk, d, m in zip(q, k, dmat, m_t)]
    sv = [_mm(ss, v1) for ss, v1 in zip(s, vv)]
    s_sum = [jnp.sum(ss, axis=1, keepdims=True) for ss in s]
    we = [jnp.exp(lw - m) for lw, m in zip(lw_end, m_new)]
    ge = [jnp.exp(g - m) for g, m in zip(g_end, m_new)]
    kv = [_mm(kk, w1 * v1, "tn") for kk, w1, v1 in zip(k, we, vv)]
    k_sum = [jnp.sum(w1 * kk, axis=0, keepdims=True) for w1, kk in zip(we, k)]

    c_run = [c_scr[h] for h in range(M_HEADS)]
    n_run = [n_scr[h][0:1, :] for h in range(M_HEADS)]
    num, den = [], []
    for i, (c, h) in enumerate(items):
        num.append(w_st[i] * _mm(q[i], c_run[h]) + sv[i])
        den.append(w_st[i] * jnp.sum(q[i] * n_run[h], axis=1, keepdims=True) + s_sum[i])
        c_run[h] = ge[i] * c_run[h] + kv[i]
        n_run[h] = ge[i] * n_run[h] + k_sum[i]
    for h in range(M_HEADS):
        c_scr[h] = c_run[h]
        n_scr[h] = jnp.broadcast_to(n_run[h], (SUBLANES, M_HEAD_DIM))
        m_scr[h] = jnp.broadcast_to(m_run[h], (SUBLANES, LANES))

    for i, (c, h) in enumerate(items):
        hh = num[i] / jnp.maximum(jnp.abs(den[i]), jnp.exp(-m_t[i]))
        hn = _head_norm_rows(hh, M_GN_EPS) * gnw[:, lanes(h)] + skip[:, lanes(h)] * xc_b[i]
        o = pm_ref[rows(c), 2 * M_WIDTH + h * M_HEAD_DIM:2 * M_WIDTH + (h + 1) * M_HEAD_DIM]
        y_ref[rows(c), lanes(h)] = _sigmoid(o) * hn

    @pl.when(t_idx == pl.num_programs(1) - 1)
    def _():
        c_out_ref[0] = c_scr[...]
        n_out_ref[0] = n_scr[...]
        m_out_ref[0] = m_scr[...]


def _mlstm_seq(pm, p, nb, nt, tt):
    m = pm.shape[0]
    row = lambda n: _const_spec((1, n))
    return pl.pallas_call(
        functools.partial(_mlstm_seq_body, tt=tt),
        grid=(nb, nt),
        in_specs=[pl.BlockSpec((tt, M_COLS_PAD), lambda b, t: (b * nt + t, 0)),
                  _const_spec((CONV_W, M_WIDTH)), row(M_WIDTH),
                  _const_spec((M_HEADS, M_HEAD_DIM, M_HEAD_DIM)), _const_spec((M_HEADS, M_HEAD_DIM, M_HEAD_DIM)),
                  row(LANES), row(LANES), row(M_WIDTH), row(M_WIDTH)],
        out_specs=[pl.BlockSpec((tt, M_WIDTH), lambda b, t: (b * nt + t, 0)),
                   pl.BlockSpec((1, M_HEADS, M_HEAD_DIM, M_HEAD_DIM), lambda b, t: (b, 0, 0, 0)),
                   pl.BlockSpec((1, M_HEADS, SUBLANES, M_HEAD_DIM), lambda b, t: (b, 0, 0, 0)),
                   pl.BlockSpec((1, M_HEADS, SUBLANES, LANES), lambda b, t: (b, 0, 0, 0))],
        out_shape=[jax.ShapeDtypeStruct((m, M_WIDTH), F32),
                   jax.ShapeDtypeStruct((nb, M_HEADS, M_HEAD_DIM, M_HEAD_DIM), F32),
                   jax.ShapeDtypeStruct((nb, M_HEADS, SUBLANES, M_HEAD_DIM), F32),
                   jax.ShapeDtypeStruct((nb, M_HEADS, SUBLANES, LANES), F32)],
        scratch_shapes=[pltpu.VMEM((M_HEADS, M_HEAD_DIM, M_HEAD_DIM), F32),
                        pltpu.VMEM((M_HEADS, SUBLANES, M_HEAD_DIM), F32),
                        pltpu.VMEM((M_HEADS, SUBLANES, LANES), F32),
                        pltpu.VMEM((SUBLANES, M_WIDTH), F32)],
        compiler_params=pltpu.CompilerParams(dimension_semantics=("arbitrary", "arbitrary"),
                                             vmem_limit_bytes=VMEM_LIMIT),
        name="mlstm_seq",
    )(pm, p["conv_w"], p["conv_b"], p["wq"], p["wk"], p["i_b"], p["f_b"], p["gn_w"], p["skip"])


def _mlstm_step_body(pm_ref, conv_ref, c_ref, n_ref, m_ref, cw_ref, cb_ref, wq_ref, wk_ref, ib_ref, fb_ref,
                     gnw_ref, skip_ref, y_ref, c_out_ref, n_out_ref, m_out_ref, conv_out_ref, hv_scr, *, bb):
    xm = pm_ref[:, 0:M_WIDTH]
    cw = cw_ref[...]
    xc = cb_ref[...] + xm * cw[CONV_W - 1:CONV_W, :]
    for j in range(CONV_W - 1):
        xc = xc + conv_ref[:, j * M_WIDTH:(j + 1) * M_WIDTH] * cw[j:j + 1, :]
    xc = xc * _sigmoid(xc)
    conv_out_ref[:, 0:(CONV_W - 2) * M_WIDTH] = conv_ref[:, M_WIDTH:(CONV_W - 1) * M_WIDTH]
    conv_out_ref[:, (CONV_W - 2) * M_WIDTH:] = xm

    gt = pm_ref[:, GATE_LANE0:GATE_LANE0 + LANES]
    li = (gt + ib_ref[...])[:, 0:M_HEADS]
    lf = _log_sigmoid(gt + fb_ref[...])[:, M_HEADS:2 * M_HEADS]
    m_prev = m_ref[...]
    g_st = lf + m_prev
    m_t = jnp.maximum(g_st, (lf - lf) + li)
    w_in = jnp.exp(((lf - lf) + li) - m_t)
    w_st = jnp.exp(g_st - m_t)
    floor = jnp.exp(-m_t)
    m_out_ref[...] = m_t

    ri = _iota2((M_HEAD_DIM, M_HEAD_DIM), 0)
    ci = _iota2((M_HEAD_DIM, M_HEAD_DIM), 1)
    eye = ri == ci
    for h in range(M_HEADS):
        hs = slice(h * M_HEAD_DIM, (h + 1) * M_HEAD_DIM)
        xc_h = xc[:, hs]
        q = _mm(xc_h, wq_ref[h]) * (M_HEAD_DIM ** -0.5)
        k = _mm(xc_h, wk_ref[h])
        vv = pm_ref[:, M_WIDTH + h * M_HEAD_DIM:M_WIDTH + (h + 1) * M_HEAD_DIM]
        n0 = n_ref[:, hs]
        qk = jnp.sum(q * k, axis=1, keepdims=True)
        qn = jnp.sum(q * n0, axis=1, keepdims=True)
        s = qk * w_in[:, h:h + 1]
        den = w_st[:, h:h + 1] * qn + s
        inv = 1.0 / jnp.maximum(jnp.abs(den), floor[:, h:h + 1])
        n_out_ref[:, hs] = w_st[:, h:h + 1] * n0 + w_in[:, h:h + 1] * k
        for b in range(bb):
            bs = slice(b, b + 1)
            c0 = c_ref[b, h]
            q_col = jnp.sum(jnp.where(eye, q[bs], 0.0), axis=1, keepdims=True)
            k_col = jnp.sum(jnp.where(eye, k[bs], 0.0), axis=1, keepdims=True)
            qc = jnp.sum(c0 * q_col, axis=0, keepdims=True)
            ge = w_st[bs, h:h + 1]
            we = w_in[bs, h:h + 1]
            c_out_ref[b, h] = ge * c0 + (we * k_col) * vv[bs]
            hv_scr[bs, hs] = (ge * qc + s[bs] * vv[bs]) * inv[bs]
    hv = hv_scr[...]
    gnw = gnw_ref[...]
    skip = skip_ref[...]
    outs = []
    for h in range(M_HEADS):
        hs = slice(h * M_HEAD_DIM, (h + 1) * M_HEAD_DIM)
        outs.append(_head_norm_rows(hv[:, hs], M_GN_EPS) * gnw[:, hs] + skip[:, hs] * xc[:, hs])
    o = pm_ref[:, 2 * M_WIDTH:3 * M_WIDTH]
    y_ref[...] = _sigmoid(o) * jnp.concatenate(outs, axis=1)


def _mlstm_step(pm, conv0, c0, n0, m0, p, bb):
    nb = pm.shape[0]
    row = lambda n: _const_spec((1, n))
    conv_cols = (CONV_W - 1) * M_WIDTH
    c_spec = pl.BlockSpec((bb, M_HEADS, M_HEAD_DIM, M_HEAD_DIM), lambda i: (i, 0, 0, 0))
    n_spec = pl.BlockSpec((bb, M_WIDTH), lambda i: (i, 0))
    m_spec = pl.BlockSpec((bb, M_HEADS), lambda i: (i, 0))
    conv_spec = pl.BlockSpec((bb, conv_cols), lambda i: (i, 0))
    return pl.pallas_call(
        functools.partial(_mlstm_step_body, bb=bb),
        grid=(nb // bb,),
        in_specs=[pl.BlockSpec((bb, M_COLS_PAD), lambda i: (i, 0)), conv_spec, c_spec, n_spec, m_spec,
                  _const_spec((CONV_W, M_WIDTH)), row(M_WIDTH),
                  _const_spec((M_HEADS, M_HEAD_DIM, M_HEAD_DIM)), _const_spec((M_HEADS, M_HEAD_DIM, M_HEAD_DIM)),
                  row(LANES), row(LANES), row(M_WIDTH), row(M_WIDTH)],
        out_specs=[pl.BlockSpec((bb, M_WIDTH), lambda i: (i, 0)), c_spec, n_spec, m_spec, conv_spec],
        out_shape=[jax.ShapeDtypeStruct((nb, M_WIDTH), F32),
                   jax.ShapeDtypeStruct(c0.shape, F32),
                   jax.ShapeDtypeStruct(n0.shape, F32),
                   jax.ShapeDtypeStruct(m0.shape, F32),
                   jax.ShapeDtypeStruct(conv0.shape, F32)],
        scratch_shapes=[pltpu.VMEM((bb, M_WIDTH), F32)],
        compiler_params=pltpu.CompilerParams(dimension_semantics=("arbitrary",), vmem_limit_bytes=VMEM_LIMIT),
        name="mlstm_step",
    )(pm, conv0, c0, n0, m0, p["conv_w"], p["conv_b"], p["wq"], p["wk"], p["i_b"], p["f_b"],
      p["gn_w"], p["skip"])


def _out_body(x_ref, pg_ref, yr_ref, ym_ref, rup_ref, mup_ref, wout_ref, gffn_ref, w1_ref, w2_ref, gfin_ref,
              y_ref):
    up_r = jnp.dot(yr_ref[...].astype(BF16), rup_ref[...], preferred_element_type=F32)
    up_m = jnp.dot(ym_ref[...].astype(BF16), mup_ref[...], preferred_element_type=F32)
    merged = _sigmoid(pg_ref[:, 0:D_MODEL]) * up_r + _sigmoid(pg_ref[:, D_MODEL:]) * up_m
    x1 = x_ref[...] + jnp.dot(merged.astype(BF16), wout_ref[...], preferred_element_type=F32)
    hn = _rms(x1, gffn_ref[...]).astype(BF16)
    hid = jnp.maximum(jnp.dot(hn, w1_ref[...], preferred_element_type=F32), 0.0)
    hid = (hid * hid).astype(BF16)
    x2 = x1 + jnp.dot(hid, w2_ref[...], preferred_element_type=F32)
    y_ref[...] = _rms(x2, gfin_ref[...])


def _out(x2, pg, yr, ym, p, tm):
    m = x2.shape[0]
    tile = lambda n: pl.BlockSpec((tm, n), lambda i: (i, 0))
    return pl.pallas_call(
        _out_body,
        grid=(m // tm,),
        in_specs=[tile(D_MODEL), tile(GATE_COLS), tile(R_WIDTH), tile(M_WIDTH),
                  _const_spec((R_WIDTH, D_MODEL)), _const_spec((M_WIDTH, D_MODEL)),
                  _const_spec((D_MODEL, D_MODEL)), _const_spec((1, D_MODEL)),
                  _const_spec((D_MODEL, D_FF)), _const_spec((D_FF, D_MODEL)), _const_spec((1, D_MODEL))],
        out_specs=tile(D_MODEL),
        out_shape=jax.ShapeDtypeStruct((m, D_MODEL), F32),
        compiler_params=pltpu.CompilerParams(dimension_semantics=("arbitrary",), vmem_limit_bytes=VMEM_LIMIT),
        name="merge_ffn",
    )(x2, pg, yr, ym, p["r_up"], p["m_up"], p["w_out"], p["g_ffn"], p["w1"], p["w2"], p["g_fin"])


PROJ_TM = 512
SEQ_TT = 128
OUT_TM = 256
STEP_BB = 8
RWKV_HI = False


def _pad_lanes(v, start):
    out = jnp.zeros((1, LANES), F32)
    return lax.dynamic_update_slice(out, v.reshape(1, -1), (0, start))


def kernel(x_prompt, x_sample, state_rwkv_shift, state_rwkv_wkv, state_mlstm_C, state_mlstm_n, state_mlstm_m, state_mlstm_conv, norm_mix_g, w_in, r_mu, r_w0, r_w2, r_a0, r_a2, r_g2, r_kk, r_ka, r_rk, r_gn_w, r_gn_b, r_up, m_conv_w, m_conv_b, m_wq, m_wk, m_i_b, m_f_b, m_gn_w, m_skip, m_up, gate_b, w_out, norm_ffn_g, ffn_w1, ffn_w2, norm_final_g):
    nbp, seq, _ = x_prompt.shape
    nbs = x_sample.shape[0]
    w = w_in[0]
    w_pad = jnp.concatenate(
        [w[:, :R_SHIFT_COLS + M_COLS], jnp.zeros((D_MODEL, M_COLS_PAD - M_COLS), F32),
         w[:, R_SHIFT_COLS + M_COLS:]], axis=1).astype(BF16)
    g_mix = norm_mix_g[0].reshape(1, D_MODEL)
    gb = gate_b[0].reshape(1, GATE_COLS)
    rp = dict(mu=r_mu[0].reshape(1, -1), w0=r_w0[0].reshape(1, -1), w2=r_w2[0].astype(BF16),
              a0=r_a0[0].reshape(1, -1), a2=r_a2[0].astype(BF16), g2=r_g2[0].astype(BF16),
              kk=r_kk[0].reshape(1, -1), ka=r_ka[0].reshape(1, -1), rk=r_rk[0].reshape(1, -1),
              gn_w=r_gn_w[0].reshape(1, -1), gn_b=r_gn_b[0].reshape(1, -1))
    mp = dict(conv_w=m_conv_w[0], conv_b=m_conv_b[0].reshape(1, -1), wq=m_wq[0].astype(BF16),
              wk=m_wk[0].astype(BF16), i_b=_pad_lanes(m_i_b[0], 0), f_b=_pad_lanes(m_f_b[0], M_HEADS),
              gn_w=m_gn_w[0].reshape(1, -1), skip=m_skip[0].reshape(1, -1))
    op = dict(r_up=r_up[0].astype(BF16), m_up=m_up[0].astype(BF16), w_out=w_out[0].astype(BF16),
              g_ffn=norm_ffn_g[0].reshape(1, -1), w1=ffn_w1[0].astype(BF16), w2=ffn_w2[0].astype(BF16),
              g_fin=norm_final_g.reshape(1, -1))

    xp = x_prompt.reshape(nbp * seq, D_MODEL)
    pr, pm, pg = _proj(xp, g_mix, w_pad, gb, PROJ_TM)
    nt = seq // SEQ_TT
    y_r, wkv_p = _rwkv_seq(pr, rp, nbp, nt, SEQ_TT, RWKV_HI)
    y_m, c_p, n_p, m_p = _mlstm_seq(pm, mp, nbp, nt, SEQ_TT)
    y_p = _out(xp, pg, y_r, y_m, op, OUT_TM).reshape(nbp, seq, D_MODEL)
    conv_p = pm.reshape(nbp, seq, M_COLS_PAD)[:, seq - (CONV_W - 1):, :M_WIDTH]

    xn_s, xn_last, pr_s, pm_s, pg_s, prev_s = _proj_step(
        x_sample[:, 0], state_rwkv_shift[0], x_prompt[:, seq - 1], g_mix, w_pad, gb)
    yr_s, wkv_s = _rwkv_step(pr_s, prev_s, state_rwkv_wkv[0], rp, STEP_BB)
    conv0 = state_mlstm_conv[0].reshape(nbs, (CONV_W - 1) * M_WIDTH)
    ym_s, c_s, n_s, m_s, conv_s = _mlstm_step(pm_s, conv0, state_mlstm_C[0],
                                              state_mlstm_n[0].reshape(nbs, M_WIDTH),
                                              state_mlstm_m[0], mp, STEP_BB)
    y_s = _out(x_sample[:, 0], pg_s, yr_s, ym_s, op, nbs).reshape(nbs, 1, D_MODEL)

    return (y_p, y_s,
            xn_last[None], wkv_p[None], c_p[None], n_p[:, :, 0, :][None], m_p[:, :, 0, 0][None], conv_p[None],
            xn_s[None], wkv_s[None], c_s[None], n_s.reshape(nbs, M_HEADS, M_HEAD_DIM)[None], m_s[None],
            conv_s.reshape(nbs, CONV_W - 1, M_WIDTH)[None])
```

```python
import functools
import math

import jax
import jax.numpy as jnp
from jax import lax
from jax.experimental import pallas as pl
from jax.experimental.pallas import tpu as pltpu

F32 = jnp.float32
BF16 = jnp.bfloat16

D_MODEL = 1024
R_HEADS = 8
R_HEAD_DIM = 64
R_WIDTH = R_HEADS * R_HEAD_DIM
GROUP_HEADS = 4
R_GROUPS = R_HEADS // GROUP_HEADS
GROUP_W = GROUP_HEADS * R_HEAD_DIM
W_LORA = 64
A_LORA = 64
G_LORA = 128
R_GN_EPS = 64e-5
M_HEADS = 4
M_HEAD_DIM = 128
M_WIDTH = M_HEADS * M_HEAD_DIM
CONV_W = 4
M_GN_EPS = 1e-5
D_FF = 4 * D_MODEL
RMS_EPS = 1e-6
R_SHIFT_COLS = 3 * R_WIDTH + W_LORA + A_LORA + G_LORA
M_COLS = 3 * M_WIDTH + 2 * M_HEADS
GATE_COLS = 2 * D_MODEL

LANES = 128
SUBLANES = 8
CHUNK = 64
M_COLS_PAD = -(-M_COLS // LANES) * LANES
IN_COLS_PAD = R_SHIFT_COLS + M_COLS_PAD + GATE_COLS
VMEM_LIMIT = 56 * 1024 * 1024


def _mm(a, b, dims="nn"):
    ca = 1 if dims[0] == "n" else 0
    cb = 0 if dims[1] == "n" else 1
    dn = (((ca,), (cb,)), ((), ()))
    return lax.dot_general(a.astype(BF16), b.astype(BF16), dn, preferred_element_type=F32)


def _mm_split(a, b, dims="nn"):
    a_hi = a.astype(BF16)
    b_hi = b.astype(BF16)
    a_lo = a - a_hi.astype(F32)
    b_lo = b - b_hi.astype(F32)
    return _mm(a_hi, b_hi, dims) + (_mm(a_hi, b_lo, dims) + _mm(a_lo, b_hi, dims))


def _mm_mask_lhs(mask, b):
    b_hi = b.astype(BF16)
    return _mm(mask, b_hi) + _mm(mask, b - b_hi.astype(F32))


def _mm_mask_rhs(a, mask):
    a_hi = a.astype(BF16)
    return _mm(a_hi, mask) + _mm(a - a_hi.astype(F32), mask)


def _rms(x, g):
    return x * lax.rsqrt(jnp.mean(x * x, axis=-1, keepdims=True) + RMS_EPS) * g


def _sigmoid(x):
    return 1.0 / (1.0 + jnp.exp(-x))


def _log_sigmoid(x):
    return jnp.minimum(x, 0.0) - jnp.log(1.0 + jnp.exp(-jnp.abs(x)))


def _iota2(shape, axis):
    return lax.broadcasted_iota(jnp.int32, shape, axis)


def _const_spec(shape):
    nd = len(shape)
    return pl.BlockSpec(shape, lambda *_: (0,) * nd, pipeline_mode=pl.Buffered(1))


def _proj_body(x_ref, g_ref, w_ref, gb_ref, pr_ref, pm_ref, pg_ref):
    xb = _rms(x_ref[...], g_ref[...]).astype(BF16)
    pr_ref[...] = jnp.dot(xb, w_ref[:, 0:R_SHIFT_COLS], preferred_element_type=F32)
    pm_ref[...] = jnp.dot(xb, w_ref[:, R_SHIFT_COLS:R_SHIFT_COLS + M_COLS_PAD], preferred_element_type=F32)
    pg_ref[...] = jnp.dot(xb, w_ref[:, R_SHIFT_COLS + M_COLS_PAD:], preferred_element_type=F32) + gb_ref[...]


def _proj(x2, g, w_pad, gate_b, tm):
    m = x2.shape[0]
    return pl.pallas_call(
        _proj_body,
        grid=(m // tm,),
        in_specs=[pl.BlockSpec((tm, D_MODEL), lambda i: (i, 0)),
                  _const_spec((1, D_MODEL)),
                  _const_spec((D_MODEL, IN_COLS_PAD)),
                  _const_spec((1, GATE_COLS))],
        out_specs=[pl.BlockSpec((tm, R_SHIFT_COLS), lambda i: (i, 0)),
                   pl.BlockSpec((tm, M_COLS_PAD), lambda i: (i, 0)),
                   pl.BlockSpec((tm, GATE_COLS), lambda i: (i, 0))],
        out_shape=[jax.ShapeDtypeStruct((m, R_SHIFT_COLS), F32),
                   jax.ShapeDtypeStruct((m, M_COLS_PAD), F32),
                   jax.ShapeDtypeStruct((m, GATE_COLS), F32)],
        compiler_params=pltpu.CompilerParams(dimension_semantics=("arbitrary",), vmem_limit_bytes=VMEM_LIMIT),
        name="proj",
    )(x2, g, w_pad, gate_b)


def _proj_step_body(xs_ref, sh_ref, xl_ref, g_ref, w_ref, gb_ref,
                    xns_ref, xnl_ref, pr_ref, pm_ref, pg_ref, prev_ref):
    g = g_ref[...]
    xn = _rms(xs_ref[...], g)
    xns_ref[...] = xn
    xnl_ref[...] = _rms(xl_ref[...], g)
    xb = xn.astype(BF16)
    pr_ref[...] = jnp.dot(xb, w_ref[:, 0:R_SHIFT_COLS], preferred_element_type=F32)
    pm_ref[...] = jnp.dot(xb, w_ref[:, R_SHIFT_COLS:R_SHIFT_COLS + M_COLS_PAD], preferred_element_type=F32)
    pg_ref[...] = jnp.dot(xb, w_ref[:, R_SHIFT_COLS + M_COLS_PAD:], preferred_element_type=F32) + gb_ref[...]
    prev_ref[...] = jnp.dot(sh_ref[...].astype(BF16), w_ref[:, 0:R_SHIFT_COLS], preferred_element_type=F32)


def _proj_step(xs, shift0, xlast, g, w_pad, gate_b):
    nb = xs.shape[0]
    nl = xlast.shape[0]
    return pl.pallas_call(
        _proj_step_body,
        out_shape=[jax.ShapeDtypeStruct((nb, D_MODEL), F32),
                   jax.ShapeDtypeStruct((nl, D_MODEL), F32),
                   jax.ShapeDtypeStruct((nb, R_SHIFT_COLS), F32),
                   jax.ShapeDtypeStruct((nb, M_COLS_PAD), F32),
                   jax.ShapeDtypeStruct((nb, GATE_COLS), F32),
                   jax.ShapeDtypeStruct((nb, R_SHIFT_COLS), F32)],
        compiler_params=pltpu.CompilerParams(vmem_limit_bytes=VMEM_LIMIT),
        name="proj_step",
    )(xs, shift0, xlast, g, w_pad, gate_b)


LOG_DECAY_SCALE = -math.exp(-0.5)


def _rwkv_mix(pr, prev, mu, w0, w2, a0, a2, g2, r_kk, r_ka):
    mixed = pr + (prev - pr) * mu
    r = mixed[:, 0:R_WIDTH]
    k = mixed[:, R_WIDTH:2 * R_WIDTH]
    v = mixed[:, 2 * R_WIDTH:3 * R_WIDTH]
    xw = mixed[:, 3 * R_WIDTH:3 * R_WIDTH + W_LORA]
    xa = mixed[:, 3 * R_WIDTH + W_LORA:3 * R_WIDTH + W_LORA + A_LORA]
    xg = mixed[:, 3 * R_WIDTH + W_LORA + A_LORA:]
    w = w0 + _mm(jnp.tanh(xw), w2)
    logw = LOG_DECAY_SCALE * _sigmoid(w)
    a = _sigmoid(a0 + _mm(xa, a2))
    g = _mm(_sigmoid(xg), g2)
    kk0 = k * r_kk
    k2 = k * (1.0 + (a - 1.0) * r_ka)
    return r, k2, v, logw, a, g, kk0


def _rwkv_seq_body(pr_ref, mu_ref, w0_ref, w2_ref, a0_ref, a2_ref, g2_ref, kk_ref, ka_ref, rk_ref,
                   gnw_ref, gnb_ref, seg_ref, y_ref, s_out_ref, s_scr, carry_scr, *, tt):
    t_idx = pl.program_id(1)

    @pl.when(t_idx == 0)
    def _():
        s_scr[...] = jnp.zeros_like(s_scr)
        carry_scr[...] = jnp.zeros_like(carry_scr)

    pr = pr_ref[...]
    row = _iota2(pr.shape, 0)
    prev = jnp.where(row == 0, carry_scr[...], pltpu.roll(pr, 1, axis=0))
    carry_scr[...] = pr[tt - 1:tt, :]

    r, k2, v, logw, a, g, kk0 = _rwkv_mix(pr, prev, mu_ref[...], w0_ref[...], w2_ref[...], a0_ref[...],
                                          a2_ref[...], g2_ref[...], kk_ref[...], ka_ref[...])
    ri = _iota2((CHUNK, GROUP_W), 0)
    ci = _iota2((CHUNK, GROUP_W), 1)
    src = ci % R_HEAD_DIM
    lane_head = ci // R_HEAD_DIM
    low_incl = src <= ri
    low_strict = src < ri
    eye = (src == ri).astype(F32)
    rr = _iota2((GROUP_W, GROUP_W), 0)
    cc = _iota2((GROUP_W, GROUP_W), 1)
    same_head = (rr // R_HEAD_DIM) == (cc // R_HEAD_DIM)
    eye_bd = (rr == cc).astype(F32)
    seg = seg_ref[...]

    def seg_sum(x, mm=_mm):
        return jnp.concatenate([mm(x[:, q * GROUP_W:(q + 1) * GROUP_W], seg) for q in range(R_GROUPS)], axis=1)
    tril = (_iota2((CHUNK, CHUNK), 1) <= _iota2((CHUNK, CHUNK), 0)).astype(F32)

    nc = tt // CHUNK
    cum = jnp.concatenate([_mm_mask_lhs(tril, logw[c * CHUNK:(c + 1) * CHUNK]) for c in range(nc)], axis=0)
    cum_last = jnp.concatenate(
        [jnp.broadcast_to(cum[(c + 1) * CHUNK - 1:(c + 1) * CHUNK, :], (CHUNK, R_WIDTH)) for c in range(nc)], axis=0)
    d_inv = jnp.exp(-cum)
    d_end = jnp.exp(cum_last - cum)
    d_last = jnp.exp(cum_last)
    kk = kk0 / jnp.maximum(jnp.sqrt(seg_sum(kk0 * kk0)), 1e-12)
    kka = kk * a
    at = kk * jnp.exp(cum - logw)
    rt = r * jnp.exp(cum)
    bt = kka * d_inv
    kt = k2 * d_inv
    bh = kka * d_end
    kh = k2 * d_end

    items = [(c, p) for c in range(nc) for p in range(R_GROUPS)]
    blk = lambda x, c, p: x[c * CHUNK:(c + 1) * CHUNK, p * GROUP_W:(p + 1) * GROUP_W]
    each = lambda f, *ls: [f(*xs) for xs in zip(*ls)]
    cat0 = lambda *xs: jnp.concatenate(xs, axis=0)
    cat1 = lambda *xs: jnp.concatenate(xs, axis=1)

    def bd(x):
        return cat0(*[jnp.where(lane_head == j, x, 0.0) for j in range(GROUP_HEADS)])

    a_ = [blk(at, c, p) for c, p in items]
    b_ = [blk(bt, c, p) for c, p in items]
    bh_ = [blk(bh, c, p) for c, p in items]
    r_ = [blk(rt, c, p) for c, p in items]
    k_ = [blk(kt, c, p) for c, p in items]
    kh_ = [blk(kh, c, p) for c, p in items]
    v_ = [blk(v, c, p) for c, p in items]
    vbd = [bd(x) for x in v_]
    pq = each(lambda a1, r1, b1, k1: _mm(cat0(a1, r1), cat0(bd(b1), bd(k1)), "nt"), a_, r_, b_, k_)
    t_ab = [jnp.where(low_strict, x[:CHUNK, :GROUP_W], 0.0) for x in pq]
    t_ak = [jnp.where(low_strict, x[:CHUNK, GROUP_W:], 0.0) for x in pq]
    m_rb = [jnp.where(low_incl, x[CHUNK:, :GROUP_W], 0.0) for x in pq]
    m_rk = [jnp.where(low_incl, x[CHUNK:, GROUP_W:], 0.0) for x in pq]
    q_pow = each(lambda t: _mm(t, bd(t)), t_ab)
    x_inv = [eye - t for t in t_ab]
    for _ in range(4):
        z = each(lambda x, q: _mm(cat0(x, q), bd(q)), x_inv, q_pow)
        x_inv = [x + zz[:CHUNK] for x, zz in zip(x_inv, z)]
        q_pow = [zz[CHUNK:] for zz in z]
    x_inv = each(lambda x, q: x + _mm(x, bd(q)), x_inv, q_pow)
    tv = each(_mm, t_ak, vbd)
    xw = each(lambda x, a1, t: _mm(x, cat1(bd(a1), bd(t))), x_inv, a_, tv)
    mw = each(lambda m1, x: _mm(m1, cat1(bd(x[:, :GROUP_W]), bd(x[:, GROUP_W:]))), m_rb, xw)
    q_eff = [r1 - m1[:, :GROUP_W] for r1, m1 in zip(r_, mw)]
    y_loc = each(lambda m1, vb, m2: _mm(m1, vb) - m2[:, GROUP_W:], m_rk, vbd, mw)
    g_bd = [eye_bd * blk(d_last, c, p)[0:1] - jnp.where(same_head, _mm(x[:, :GROUP_W], b1, "tn"), 0.0)
            for (c, p), x, b1 in zip(items, xw, bh_)]
    h_full = each(lambda vv, x, k1, b1: _mm(cat0(vv, x[:, GROUP_W:]), cat0(k1, -b1), "tn"), v_, xw, kh_, bh_)
    h_pair = []
    for x in h_full:
        acc = x[:CHUNK]
        for j in range(1, GROUP_HEADS):
            acc = jnp.where(lane_head == j, x[j * CHUNK:(j + 1) * CHUNK], acc)
        h_pair.append(acc)

    state = [s_scr[p] for p in range(R_GROUPS)]
    ys = []
    for i, (c, p) in enumerate(items):
        ys.append(_mm(q_eff[i], bd(state[p]), "nt") + y_loc[i])
        state[p] = _mm(state[p], g_bd[i]) + h_pair[i]
    for p in range(R_GROUPS):
        s_scr[p] = state[p]

    y_all = cat0(*[cat1(*ys[c * R_GROUPS:(c + 1) * R_GROUPS]) for c in range(nc)])
    yc = y_all - seg_sum(y_all) * (1.0 / R_HEAD_DIM)
    var = seg_sum(yc * yc) * (1.0 / R_HEAD_DIM)
    bonus = seg_sum(r * k2 * rk_ref[...], _mm_mask_rhs)
    y_ref[...] = (yc * lax.rsqrt(var + R_GN_EPS) * gnw_ref[...] + gnb_ref[...] + bonus * v) * g

    @pl.when(t_idx == pl.num_programs(1) - 1)
    def _():
        for p in range(R_GROUPS):
            for j in range(GROUP_HEADS):
                s_out_ref[0, GROUP_HEADS * p + j] = s_scr[p][:, j * R_HEAD_DIM:(j + 1) * R_HEAD_DIM]


def _rwkv_seq(pr, p, nb, nt, tt):
    m = pr.shape[0]
    row = lambda n: _const_spec((1, n))
    return pl.pallas_call(
        functools.partial(_rwkv_seq_body, tt=tt),
        grid=(nb, nt),
        in_specs=[pl.BlockSpec((tt, R_SHIFT_COLS), lambda b, t: (b * nt + t, 0)),
                  row(R_SHIFT_COLS), row(R_WIDTH), _const_spec((W_LORA, R_WIDTH)), row(R_WIDTH),
                  _const_spec((A_LORA, R_WIDTH)), _const_spec((G_LORA, R_WIDTH)), row(R_WIDTH), row(R_WIDTH),
                  row(R_WIDTH), row(R_WIDTH), row(R_WIDTH), _const_spec((GROUP_W, GROUP_W))],
        out_specs=[pl.BlockSpec((tt, R_WIDTH), lambda b, t: (b * nt + t, 0)),
                   pl.BlockSpec((1, R_HEADS, R_HEAD_DIM, R_HEAD_DIM), lambda b, t: (b, 0, 0, 0))],
        out_shape=[jax.ShapeDtypeStruct((m, R_WIDTH), F32),
                   jax.ShapeDtypeStruct((nb, R_HEADS, R_HEAD_DIM, R_HEAD_DIM), F32)],
        scratch_shapes=[pltpu.VMEM((R_GROUPS, R_HEAD_DIM, GROUP_W), F32),
                        pltpu.VMEM((1, R_SHIFT_COLS), F32)],
        compiler_params=pltpu.CompilerParams(dimension_semantics=("arbitrary", "arbitrary"),
                                             vmem_limit_bytes=VMEM_LIMIT),
        name="rwkv_seq",
    )(pr, p["mu"], p["w0"], p["w2"], p["a0"], p["a2"], p["g2"], p["kk"], p["ka"], p["rk"],
      p["gn_w"], p["gn_b"], p["seg"])


def _rwkv_step_body(pr_ref, prev_ref, s_ref, mu_ref, w0_ref, w2_ref, a0_ref, a2_ref, g2_ref, kk_ref, ka_ref,
                    rk_ref, gnw_ref, gnb_ref, y_ref, s_out_ref, *, bb):
    r, k2, v, logw, a, g, kk0 = _rwkv_mix(pr_ref[...], prev_ref[...], mu_ref[...], w0_ref[...], w2_ref[...],
                                          a0_ref[...], a2_ref[...], g2_ref[...], kk_ref[...], ka_ref[...])
    decay = jnp.exp(logw)
    dr = decay * r
    rk = r * k2 * rk_ref[...]
    gnw = gnw_ref[...]
    gnb = gnb_ref[...]
    pad = jnp.zeros((SUBLANES - 2, R_HEAD_DIM), F32)
    heads = [slice(h * R_HEAD_DIM, (h + 1) * R_HEAD_DIM) for h in range(R_HEADS)]
    rows = [slice(b, b + 1) for b in range(bb)]
    kk_n = []
    for hs in heads:
        x = kk0[:, hs]
        kk_n.append(x / jnp.maximum(jnp.sqrt(jnp.sum(x * x, axis=1, keepdims=True)), 1e-12))
    kka = [x * a[:, hs] for x, hs in zip(kk_n, heads)]
    s0 = [[s_ref[b, h] for b in range(bb)] for h in range(R_HEADS)]
    red = [[_mm_split(jnp.concatenate([kk_n[h][bs], dr[bs, heads[h]]], axis=0), s0[h][b], "nt")
            for b, bs in enumerate(rows)] for h in range(R_HEADS)]
    s_kk = [jnp.concatenate([x[0:1] for x in red[h]], axis=0) for h in range(R_HEADS)]
    y_dec = [jnp.concatenate([x[1:2] for x in red[h]], axis=0) for h in range(R_HEADS)]
    upd = [[_mm_split(jnp.concatenate([-s_kk[h][bs], v[bs, heads[h]], pad], axis=0),
                      jnp.concatenate([kka[h][bs], k2[bs, heads[h]], pad], axis=0), "tn")
            for bs in rows] for h in range(R_HEADS)]
    for h in range(R_HEADS):
        for b in range(bb):
            s_out_ref[b, h] = s0[h][b] * decay[rows[b], heads[h]] + upd[h][b]
    outs = []
    for h, hs in enumerate(heads):
        r_h, k_h, v_h = r[:, hs], k2[:, hs], v[:, hs]
        y = (y_dec[h] - s_kk[h] * jnp.sum(kka[h] * r_h, axis=1, keepdims=True)
             + v_h * jnp.sum(k_h * r_h, axis=1, keepdims=True))
        yc = y - jnp.mean(y, axis=1, keepdims=True)
        var = jnp.mean(yc * yc, axis=1, keepdims=True)
        bonus = jnp.sum(rk[:, hs], axis=1, keepdims=True)
        outs.append(yc * lax.rsqrt(var + R_GN_EPS) * gnw[:, hs] + gnb[:, hs] + bonus * v_h)
    y_ref[...] = jnp.concatenate(outs, axis=1) * g


def _rwkv_step(pr, prev, s0, p, bb):
    nb = pr.shape[0]
    row = lambda n: _const_spec((1, n))
    state_spec = pl.BlockSpec((bb, R_HEADS, R_HEAD_DIM, R_HEAD_DIM), lambda i: (i, 0, 0, 0))
    return pl.pallas_call(
        functools.partial(_rwkv_step_body, bb=bb),
        grid=(nb // bb,),
        in_specs=[pl.BlockSpec((bb, R_SHIFT_COLS), lambda i: (i, 0)),
                  pl.BlockSpec((bb, R_SHIFT_COLS), lambda i: (i, 0)),
                  state_spec,
                  row(R_SHIFT_COLS), row(R_WIDTH), _const_spec((W_LORA, R_WIDTH)), row(R_WIDTH),
                  _const_spec((A_LORA, R_WIDTH)), _const_spec((G_LORA, R_WIDTH)), row(R_WIDTH), row(R_WIDTH),
                  row(R_WIDTH), row(R_WIDTH), row(R_WIDTH)],
        out_specs=[pl.BlockSpec((bb, R_WIDTH), lambda i: (i, 0)), state_spec],
        out_shape=[jax.ShapeDtypeStruct((nb, R_WIDTH), F32),
                   jax.ShapeDtypeStruct(s0.shape, F32)],
        compiler_params=pltpu.CompilerParams(dimension_semantics=("arbitrary",), vmem_limit_bytes=VMEM_LIMIT),
        name="rwkv_step",
    )(pr, prev, s0, p["mu"], p["w0"], p["w2"], p["a0"], p["a2"], p["g2"], p["kk"], p["ka"], p["rk"],
      p["gn_w"], p["gn_b"])


GATE_LANE0 = 3 * M_WIDTH


def _head_norm_rows(x, eps):
    xc = x - jnp.mean(x, axis=1, keepdims=True)
    var = jnp.mean(xc * xc, axis=1, keepdims=True)
    return xc * lax.rsqrt(var + eps)


def _mlstm_seq_body(pm_ref, cw_ref, cb_ref, wq_ref, wk_ref, ib_ref, fb_ref, gnw_ref, skip_ref,
                    y_ref, c_out_ref, n_out_ref, m_out_ref, c_scr, n_scr, m_scr, carry_scr, *, tt):
    t_idx = pl.program_id(1)

    @pl.when(t_idx == 0)
    def _():
        c_scr[...] = jnp.zeros_like(c_scr)
        n_scr[...] = jnp.zeros_like(n_scr)
        m_scr[...] = jnp.zeros_like(m_scr)
        carry_scr[...] = jnp.zeros_like(carry_scr)

    xm = pm_ref[:, 0:M_WIDTH]
    carry = carry_scr[...]
    carry_scr[...] = xm[tt - SUBLANES:tt, :]
    row8 = _iota2((SUBLANES, M_WIDTH), 0)
    cw = cw_ref[...]
    xc = cb_ref[...] + xm * cw[CONV_W - 1:CONV_W, :]
    for s in range(1, CONV_W):
        rolled = pltpu.roll(xm, s, axis=0)
        top = jnp.where(row8 < s, pltpu.roll(carry, s, axis=0), rolled[0:SUBLANES])
        shifted = jnp.concatenate([top, rolled[SUBLANES:]], axis=0)
        xc = xc + shifted * cw[CONV_W - 1 - s:CONV_W - s, :]
    xc = xc * _sigmoid(xc)

    gt = pm_ref[:, GATE_LANE0:GATE_LANE0 + LANES]
    li_all = gt + ib_ref[...]
    lf_all = _log_sigmoid(gt + fb_ref[...])

    ri = _iota2((CHUNK, CHUNK), 0)
    ci = _iota2((CHUNK, CHUNK), 1)
    causal = ci <= ri
    tril = causal.astype(F32)
    gnw = gnw_ref[...]
    skip = skip_ref[...]

    bc_all = jnp.concatenate(
        [_mm_mask_lhs(tril, lf_all[c * CHUNK:(c + 1) * CHUNK]) for c in range(tt // CHUNK)], axis=0)
    li_t = li_all.T
    bc_t = bc_all.T

    nc = tt // CHUNK
    items = [(c, h) for c in range(nc) for h in range(M_HEADS)]
    rows = lambda c: slice(c * CHUNK, (c + 1) * CHUNK)
    lanes = lambda h: slice(h * M_HEAD_DIM, (h + 1) * M_HEAD_DIM)
    xc_b = [xc[rows(c), lanes(h)] for c, h in items]
    q = [_mm(x, wq_ref[h]) * (M_HEAD_DIM ** -0.5) for (c, h), x in zip(items, xc_b)]
    k = [_mm(x, wk_ref[h]) for (c, h), x in zip(items, xc_b)]
    vv = [pm_ref[rows(c), M_WIDTH + h * M_HEAD_DIM:M_WIDTH + (h + 1) * M_HEAD_DIM] for c, h in items]
    li_col = [li_all[rows(c), h:h + 1] for c, h in items]
    bc_col = [bc_all[rows(c), M_HEADS + h:M_HEADS + h + 1] for c, h in items]
    dmat = [jnp.where(causal, (bcc - bc_t[M_HEADS + h:M_HEADS + h + 1, rows(c)]) + li_t[h:h + 1, rows(c)], -jnp.inf)
            for (c, h), bcc in zip(items, bc_col)]
    d_max = [jnp.max(d, axis=1, keepdims=True) for d in dmat]
    b_last = [x[CHUNK - 1:CHUNK, :] for x in bc_col]
    lw_end = [(bl - bcc) + lic for bl, bcc, lic in zip(b_last, bc_col, li_col)]
    lw_max = [jnp.max(x, axis=0, keepdims=True) for x in lw_end]

    m_run = [m_scr[h][0:1, 0:1] for h in range(M_HEADS)]
    m_t, g_st, g_end, m_new = [], [], [], []
    for i, (c, h) in enumerate(items):
        g_st.append(bc_col[i] + m_run[h])
        m_t.append(jnp.maximum(g_st[i], d_max[i]))
        g_end.append(b_last[i] + m_run[h])
        m_new.append(jnp.maximum(g_end[i], lw_max[i]))
        m_run[h] = m_new[i]

    w_st = [jnp.exp(g - m) for g, m in zip(g_st, m_t)]
    s = [_mm(qq, kk, "nt") * jnp.exp(d - m) for qq, kk, d, m in zip(q, k, dmat, m_t)]
    sv = [_mm(ss, v1) for ss, v1 in zip(s, vv)]
    s_sum = [jnp.sum(ss, axis=1, keepdims=True) for ss in s]
    we = [jnp.exp(lw - m) for lw, m in zip(lw_end, m_new)]
    ge = [jnp.exp(g - m) for g, m in zip(g_end, m_new)]
    kv = [_mm(kk, w1 * v1, "tn") for kk, w1, v1 in zip(k, we, vv)]
    k_sum = [jnp.sum(w1 * kk, axis=0, keepdims=True) for w1, kk in zip(we, k)]

    c_run = [c_scr[h] for h in range(M_HEADS)]
    n_run = [n_scr[h][0:1, :] for h in range(M_HEADS)]
    num, den = [], []
    for i, (c, h) in enumerate(items):
        num.append(w_st[i] * _mm(q[i], c_run[h]) + sv[i])
        den.append(w_st[i] * jnp.sum(q[i] * n_run[h], axis=1, keepdims=True) + s_sum[i])
        c_run[h] = ge[i] * c_run[h] + kv[i]
        n_run[h] = ge[i] * n_run[h] + k_sum[i]
    for h in range(M_HEADS):
        c_scr[h] = c_run[h]
        n_scr[h] = jnp.broadcast_to(n_run[h], (SUBLANES, M_HEAD_DIM))
        m_scr[h] = jnp.broadcast_to(m_run[h], (SUBLANES, LANES))

    for i, (c, h) in enumerate(items):
        hh = num[i] / jnp.maximum(jnp.abs(den[i]), jnp.exp(-m_t[i]))
        hn = _head_norm_rows(hh, M_GN_EPS) * gnw[:, lanes(h)] + skip[:, lanes(h)] * xc_b[i]
        o = pm_ref[rows(c), 2 * M_WIDTH + h * M_HEAD_DIM:2 * M_WIDTH + (h + 1) * M_HEAD_DIM]
        y_ref[rows(c), lanes(h)] = _sigmoid(o) * hn

    @pl.when(t_idx == pl.num_programs(1) - 1)
    def _():
        c_out_ref[0] = c_scr[...]
        n_out_ref[0] = n_scr[...]
        m_out_ref[0] = m_scr[...]


def _mlstm_seq(pm, p, nb, nt, tt):
    m = pm.shape[0]
    row = lambda n: _const_spec((1, n))
    return pl.pallas_call(
        functools.partial(_mlstm_seq_body, tt=tt),
        grid=(nb, nt),
        in_specs=[pl.BlockSpec((tt, M_COLS_PAD), lambda b, t: (b * nt + t, 0)),
                  _const_spec((CONV_W, M_WIDTH)), row(M_WIDTH),
                  _const_spec((M_HEADS, M_HEAD_DIM, M_HEAD_DIM)), _const_spec((M_HEADS, M_HEAD_DIM, M_HEAD_DIM)),
                  row(LANES), row(LANES), row(M_WIDTH), row(M_WIDTH)],
        out_specs=[pl.BlockSpec((tt, M_WIDTH), lambda b, t: (b * nt + t, 0)),
                   pl.BlockSpec((1, M_HEADS, M_HEAD_DIM, M_HEAD_DIM), lambda b, t: (b, 0, 0, 0)),
                   pl.BlockSpec((1, M_HEADS, SUBLANES, M_HEAD_DIM), lambda b, t: (b, 0, 0, 0)),
                   pl.BlockSpec((1, M_HEADS, SUBLANES, LANES), lambda b, t: (b, 0, 0, 0))],
        out_shape=[jax.ShapeDtypeStruct((m, M_WIDTH), F32),
                   jax.ShapeDtypeStruct((nb, M_HEADS, M_HEAD_DIM, M_HEAD_DIM), F32),
                   jax.ShapeDtypeStruct((nb, M_HEADS, SUBLANES, M_HEAD_DIM), F32),
                   jax.ShapeDtypeStruct((nb, M_HEADS, SUBLANES, LANES), F32)],
        scratch_shapes=[pltpu.VMEM((M_HEADS, M_HEAD_DIM, M_HEAD_DIM), F32),
                        pltpu.VMEM((M_HEADS, SUBLANES, M_HEAD_DIM), F32),
                        pltpu.VMEM((M_HEADS, SUBLANES, LANES), F32),
                        pltpu.VMEM((SUBLANES, M_WIDTH), F32)],
        compiler_params=pltpu.CompilerParams(dimension_semantics=("arbitrary", "arbitrary"),
                                             vmem_limit_bytes=VMEM_LIMIT),
        name="mlstm_seq",
    )(pm, p["conv_w"], p["conv_b"], p["wq"], p["wk"], p["i_b"], p["f_b"], p["gn_w"], p["skip"])


def _mlstm_step_body(pm_ref, conv_ref, c_ref, n_ref, m_ref, cw_ref, cb_ref, wq_ref, wk_ref, ib_ref, fb_ref,
                     gnw_ref, skip_ref, y_ref, c_out_ref, n_out_ref, m_out_ref, conv_out_ref, hv_scr, *, bb):
    xm = pm_ref[:, 0:M_WIDTH]
    cw = cw_ref[...]
    xc = cb_ref[...] + xm * cw[CONV_W - 1:CONV_W, :]
    for j in range(CONV_W - 1):
        xc = xc + conv_ref[:, j * M_WIDTH:(j + 1) * M_WIDTH] * cw[j:j + 1, :]
    xc = xc * _sigmoid(xc)
    conv_out_ref[:, 0:(CONV_W - 2) * M_WIDTH] = conv_ref[:, M_WIDTH:(CONV_W - 1) * M_WIDTH]
    conv_out_ref[:, (CONV_W - 2) * M_WIDTH:] = xm

    gt = pm_ref[:, GATE_LANE0:GATE_LANE0 + LANES]
    li = (gt + ib_ref[...])[:, 0:M_HEADS]
    lf = _log_sigmoid(gt + fb_ref[...])[:, M_HEADS:2 * M_HEADS]
    m_prev = m_ref[...]
    g_st = lf + m_prev
    m_t = jnp.maximum(g_st, (lf - lf) + li)
    w_in = jnp.exp(((lf - lf) + li) - m_t)
    w_st = jnp.exp(g_st - m_t)
    floor = jnp.exp(-m_t)
    m_out_ref[...] = m_t

    ri = _iota2((M_HEAD_DIM, M_HEAD_DIM), 0)
    ci = _iota2((M_HEAD_DIM, M_HEAD_DIM), 1)
    eye = ri == ci
    for h in range(M_HEADS):
        hs = slice(h * M_HEAD_DIM, (h + 1) * M_HEAD_DIM)
        xc_h = xc[:, hs]
        q = _mm(xc_h, wq_ref[h]) * (M_HEAD_DIM ** -0.5)
        k = _mm(xc_h, wk_ref[h])
        vv = pm_ref[:, M_WIDTH + h * M_HEAD_DIM:M_WIDTH + (h + 1) * M_HEAD_DIM]
        n0 = n_ref[:, hs]
        qk = jnp.sum(q * k, axis=1, keepdims=True)
        qn = jnp.sum(q * n0, axis=1, keepdims=True)
        s = qk * w_in[:, h:h + 1]
        den = w_st[:, h:h + 1] * qn + s
        inv = 1.0 / jnp.maximum(jnp.abs(den), floor[:, h:h + 1])
        n_out_ref[:, hs] = w_st[:, h:h + 1] * n0 + w_in[:, h:h + 1] * k
        for b in range(bb):
            bs = slice(b, b + 1)
            c0 = c_ref[b, h]
            q_col = jnp.sum(jnp.where(eye, q[bs], 0.0), axis=1, keepdims=True)
            k_col = jnp.sum(jnp.where(eye, k[bs], 0.0), axis=1, keepdims=True)
            qc = jnp.sum(c0 * q_col, axis=0, keepdims=True)
            ge = w_st[bs, h:h + 1]
            we = w_in[bs, h:h + 1]
            c_out_ref[b, h] = ge * c0 + (we * k_col) * vv[bs]
            hv_scr[bs, hs] = (ge * qc + s[bs] * vv[bs]) * inv[bs]
    hv = hv_scr[...]
    gnw = gnw_ref[...]
    skip = skip_ref[...]
    outs = []
    for h in range(M_HEADS):
        hs = slice(h * M_HEAD_DIM, (h + 1) * M_HEAD_DIM)
        outs.append(_head_norm_rows(hv[:, hs], M_GN_EPS) * gnw[:, hs] + skip[:, hs] * xc[:, hs])
    o = pm_ref[:, 2 * M_WIDTH:3 * M_WIDTH]
    y_ref[...] = _sigmoid(o) * jnp.concatenate(outs, axis=1)


def _mlstm_step(pm, conv0, c0, n0, m0, p, bb):
    nb = pm.shape[0]
    row = lambda n: _const_spec((1, n))
    conv_cols = (CONV_W - 1) * M_WIDTH
    c_spec = pl.BlockSpec((bb, M_HEADS, M_HEAD_DIM, M_HEAD_DIM), lambda i: (i, 0, 0, 0))
    n_spec = pl.BlockSpec((bb, M_WIDTH), lambda i: (i, 0))
    m_spec = pl.BlockSpec((bb, M_HEADS), lambda i: (i, 0))
    conv_spec = pl.BlockSpec((bb, conv_cols), lambda i: (i, 0))
    return pl.pallas_call(
        functools.partial(_mlstm_step_body, bb=bb),
        grid=(nb // bb,),
        in_specs=[pl.BlockSpec((bb, M_COLS_PAD), lambda i: (i, 0)), conv_spec, c_spec, n_spec, m_spec,
                  _const_spec((CONV_W, M_WIDTH)), row(M_WIDTH),
                  _const_spec((M_HEADS, M_HEAD_DIM, M_HEAD_DIM)), _const_spec((M_HEADS, M_HEAD_DIM, M_HEAD_DIM)),
                  row(LANES), row(LANES), row(M_WIDTH), row(M_WIDTH)],
        out_specs=[pl.BlockSpec((bb, M_WIDTH), lambda i: (i, 0)), c_spec, n_spec, m_spec, conv_spec],
        out_shape=[jax.ShapeDtypeStruct((nb, M_WIDTH), F32),
                   jax.ShapeDtypeStruct(c0.shape, F32),
                   jax.ShapeDtypeStruct(n0.shape, F32),
                   jax.ShapeDtypeStruct(m0.shape, F32),
                   jax.ShapeDtypeStruct(conv0.shape, F32)],
        scratch_shapes=[pltpu.VMEM((bb, M_WIDTH), F32)],
        compiler_params=pltpu.CompilerParams(dimension_semantics=("arbitrary",), vmem_limit_bytes=VMEM_LIMIT),
        name="mlstm_step",
    )(pm, conv0, c0, n0, m0, p["conv_w"], p["conv_b"], p["wq"], p["wk"], p["i_b"], p["f_b"],
      p["gn_w"], p["skip"])


def _out_body(x_ref, pg_ref, yr_ref, ym_ref, rup_ref, mup_ref, wout_ref, gffn_ref, w1_ref, w2_ref, gfin_ref,
              y_ref):
    up_r = jnp.dot(yr_ref[...].astype(BF16), rup_ref[...], preferred_element_type=F32)
    up_m = jnp.dot(ym_ref[...].astype(BF16), mup_ref[...], preferred_element_type=F32)
    merged = _sigmoid(pg_ref[:, 0:D_MODEL]) * up_r + _sigmoid(pg_ref[:, D_MODEL:]) * up_m
    x1 = x_ref[...] + jnp.dot(merged.astype(BF16), wout_ref[...], preferred_element_type=F32)
    hn = _rms(x1, gffn_ref[...]).astype(BF16)
    hid = jnp.maximum(jnp.dot(hn, w1_ref[...], preferred_element_type=F32), 0.0)
    hid = (hid * hid).astype(BF16)
    x2 = x1 + jnp.dot(hid, w2_ref[...], preferred_element_type=F32)
    y_ref[...] = _rms(x2, gfin_ref[...])


def _out(x2, pg, yr, ym, p, tm):
    m = x2.shape[0]
    tile = lambda n: pl.BlockSpec((tm, n), lambda i: (i, 0))
    return pl.pallas_call(
        _out_body,
        grid=(m // tm,),
        in_specs=[tile(D_MODEL), tile(GATE_COLS), tile(R_WIDTH), tile(M_WIDTH),
                  _const_spec((R_WIDTH, D_MODEL)), _const_spec((M_WIDTH, D_MODEL)),
                  _const_spec((D_MODEL, D_MODEL)), _const_spec((1, D_MODEL)),
                  _const_spec((D_MODEL, D_FF)), _const_spec((D_FF, D_MODEL)), _const_spec((1, D_MODEL))],
        out_specs=tile(D_MODEL),
        out_shape=jax.ShapeDtypeStruct((m, D_MODEL), F32),
        compiler_params=pltpu.CompilerParams(dimension_semantics=("arbitrary",), vmem_limit_bytes=VMEM_LIMIT),
        name="merge_ffn",
    )(x2, pg, yr, ym, p["r_up"], p["m_up"], p["w_out"], p["g_ffn"], p["w1"], p["w2"], p["g_fin"])


PROJ_TM = 512
RWKV_TT = 512
MLSTM_TT = 128
OUT_TM = 256
STEP_BB = 8


def _head_segments():
    head = jnp.arange(GROUP_W) // R_HEAD_DIM
    return (head[:, None] == head[None, :]).astype(BF16)


def _pad_lanes(v, start):
    out = jnp.zeros((1, LANES), F32)
    return lax.dynamic_update_slice(out, v.reshape(1, -1), (0, start))


def kernel(x_prompt, x_sample, state_rwkv_shift, state_rwkv_wkv, state_mlstm_C, state_mlstm_n, state_mlstm_m, state_mlstm_conv, norm_mix_g, w_in, r_mu, r_w0, r_w2, r_a0, r_a2, r_g2, r_kk, r_ka, r_rk, r_gn_w, r_gn_b, r_up, m_conv_w, m_conv_b, m_wq, m_wk, m_i_b, m_f_b, m_gn_w, m_skip, m_up, gate_b, w_out, norm_ffn_g, ffn_w1, ffn_w2, norm_final_g):
    nbp, seq, _ = x_prompt.shape
    nbs = x_sample.shape[0]
    w = w_in[0]
    w_pad = jnp.concatenate(
        [w[:, :R_SHIFT_COLS + M_COLS], jnp.zeros((D_MODEL, M_COLS_PAD - M_COLS), F32),
         w[:, R_SHIFT_COLS + M_COLS:]], axis=1).astype(BF16)
    g_mix = norm_mix_g[0].reshape(1, D_MODEL)
    gb = gate_b[0].reshape(1, GATE_COLS)
    rp = dict(mu=r_mu[0].reshape(1, -1), w0=r_w0[0].reshape(1, -1), w2=r_w2[0].astype(BF16),
              a0=r_a0[0].reshape(1, -1), a2=r_a2[0].astype(BF16), g2=r_g2[0].astype(BF16),
              kk=r_kk[0].reshape(1, -1), ka=r_ka[0].reshape(1, -1), rk=r_rk[0].reshape(1, -1),
              gn_w=r_gn_w[0].reshape(1, -1), gn_b=r_gn_b[0].reshape(1, -1), seg=_head_segments())
    mp = dict(conv_w=m_conv_w[0], conv_b=m_conv_b[0].reshape(1, -1), wq=m_wq[0].astype(BF16),
              wk=m_wk[0].astype(BF16), i_b=_pad_lanes(m_i_b[0], 0), f_b=_pad_lanes(m_f_b[0], M_HEADS),
              gn_w=m_gn_w[0].reshape(1, -1), skip=m_skip[0].reshape(1, -1))
    op = dict(r_up=r_up[0].astype(BF16), m_up=m_up[0].astype(BF16), w_out=w_out[0].astype(BF16),
              g_ffn=norm_ffn_g[0].reshape(1, -1), w1=ffn_w1[0].astype(BF16), w2=ffn_w2[0].astype(BF16),
              g_fin=norm_final_g.reshape(1, -1))

    xp = x_prompt.reshape(nbp * seq, D_MODEL)
    pr, pm, pg = _proj(xp, g_mix, w_pad, gb, PROJ_TM)
    y_r, wkv_p = _rwkv_seq(pr, rp, nbp, seq // RWKV_TT, RWKV_TT)
    y_m, c_p, n_p, m_p = _mlstm_seq(pm, mp, nbp, seq // MLSTM_TT, MLSTM_TT)
    y_p = _out(xp, pg, y_r, y_m, op, OUT_TM).reshape(nbp, seq, D_MODEL)
    conv_p = pm.reshape(nbp, seq, M_COLS_PAD)[:, seq - (CONV_W - 1):, :M_WIDTH]

    xn_s, xn_last, pr_s, pm_s, pg_s, prev_s = _proj_step(
        x_sample[:, 0], state_rwkv_shift[0], x_prompt[:, seq - 1], g_mix, w_pad, gb)
    yr_s, wkv_s = _rwkv_step(pr_s, prev_s, state_rwkv_wkv[0], rp, STEP_BB)
    conv0 = state_mlstm_conv[0].reshape(nbs, (CONV_W - 1) * M_WIDTH)
    ym_s, c_s, n_s, m_s, conv_s = _mlstm_step(pm_s, conv0, state_mlstm_C[0],
                                              state_mlstm_n[0].reshape(nbs, M_WIDTH),
                                              state_mlstm_m[0], mp, STEP_BB)
    y_s = _out(x_sample[:, 0], pg_s, yr_s, ym_s, op, nbs).reshape(nbs, 1, D_MODEL)

    return (y_p, y_s,
            xn_last[None], wkv_p[None], c_p[None], n_p[:, :, 0, :][None], m_p[:, :, 0, 0][None], conv_p[None],
            xn_s[None], wkv_s[None], c_s[None], n_s.reshape(nbs, M_HEADS, M_HEAD_DIM)[None], m_s[None],
            conv_s.reshape(nbs, CONV_W - 1, M_WIDTH)[None])
```

```python
import functools
import math

import jax
import jax.numpy as jnp
from jax import lax
from jax.experimental import pallas as pl
from jax.experimental.pallas import tpu as pltpu

F32 = jnp.float32
BF16 = jnp.bfloat16

D_MODEL = 1024
R_HEADS = 8
R_HEAD_DIM = 64
R_WIDTH = R_HEADS * R_HEAD_DIM
GROUP_HEADS = 4
R_GROUPS = R_HEADS // GROUP_HEADS
GROUP_W = GROUP_HEADS * R_HEAD_DIM
W_LORA = 64
A_LORA = 64
G_LORA = 128
R_GN_EPS = 64e-5
M_HEADS = 4
M_HEAD_DIM = 128
M_WIDTH = M_HEADS * M_HEAD_DIM
CONV_W = 4
M_GN_EPS = 1e-5
D_FF = 4 * D_MODEL
RMS_EPS = 1e-6
R_SHIFT_COLS = 3 * R_WIDTH + W_LORA + A_LORA + G_LORA
M_COLS = 3 * M_WIDTH + 2 * M_HEADS
GATE_COLS = 2 * D_MODEL

LANES = 128
SUBLANES = 8
CHUNK = 64
IN_COLS = R_SHIFT_COLS + M_COLS + GATE_COLS
W_GATE0 = R_SHIFT_COLS
W_M0 = R_SHIFT_COLS + GATE_COLS
VMEM_LIMIT = 56 * 1024 * 1024


def _mm(a, b, dims="nn"):
    ca = 1 if dims[0] == "n" else 0
    cb = 0 if dims[1] == "n" else 1
    dn = (((ca,), (cb,)), ((), ()))
    return lax.dot_general(a.astype(BF16), b.astype(BF16), dn, preferred_element_type=F32)


def _mm_split(a, b, dims="nn"):
    a_hi = a.astype(BF16)
    b_hi = b.astype(BF16)
    a_lo = a - a_hi.astype(F32)
    b_lo = b - b_hi.astype(F32)
    return _mm(a_hi, b_hi, dims) + (_mm(a_hi, b_lo, dims) + _mm(a_lo, b_hi, dims))


def _mm_mask_lhs(mask, b):
    b_hi = b.astype(BF16)
    return _mm(mask, b_hi) + _mm(mask, b - b_hi.astype(F32))


def _mm_mask_rhs(a, mask):
    a_hi = a.astype(BF16)
    return _mm(a_hi, mask) + _mm(a - a_hi.astype(F32), mask)


def _rms(x, g):
    return x * lax.rsqrt(jnp.mean(x * x, axis=-1, keepdims=True) + RMS_EPS) * g


def _sigmoid(x):
    return 1.0 / (1.0 + jnp.exp(-x))


def _log_sigmoid(x):
    return jnp.minimum(x, 0.0) - jnp.log(1.0 + jnp.exp(-jnp.abs(x)))


def _iota2(shape, axis):
    return lax.broadcasted_iota(jnp.int32, shape, axis)


def _const_spec(shape):
    nd = len(shape)
    return pl.BlockSpec(shape, lambda *_: (0,) * nd, pipeline_mode=pl.Buffered(1))


def _proj_cols(xb, wt_ref, lo, hi):
    return lax.dot_general(xb, wt_ref[lo:hi, :], (((1,), (1,)), ((), ())), preferred_element_type=F32)


def _proj_body(x_ref, g_ref, w_ref, gb_ref, pr_ref, pm_ref, pg_ref):
    xb = _rms(x_ref[...], g_ref[...]).astype(BF16)
    pr_ref[...] = _proj_cols(xb, w_ref, 0, R_SHIFT_COLS)
    pm_ref[...] = _proj_cols(xb, w_ref, W_M0, IN_COLS)
    pg_ref[...] = _proj_cols(xb, w_ref, W_GATE0, W_M0) + gb_ref[...]


def _proj(x2, g, w_cat, gate_b, tm):
    m = x2.shape[0]
    return pl.pallas_call(
        _proj_body,
        grid=(m // tm,),
        in_specs=[pl.BlockSpec((tm, D_MODEL), lambda i: (i, 0)),
                  _const_spec((1, D_MODEL)),
                  _const_spec((IN_COLS, D_MODEL)),
                  _const_spec((1, GATE_COLS))],
        out_specs=[pl.BlockSpec((tm, R_SHIFT_COLS), lambda i: (i, 0)),
                   pl.BlockSpec((tm, M_COLS), lambda i: (i, 0)),
                   pl.BlockSpec((tm, GATE_COLS), lambda i: (i, 0))],
        out_shape=[jax.ShapeDtypeStruct((m, R_SHIFT_COLS), F32),
                   jax.ShapeDtypeStruct((m, M_COLS), F32),
                   jax.ShapeDtypeStruct((m, GATE_COLS), F32)],
        compiler_params=pltpu.CompilerParams(dimension_semantics=("arbitrary",), vmem_limit_bytes=VMEM_LIMIT),
        name="proj",
    )(x2, g, w_cat, gate_b)


def _proj_step_body(xs_ref, sh_ref, xl_ref, g_ref, w_ref, gb_ref,
                    xns_ref, xnl_ref, pr_ref, pm_ref, pg_ref, prev_ref):
    g = g_ref[...]
    xn = _rms(xs_ref[...], g)
    xns_ref[...] = xn
    xnl_ref[...] = _rms(xl_ref[...], g)
    xb = xn.astype(BF16)
    pr_ref[...] = _proj_cols(xb, w_ref, 0, R_SHIFT_COLS)
    pm_ref[...] = _proj_cols(xb, w_ref, W_M0, IN_COLS)
    pg_ref[...] = _proj_cols(xb, w_ref, W_GATE0, W_M0) + gb_ref[...]
    prev_ref[...] = _proj_cols(sh_ref[...].astype(BF16), w_ref, 0, R_SHIFT_COLS)


def _proj_step(xs, shift0, xlast, g, w_cat, gate_b):
    nb = xs.shape[0]
    nl = xlast.shape[0]
    return pl.pallas_call(
        _proj_step_body,
        out_shape=[jax.ShapeDtypeStruct((nb, D_MODEL), F32),
                   jax.ShapeDtypeStruct((nl, D_MODEL), F32),
                   jax.ShapeDtypeStruct((nb, R_SHIFT_COLS), F32),
                   jax.ShapeDtypeStruct((nb, M_COLS), F32),
                   jax.ShapeDtypeStruct((nb, GATE_COLS), F32),
                   jax.ShapeDtypeStruct((nb, R_SHIFT_COLS), F32)],
        compiler_params=pltpu.CompilerParams(vmem_limit_bytes=VMEM_LIMIT),
        name="proj_step",
    )(xs, shift0, xlast, g, w_cat, gate_b)


LOG_DECAY_SCALE = -math.exp(-0.5)


def _rwkv_mix(pr, prev, mu, w0, w2, a0, a2, g2, r_kk, r_ka):
    mixed = pr + (prev - pr) * mu
    r = mixed[:, 0:R_WIDTH]
    k = mixed[:, R_WIDTH:2 * R_WIDTH]
    v = mixed[:, 2 * R_WIDTH:3 * R_WIDTH]
    xw = mixed[:, 3 * R_WIDTH:3 * R_WIDTH + W_LORA]
    xa = mixed[:, 3 * R_WIDTH + W_LORA:3 * R_WIDTH + W_LORA + A_LORA]
    xg = mixed[:, 3 * R_WIDTH + W_LORA + A_LORA:]
    w = w0 + _mm(jnp.tanh(xw), w2)
    logw = LOG_DECAY_SCALE * _sigmoid(w)
    a = _sigmoid(a0 + _mm(xa, a2))
    g = _mm(_sigmoid(xg), g2)
    kk0 = k * r_kk
    k2 = k * (1.0 + (a - 1.0) * r_ka)
    return r, k2, v, logw, a, g, kk0


def _rwkv_seq_body(pr_ref, mu_ref, w0_ref, w2_ref, a0_ref, a2_ref, g2_ref, kk_ref, ka_ref, rk_ref,
                   gnw_ref, gnb_ref, seg_ref, y_ref, s_out_ref, s_scr, carry_scr, *, tt):
    t_idx = pl.program_id(1)

    @pl.when(t_idx == 0)
    def _():
        s_scr[...] = jnp.zeros_like(s_scr)
        carry_scr[...] = jnp.zeros_like(carry_scr)

    pr = pr_ref[...]
    row = _iota2(pr.shape, 0)
    prev = jnp.where(row == 0, carry_scr[...], pltpu.roll(pr, 1, axis=0))
    carry_scr[...] = pr[tt - 1:tt, :]

    r, k2, v, logw, a, g, kk0 = _rwkv_mix(pr, prev, mu_ref[...], w0_ref[...], w2_ref[...], a0_ref[...],
                                          a2_ref[...], g2_ref[...], kk_ref[...], ka_ref[...])
    ri = _iota2((CHUNK, GROUP_W), 0)
    ci = _iota2((CHUNK, GROUP_W), 1)
    src = ci % R_HEAD_DIM
    lane_head = ci // R_HEAD_DIM
    low_incl = src <= ri
    low_strict = src < ri
    eye = (src == ri).astype(F32)
    rr = _iota2((GROUP_W, GROUP_W), 0)
    cc = _iota2((GROUP_W, GROUP_W), 1)
    same_head = (rr // R_HEAD_DIM) == (cc // R_HEAD_DIM)
    eye_bd = (rr == cc).astype(F32)
    seg = seg_ref[...]

    def seg_sum(x, mm=_mm):
        return jnp.concatenate([mm(x[:, q * GROUP_W:(q + 1) * GROUP_W], seg) for q in range(R_GROUPS)], axis=1)
    tril = (_iota2((CHUNK, CHUNK), 1) <= _iota2((CHUNK, CHUNK), 0)).astype(F32)

    nc = tt // CHUNK
    cum = jnp.concatenate([_mm_mask_lhs(tril, logw[c * CHUNK:(c + 1) * CHUNK]) for c in range(nc)], axis=0)
    cum_last = jnp.concatenate(
        [jnp.broadcast_to(cum[(c + 1) * CHUNK - 1:(c + 1) * CHUNK, :], (CHUNK, R_WIDTH)) for c in range(nc)], axis=0)
    d_inv = jnp.exp(-cum)
    d_end = jnp.exp(cum_last - cum)
    d_last = jnp.exp(cum_last)
    kk = kk0 / jnp.maximum(jnp.sqrt(seg_sum(kk0 * kk0)), 1e-12)
    kka = kk * a
    at = kk * jnp.exp(cum - logw)
    rt = r * jnp.exp(cum)
    bt = kka * d_inv
    kt = k2 * d_inv
    bh = kka * d_end
    kh = k2 * d_end

    items = [(c, p) for c in range(nc) for p in range(R_GROUPS)]
    blk = lambda x, c, p: x[c * CHUNK:(c + 1) * CHUNK, p * GROUP_W:(p + 1) * GROUP_W]
    each = lambda f, *ls: [f(*xs) for xs in zip(*ls)]
    cat0 = lambda *xs: jnp.concatenate(xs, axis=0)
    cat1 = lambda *xs: jnp.concatenate(xs, axis=1)

    def bd(x):
        return cat0(*[jnp.where(lane_head == j, x, 0.0) for j in range(GROUP_HEADS)])

    a_ = [blk(at, c, p) for c, p in items]
    b_ = [blk(bt, c, p) for c, p in items]
    bh_ = [blk(bh, c, p) for c, p in items]
    r_ = [blk(rt, c, p) for c, p in items]
    k_ = [blk(kt, c, p) for c, p in items]
    kh_ = [blk(kh, c, p) for c, p in items]
    v_ = [blk(v, c, p) for c, p in items]
    vbd = [bd(x) for x in v_]
    pq = each(lambda a1, r1, b1, k1: _mm(cat0(a1, r1), cat0(bd(b1), bd(k1)), "nt"), a_, r_, b_, k_)
    t_ab = [jnp.where(low_strict, x[:CHUNK, :GROUP_W], 0.0) for x in pq]
    t_ak = [jnp.where(low_strict, x[:CHUNK, GROUP_W:], 0.0) for x in pq]
    m_rb = [jnp.where(low_incl, x[CHUNK:, :GROUP_W], 0.0) for x in pq]
    m_rk = [jnp.where(low_incl, x[CHUNK:, GROUP_W:], 0.0) for x in pq]
    q_pow = each(lambda t: _mm(t, bd(t)), t_ab)
    x_inv = [eye - t for t in t_ab]
    for _ in range(4):
        z = each(lambda x, q: _mm(cat0(x, q), bd(q)), x_inv, q_pow)
        x_inv = [x + zz[:CHUNK] for x, zz in zip(x_inv, z)]
        q_pow = [zz[CHUNK:] for zz in z]
    x_inv = each(lambda x, q: x + _mm(x, bd(q)), x_inv, q_pow)
    tv = each(_mm, t_ak, vbd)
    xw = each(lambda x, a1, t: _mm(x, cat1(bd(a1), bd(t))), x_inv, a_, tv)
    mw = each(lambda m1, x: _mm(m1, cat1(bd(x[:, :GROUP_W]), bd(x[:, GROUP_W:]))), m_rb, xw)
    q_eff = [r1 - m1[:, :GROUP_W] for r1, m1 in zip(r_, mw)]
    y_loc = each(lambda m1, vb, m2: _mm(m1, vb) - m2[:, GROUP_W:], m_rk, vbd, mw)
    g_bd = [eye_bd * blk(d_last, c, p)[0:1] - jnp.where(same_head, _mm(x[:, :GROUP_W], b1, "tn"), 0.0)
            for (c, p), x, b1 in zip(items, xw, bh_)]
    h_full = each(lambda vv, x, k1, b1: _mm(cat0(vv, x[:, GROUP_W:]), cat0(k1, -b1), "tn"), v_, xw, kh_, bh_)
    h_pair = []
    for x in h_full:
        acc = x[:CHUNK]
        for j in range(1, GROUP_HEADS):
            acc = jnp.where(lane_head == j, x[j * CHUNK:(j + 1) * CHUNK], acc)
        h_pair.append(acc)

    state = [s_scr[p] for p in range(R_GROUPS)]
    ys = []
    for i, (c, p) in enumerate(items):
        ys.append(_mm(q_eff[i], bd(state[p]), "nt") + y_loc[i])
        state[p] = _mm(state[p], g_bd[i]) + h_pair[i]
    for p in range(R_GROUPS):
        s_scr[p] = state[p]

    y_all = cat0(*[cat1(*ys[c * R_GROUPS:(c + 1) * R_GROUPS]) for c in range(nc)])
    yc = y_all - seg_sum(y_all) * (1.0 / R_HEAD_DIM)
    var = seg_sum(yc * yc) * (1.0 / R_HEAD_DIM)
    bonus = seg_sum(r * k2 * rk_ref[...], _mm_mask_rhs)
    y_ref[...] = (yc * lax.rsqrt(var + R_GN_EPS) * gnw_ref[...] + gnb_ref[...] + bonus * v) * g

    @pl.when(t_idx == pl.num_programs(1) - 1)
    def _():
        for p in range(R_GROUPS):
            for j in range(GROUP_HEADS):
                s_out_ref[0, GROUP_HEADS * p + j] = s_scr[p][:, j * R_HEAD_DIM:(j + 1) * R_HEAD_DIM]


def _rwkv_seq(pr, p, nb, nt, tt):
    m = pr.shape[0]
    row = lambda n: _const_spec((1, n))
    return pl.pallas_call(
        functools.partial(_rwkv_seq_body, tt=tt),
        grid=(nb, nt),
        in_specs=[pl.BlockSpec((tt, R_SHIFT_COLS), lambda b, t: (b * nt + t, 0)),
                  row(R_SHIFT_COLS), row(R_WIDTH), _const_spec((W_LORA, R_WIDTH)), row(R_WIDTH),
                  _const_spec((A_LORA, R_WIDTH)), _const_spec((G_LORA, R_WIDTH)), row(R_WIDTH), row(R_WIDTH),
                  row(R_WIDTH), row(R_WIDTH), row(R_WIDTH), _const_spec((GROUP_W, GROUP_W))],
        out_specs=[pl.BlockSpec((tt, R_WIDTH), lambda b, t: (b * nt + t, 0)),
                   pl.BlockSpec((1, R_HEADS, R_HEAD_DIM, R_HEAD_DIM), lambda b, t: (b, 0, 0, 0))],
        out_shape=[jax.ShapeDtypeStruct((m, R_WIDTH), F32),
                   jax.ShapeDtypeStruct((nb, R_HEADS, R_HEAD_DIM, R_HEAD_DIM), F32)],
        scratch_shapes=[pltpu.VMEM((R_GROUPS, R_HEAD_DIM, GROUP_W), F32),
                        pltpu.VMEM((1, R_SHIFT_COLS), F32)],
        compiler_params=pltpu.CompilerParams(dimension_semantics=("arbitrary", "arbitrary"),
                                             vmem_limit_bytes=VMEM_LIMIT),
        name="rwkv_seq",
    )(pr, p["mu"], p["w0"], p["w2"], p["a0"], p["a2"], p["g2"], p["kk"], p["ka"], p["rk"],
      p["gn_w"], p["gn_b"], p["seg"])


def _rwkv_step_body(pr_ref, prev_ref, s_ref, mu_ref, w0_ref, w2_ref, a0_ref, a2_ref, g2_ref, kk_ref, ka_ref,
                    rk_ref, gnw_ref, gnb_ref, seg_ref, y_ref, s_out_ref,
                    kk_scr, kka_scr, dec_scr, k_scr, v_scr, r_scr, yt_scr, g_scr, vrow_scr, bonus_scr):
    h = pl.program_id(0)
    seg = seg_ref[...]

    def seg_sum(x, mm=_mm):
        return jnp.concatenate([mm(x[:, q * GROUP_W:(q + 1) * GROUP_W], seg) for q in range(R_GROUPS)], axis=1)

    @pl.when(h == 0)
    def _():
        r, k2, v, logw, a, g, kk0 = _rwkv_mix(pr_ref[...], prev_ref[...], mu_ref[...], w0_ref[...], w2_ref[...],
                                              a0_ref[...], a2_ref[...], g2_ref[...], kk_ref[...], ka_ref[...])
        kk = kk0 / jnp.maximum(jnp.sqrt(seg_sum(kk0 * kk0)), 1e-12)
        kk_scr[...] = kk.T
        kka_scr[...] = (kk * a).T
        dec_scr[...] = jnp.exp(logw).T
        k_scr[...] = k2.T
        v_scr[...] = v.T
        r_scr[...] = r.T
        g_scr[...] = g
        vrow_scr[...] = v
        bonus_scr[...] = seg_sum(r * k2 * rk_ref[...], _mm_mask_rhs)

    base = pl.multiple_of(h * R_HEAD_DIM, R_HEAD_DIM)
    hs = pl.ds(base, R_HEAD_DIM)
    kk_t, kka_t, dec_t, k_t, r_t = kk_scr[hs, :], kka_scr[hs, :], dec_scr[hs, :], k_scr[hs, :], r_scr[hs, :]
    for i in range(R_HEAD_DIM):
        s0 = s_ref[0, i]
        s_kk = jnp.sum(s0 * kk_t, axis=0, keepdims=True)
        s1 = s0 * dec_t - s_kk * kka_t + v_scr[pl.ds(base + i, 1), :] * k_t
        s_out_ref[0, i] = s1
        yt_scr[pl.ds(base + i, 1), :] = jnp.sum(s1 * r_t, axis=0, keepdims=True)

    @pl.when(h == pl.num_programs(0) - 1)
    def _():
        y = yt_scr[...].T
        yc = y - seg_sum(y) * (1.0 / R_HEAD_DIM)
        var = seg_sum(yc * yc) * (1.0 / R_HEAD_DIM)
        y_ref[...] = (yc * lax.rsqrt(var + R_GN_EPS) * gnw_ref[...] + gnb_ref[...]
                      + bonus_scr[...] * vrow_scr[...]) * g_scr[...]


def _rwkv_step(pr, prev, s0_t, p):
    nb = pr.shape[0]
    row = lambda n: _const_spec((1, n))
    state_spec = pl.BlockSpec((1, R_HEAD_DIM, R_HEAD_DIM, nb), lambda h: (h, 0, 0, 0))
    tr = pltpu.VMEM((R_WIDTH, nb), F32)
    rw = pltpu.VMEM((nb, R_WIDTH), F32)
    return pl.pallas_call(
        _rwkv_step_body,
        grid=(R_HEADS,),
        in_specs=[_const_spec((nb, R_SHIFT_COLS)), _const_spec((nb, R_SHIFT_COLS)), state_spec,
                  row(R_SHIFT_COLS), row(R_WIDTH), _const_spec((W_LORA, R_WIDTH)), row(R_WIDTH),
                  _const_spec((A_LORA, R_WIDTH)), _const_spec((G_LORA, R_WIDTH)), row(R_WIDTH), row(R_WIDTH),
                  row(R_WIDTH), row(R_WIDTH), row(R_WIDTH), _const_spec((GROUP_W, GROUP_W))],
        out_specs=[pl.BlockSpec((nb, R_WIDTH), lambda h: (0, 0)), state_spec],
        out_shape=[jax.ShapeDtypeStruct((nb, R_WIDTH), F32),
                   jax.ShapeDtypeStruct(s0_t.shape, F32)],
        scratch_shapes=[tr, tr, tr, tr, tr, tr, tr, rw, rw, rw],
        compiler_params=pltpu.CompilerParams(dimension_semantics=("arbitrary",), vmem_limit_bytes=VMEM_LIMIT),
        name="rwkv_step",
    )(pr, prev, s0_t, p["mu"], p["w0"], p["w2"], p["a0"], p["a2"], p["g2"], p["kk"], p["ka"], p["rk"],
      p["gn_w"], p["gn_b"], p["seg"])


GATE_LANE0 = 3 * M_WIDTH


def _gate_lanes(pm_ref):
    gates = pm_ref[:, GATE_LANE0:GATE_LANE0 + 2 * M_HEADS]
    return jnp.concatenate([gates, jnp.zeros((gates.shape[0], LANES - 2 * M_HEADS), F32)], axis=1)


def _head_norm_rows(x, eps):
    xc = x - jnp.mean(x, axis=1, keepdims=True)
    var = jnp.mean(xc * xc, axis=1, keepdims=True)
    return xc * lax.rsqrt(var + eps)


def _cummax_rows(x):
    n = x.shape[0]
    row = _iota2(x.shape, 0)
    d = 1
    while d < n:
        if d < SUBLANES:
            shifted = jnp.where(row < d, -jnp.inf, pltpu.roll(x, d, axis=0))
        else:
            shifted = jnp.concatenate([jnp.full((d, x.shape[1]), -jnp.inf, x.dtype), x[:n - d]], axis=0)
        x = jnp.maximum(x, shifted)
        d *= 2
    return x


def _mlstm_seq_body(pm_ref, cw_ref, cb_ref, wq_ref, wk_ref, ib_ref, fb_ref, gnw_ref, skip_ref,
                    y_ref, c_out_ref, n_out_ref, m_out_ref, c_scr, n_scr, m_scr, carry_scr, *, tt):
    t_idx = pl.program_id(1)

    @pl.when(t_idx == 0)
    def _():
        c_scr[...] = jnp.zeros_like(c_scr)
        n_scr[...] = jnp.zeros_like(n_scr)
        m_scr[...] = jnp.zeros_like(m_scr)
        carry_scr[...] = jnp.zeros_like(carry_scr)

    xm = pm_ref[:, 0:M_WIDTH]
    carry = carry_scr[...]
    carry_scr[...] = xm[tt - SUBLANES:tt, :]
    row8 = _iota2((SUBLANES, M_WIDTH), 0)
    cw = cw_ref[...]
    xc = cb_ref[...] + xm * cw[CONV_W - 1:CONV_W, :]
    for s in range(1, CONV_W):
        rolled = pltpu.roll(xm, s, axis=0)
        top = jnp.where(row8 < s, pltpu.roll(carry, s, axis=0), rolled[0:SUBLANES])
        shifted = jnp.concatenate([top, rolled[SUBLANES:]], axis=0)
        xc = xc + shifted * cw[CONV_W - 1 - s:CONV_W - s, :]
    xc = xc * _sigmoid(xc)

    gt = _gate_lanes(pm_ref)
    li_all = gt + ib_ref[...]
    lf_all = _log_sigmoid(gt + fb_ref[...])

    ri = _iota2((CHUNK, CHUNK), 0)
    ci = _iota2((CHUNK, CHUNK), 1)
    causal = ci <= ri
    tril = causal.astype(F32)
    gnw = gnw_ref[...]
    skip = skip_ref[...]

    ones_cl = jnp.ones((CHUNK, LANES), BF16)
    ones_ll = jnp.ones((LANES, LANES), BF16)
    nc = tt // CHUNK
    rows = lambda c: slice(c * CHUNK, (c + 1) * CHUNK)
    lanes = lambda h: slice(h * M_HEAD_DIM, (h + 1) * M_HEAD_DIM)

    bc_all = jnp.concatenate([_mm_mask_lhs(tril, lf_all[rows(c)]) for c in range(nc)], axis=0)
    bc_al = pltpu.roll(bc_all, LANES - M_HEADS, axis=1)
    u_all = li_all - bc_al
    cm_all = jnp.concatenate([_cummax_rows(u_all[rows(c)]) for c in range(nc)], axis=0)
    u_t = u_all.T

    items = [(c, h) for c in range(nc) for h in range(M_HEADS)]
    rep = lambda x, c, h: jnp.broadcast_to(x[rows(c), h:h + 1], (CHUNK, LANES))
    xc_b = [xc[rows(c), lanes(h)] for c, h in items]
    q = [_mm(x, wq_ref[h]) * (M_HEAD_DIM ** -0.5) for (c, h), x in zip(items, xc_b)]
    k = [_mm(x, wk_ref[h]) for (c, h), x in zip(items, xc_b)]
    vv = [pm_ref[rows(c), M_WIDTH + h * M_HEAD_DIM:M_WIDTH + (h + 1) * M_HEAD_DIM] for c, h in items]
    cm_r = [rep(cm_all, c, h) for c, h in items]
    bc_r = [rep(bc_al, c, h) for c, h in items]
    u_r = [rep(u_all, c, h) for c, h in items]

    m_run = [m_scr[h][0:1, 0:1] for h in range(M_HEADS)]
    m_prev, m_cap, m_end = [], [], []
    for i, (c, h) in enumerate(items):
        m_prev.append(m_run[h])
        m_cap.append(jnp.maximum(m_run[h], cm_r[i]))
        m_end.append(m_cap[i][CHUNK - 1:CHUNK, 0:1])
        m_run[h] = bc_r[i][CHUNK - 1:CHUNK, 0:1] + m_end[i]

    w_st = [jnp.exp(mp - mc) for mp, mc in zip(m_prev, m_cap)]
    floor = [jnp.exp(-(b + mc)) for b, mc in zip(bc_r, m_cap)]
    w_in = [jnp.exp(jnp.where(causal, u_t[h:h + 1, rows(c)] - mc[:, :CHUNK], -jnp.inf))
            for (c, h), mc in zip(items, m_cap)]
    s = [_mm(qq, kk, "nt") * w for qq, kk, w in zip(q, k, w_in)]
    sv = [_mm(ss, v1) for ss, v1 in zip(s, vv)]
    s_sum = [_mm(ss, ones_cl) for ss in s]
    we = [jnp.exp(u - me) for u, me in zip(u_r, m_end)]
    ge = [jnp.exp(mp - me) for mp, me in zip(m_prev, m_end)]
    kv = [_mm(kk, w1 * v1, "tn") for kk, w1, v1 in zip(k, we, vv)]
    k_sum = [jnp.sum(w1 * kk, axis=0, keepdims=True) for w1, kk in zip(we, k)]

    c_run = [c_scr[h] for h in range(M_HEADS)]
    n_run = [n_scr[h][0:1, :] for h in range(M_HEADS)]
    num, den = [], []
    for i, (c, h) in enumerate(items):
        num.append(w_st[i] * _mm(q[i], c_run[h]) + sv[i])
        den.append(w_st[i] * _mm(q[i] * n_run[h], ones_ll) + s_sum[i])
        c_run[h] = ge[i] * c_run[h] + kv[i]
        n_run[h] = ge[i] * n_run[h] + k_sum[i]
    for h in range(M_HEADS):
        c_scr[h] = c_run[h]
        n_scr[h] = jnp.broadcast_to(n_run[h], (SUBLANES, M_HEAD_DIM))
        m_scr[h] = jnp.broadcast_to(m_run[h], (SUBLANES, LANES))

    for i, (c, h) in enumerate(items):
        hh = num[i] / jnp.maximum(jnp.abs(den[i]), floor[i])
        hc = hh - _mm(hh, ones_ll) * (1.0 / M_HEAD_DIM)
        var = _mm(hc * hc, ones_ll) * (1.0 / M_HEAD_DIM)
        hn = hc * lax.rsqrt(var + M_GN_EPS) * gnw[:, lanes(h)] + skip[:, lanes(h)] * xc_b[i]
        o = pm_ref[rows(c), 2 * M_WIDTH + h * M_HEAD_DIM:2 * M_WIDTH + (h + 1) * M_HEAD_DIM]
        y_ref[rows(c), lanes(h)] = _sigmoid(o) * hn

    @pl.when(t_idx == pl.num_programs(1) - 1)
    def _():
        c_out_ref[0] = c_scr[...]
        n_out_ref[0] = n_scr[...]
        m_out_ref[0] = m_scr[...]


def _mlstm_seq(pm, p, nb, nt, tt):
    m = pm.shape[0]
    row = lambda n: _const_spec((1, n))
    return pl.pallas_call(
        functools.partial(_mlstm_seq_body, tt=tt),
        grid=(nb, nt),
        in_specs=[pl.BlockSpec((tt, M_COLS), lambda b, t: (b * nt + t, 0)),
                  _const_spec((CONV_W, M_WIDTH)), row(M_WIDTH),
                  _const_spec((M_HEADS, M_HEAD_DIM, M_HEAD_DIM)), _const_spec((M_HEADS, M_HEAD_DIM, M_HEAD_DIM)),
                  row(LANES), row(LANES), row(M_WIDTH), row(M_WIDTH)],
        out_specs=[pl.BlockSpec((tt, M_WIDTH), lambda b, t: (b * nt + t, 0)),
                   pl.BlockSpec((1, M_HEADS, M_HEAD_DIM, M_HEAD_DIM), lambda b, t: (b, 0, 0, 0)),
                   pl.BlockSpec((1, M_HEADS, SUBLANES, M_HEAD_DIM), lambda b, t: (b, 0, 0, 0)),
                   pl.BlockSpec((1, M_HEADS, SUBLANES, LANES), lambda b, t: (b, 0, 0, 0))],
        out_shape=[jax.ShapeDtypeStruct((m, M_WIDTH), F32),
                   jax.ShapeDtypeStruct((nb, M_HEADS, M_HEAD_DIM, M_HEAD_DIM), F32),
                   jax.ShapeDtypeStruct((nb, M_HEADS, SUBLANES, M_HEAD_DIM), F32),
                   jax.ShapeDtypeStruct((nb, M_HEADS, SUBLANES, LANES), F32)],
        scratch_shapes=[pltpu.VMEM((M_HEADS, M_HEAD_DIM, M_HEAD_DIM), F32),
                        pltpu.VMEM((M_HEADS, SUBLANES, M_HEAD_DIM), F32),
                        pltpu.VMEM((M_HEADS, SUBLANES, LANES), F32),
                        pltpu.VMEM((SUBLANES, M_WIDTH), F32)],
        compiler_params=pltpu.CompilerParams(dimension_semantics=("arbitrary", "arbitrary"),
                                             vmem_limit_bytes=VMEM_LIMIT),
        name="mlstm_seq",
    )(pm, p["conv_w"], p["conv_b"], p["wq"], p["wk"], p["i_b"], p["f_b"], p["gn_w"], p["skip"])


def _mlstm_step_body(pm_ref, conv_ref, c_ref, n_ref, m_ref, cw_ref, cb_ref, wq_ref, wk_ref, ib_ref, fb_ref,
                     gnw_ref, skip_ref, y_ref, c_out_ref, n_out_ref, m_out_ref, conv_out_ref, hv_scr, *, bb):
    xm = pm_ref[:, 0:M_WIDTH]
    cw = cw_ref[...]
    xc = cb_ref[...] + xm * cw[CONV_W - 1:CONV_W, :]
    for j in range(CONV_W - 1):
        xc = xc + conv_ref[j] * cw[j:j + 1, :]
    xc = xc * _sigmoid(xc)
    for j in range(CONV_W - 2):
        conv_out_ref[j] = conv_ref[j + 1]
    conv_out_ref[CONV_W - 2] = xm

    gt = _gate_lanes(pm_ref)
    li = (gt + ib_ref[...])[:, 0:M_HEADS]
    lf = _log_sigmoid(gt + fb_ref[...])[:, M_HEADS:2 * M_HEADS]
    m_prev = m_ref[...]
    g_st = lf + m_prev
    m_t = jnp.maximum(g_st, (lf - lf) + li)
    w_in = jnp.exp(((lf - lf) + li) - m_t)
    w_st = jnp.exp(g_st - m_t)
    floor = jnp.exp(-m_t)
    m_out_ref[...] = m_t

    ri = _iota2((M_HEAD_DIM, M_HEAD_DIM), 0)
    ci = _iota2((M_HEAD_DIM, M_HEAD_DIM), 1)
    eye = ri == ci
    for h in range(M_HEADS):
        hs = slice(h * M_HEAD_DIM, (h + 1) * M_HEAD_DIM)
        xc_h = xc[:, hs]
        q = _mm(xc_h, wq_ref[h]) * (M_HEAD_DIM ** -0.5)
        k = _mm(xc_h, wk_ref[h])
        vv = pm_ref[:, M_WIDTH + h * M_HEAD_DIM:M_WIDTH + (h + 1) * M_HEAD_DIM]
        n0 = n_ref[:, hs]
        qk = jnp.sum(q * k, axis=1, keepdims=True)
        qn = jnp.sum(q * n0, axis=1, keepdims=True)
        s = qk * w_in[:, h:h + 1]
        den = w_st[:, h:h + 1] * qn + s
        inv = 1.0 / jnp.maximum(jnp.abs(den), floor[:, h:h + 1])
        n_out_ref[:, hs] = w_st[:, h:h + 1] * n0 + w_in[:, h:h + 1] * k
        for b in range(bb):
            bs = slice(b, b + 1)
            c0 = c_ref[b, h]
            q_col = jnp.sum(jnp.where(eye, q[bs], 0.0), axis=1, keepdims=True)
            k_col = jnp.sum(jnp.where(eye, k[bs], 0.0), axis=1, keepdims=True)
            qc = jnp.sum(c0 * q_col, axis=0, keepdims=True)
            ge = w_st[bs, h:h + 1]
            we = w_in[bs, h:h + 1]
            c_out_ref[b, h] = ge * c0 + (we * k_col) * vv[bs]
            hv_scr[bs, hs] = (ge * qc + s[bs] * vv[bs]) * inv[bs]
    hv = hv_scr[...]
    gnw = gnw_ref[...]
    skip = skip_ref[...]
    outs = []
    for h in range(M_HEADS):
        hs = slice(h * M_HEAD_DIM, (h + 1) * M_HEAD_DIM)
        outs.append(_head_norm_rows(hv[:, hs], M_GN_EPS) * gnw[:, hs] + skip[:, hs] * xc[:, hs])
    o = pm_ref[:, 2 * M_WIDTH:3 * M_WIDTH]
    y_ref[...] = _sigmoid(o) * jnp.concatenate(outs, axis=1)


def _mlstm_step(pm, conv0, c0, n0, m0, p, bb):
    nb = pm.shape[0]
    row = lambda n: _const_spec((1, n))
    c_spec = pl.BlockSpec((bb, M_HEADS, M_HEAD_DIM, M_HEAD_DIM), lambda i: (i, 0, 0, 0))
    n_spec = pl.BlockSpec((bb, M_WIDTH), lambda i: (i, 0))
    m_spec = pl.BlockSpec((bb, M_HEADS), lambda i: (i, 0))
    conv_spec = pl.BlockSpec((CONV_W - 1, bb, M_WIDTH), lambda i: (0, i, 0))
    return pl.pallas_call(
        functools.partial(_mlstm_step_body, bb=bb),
        grid=(nb // bb,),
        in_specs=[pl.BlockSpec((bb, M_COLS), lambda i: (i, 0)), conv_spec, c_spec, n_spec, m_spec,
                  _const_spec((CONV_W, M_WIDTH)), row(M_WIDTH),
                  _const_spec((M_HEADS, M_HEAD_DIM, M_HEAD_DIM)), _const_spec((M_HEADS, M_HEAD_DIM, M_HEAD_DIM)),
                  row(LANES), row(LANES), row(M_WIDTH), row(M_WIDTH)],
        out_specs=[pl.BlockSpec((bb, M_WIDTH), lambda i: (i, 0)), c_spec, n_spec, m_spec, conv_spec],
        out_shape=[jax.ShapeDtypeStruct((nb, M_WIDTH), F32),
                   jax.ShapeDtypeStruct(c0.shape, F32),
                   jax.ShapeDtypeStruct(n0.shape, F32),
                   jax.ShapeDtypeStruct(m0.shape, F32),
                   jax.ShapeDtypeStruct(conv0.shape, F32)],
        scratch_shapes=[pltpu.VMEM((bb, M_WIDTH), F32)],
        compiler_params=pltpu.CompilerParams(dimension_semantics=("arbitrary",), vmem_limit_bytes=VMEM_LIMIT),
        name="mlstm_step",
    )(pm, conv0, c0, n0, m0, p["conv_w"], p["conv_b"], p["wq"], p["wk"], p["i_b"], p["f_b"],
      p["gn_w"], p["skip"])


def _out_body(x_ref, pg_ref, yr_ref, ym_ref, rup_ref, mup_ref, wout_ref, gffn_ref, w1_ref, w2_ref, gfin_ref,
              y_ref):
    up_r = jnp.dot(yr_ref[...].astype(BF16), rup_ref[...], preferred_element_type=F32)
    up_m = jnp.dot(ym_ref[...].astype(BF16), mup_ref[...], preferred_element_type=F32)
    merged = _sigmoid(pg_ref[:, 0:D_MODEL]) * up_r + _sigmoid(pg_ref[:, D_MODEL:]) * up_m
    x1 = x_ref[...] + jnp.dot(merged.astype(BF16), wout_ref[...], preferred_element_type=F32)
    hn = _rms(x1, gffn_ref[...]).astype(BF16)
    hid = jnp.maximum(jnp.dot(hn, w1_ref[...], preferred_element_type=F32), 0.0)
    hid = (hid * hid).astype(BF16)
    x2 = x1 + jnp.dot(hid, w2_ref[...], preferred_element_type=F32)
    y_ref[...] = _rms(x2, gfin_ref[...])


def _out(x2, pg, yr, ym, p, tm):
    m = x2.shape[0]
    tile = lambda n: pl.BlockSpec((tm, n), lambda i: (i, 0))
    return pl.pallas_call(
        _out_body,
        grid=(m // tm,),
        in_specs=[tile(D_MODEL), tile(GATE_COLS), tile(R_WIDTH), tile(M_WIDTH),
                  _const_spec((R_WIDTH, D_MODEL)), _const_spec((M_WIDTH, D_MODEL)),
                  _const_spec((D_MODEL, D_MODEL)), _const_spec((1, D_MODEL)),
                  _const_spec((D_MODEL, D_FF)), _const_spec((D_FF, D_MODEL)), _const_spec((1, D_MODEL))],
        out_specs=tile(D_MODEL),
        out_shape=jax.ShapeDtypeStruct((m, D_MODEL), F32),
        compiler_params=pltpu.CompilerParams(dimension_semantics=("arbitrary",), vmem_limit_bytes=VMEM_LIMIT),
        name="merge_ffn",
    )(x2, pg, yr, ym, p["r_up"], p["m_up"], p["w_out"], p["g_ffn"], p["w1"], p["w2"], p["g_fin"])


PROJ_TM = 512
RWKV_TT = 512
MLSTM_TT = 256
OUT_TM = 256
STEP_BB = 8


def _head_segments():
    head = jnp.arange(GROUP_W) // R_HEAD_DIM
    return (head[:, None] == head[None, :]).astype(BF16)


def _pad_lanes(v, start):
    out = jnp.zeros((1, LANES), F32)
    return lax.dynamic_update_slice(out, v.reshape(1, -1), (0, start))


def kernel(x_prompt, x_sample, state_rwkv_shift, state_rwkv_wkv, state_mlstm_C, state_mlstm_n, state_mlstm_m, state_mlstm_conv, norm_mix_g, w_in, r_mu, r_w0, r_w2, r_a0, r_a2, r_g2, r_kk, r_ka, r_rk, r_gn_w, r_gn_b, r_up, m_conv_w, m_conv_b, m_wq, m_wk, m_i_b, m_f_b, m_gn_w, m_skip, m_up, gate_b, w_out, norm_ffn_g, ffn_w1, ffn_w2, norm_final_g):
    nbp, seq, _ = x_prompt.shape
    nbs = x_sample.shape[0]
    w = w_in[0]
    wt = jnp.transpose(w)
    w_cat = jnp.concatenate([wt[:R_SHIFT_COLS], wt[R_SHIFT_COLS + M_COLS:],
                             wt[R_SHIFT_COLS:R_SHIFT_COLS + M_COLS]], axis=0).astype(BF16)
    g_mix = norm_mix_g[0].reshape(1, D_MODEL)
    gb = gate_b[0].reshape(1, GATE_COLS)
    rp = dict(mu=r_mu[0].reshape(1, -1), w0=r_w0[0].reshape(1, -1), w2=r_w2[0].astype(BF16),
              a0=r_a0[0].reshape(1, -1), a2=r_a2[0].astype(BF16), g2=r_g2[0].astype(BF16),
              kk=r_kk[0].reshape(1, -1), ka=r_ka[0].reshape(1, -1), rk=r_rk[0].reshape(1, -1),
              gn_w=r_gn_w[0].reshape(1, -1), gn_b=r_gn_b[0].reshape(1, -1), seg=_head_segments())
    mp = dict(conv_w=m_conv_w[0], conv_b=m_conv_b[0].reshape(1, -1), wq=m_wq[0].astype(BF16),
              wk=m_wk[0].astype(BF16), i_b=_pad_lanes(m_i_b[0], 0), f_b=_pad_lanes(m_f_b[0], M_HEADS),
              gn_w=m_gn_w[0].reshape(1, -1), skip=m_skip[0].reshape(1, -1))
    op = dict(r_up=r_up[0].astype(BF16), m_up=m_up[0].astype(BF16), w_out=w_out[0].astype(BF16),
              g_ffn=norm_ffn_g[0].reshape(1, -1), w1=ffn_w1[0].astype(BF16), w2=ffn_w2[0].astype(BF16),
              g_fin=norm_final_g.reshape(1, -1))

    xp = x_prompt.reshape(nbp * seq, D_MODEL)
    pr, pm, pg = _proj(xp, g_mix, w_cat, gb, PROJ_TM)
    y_r, wkv_p = _rwkv_seq(pr, rp, nbp, seq // RWKV_TT, RWKV_TT)
    y_m, c_p, n_p, m_p = _mlstm_seq(pm, mp, nbp, seq // MLSTM_TT, MLSTM_TT)
    y_p = _out(xp, pg, y_r, y_m, op, OUT_TM).reshape(nbp, seq, D_MODEL)
    conv_p = pm.reshape(nbp, seq, M_COLS)[:, seq - (CONV_W - 1):, :M_WIDTH]

    xn_s, xn_last, pr_s, pm_s, pg_s, prev_s = _proj_step(
        x_sample[:, 0], state_rwkv_shift[0], x_prompt[:, seq - 1], g_mix, w_cat, gb)
    yr_s, wkv_t = _rwkv_step(pr_s, prev_s, jnp.transpose(state_rwkv_wkv[0], (1, 2, 3, 0)), rp)
    wkv_s = jnp.transpose(wkv_t, (3, 0, 1, 2))
    conv0 = jnp.transpose(state_mlstm_conv[0], (1, 0, 2))
    ym_s, c_s, n_s, m_s, conv_s = _mlstm_step(pm_s, conv0, state_mlstm_C[0],
                                              state_mlstm_n[0].reshape(nbs, M_WIDTH),
                                              state_mlstm_m[0], mp, STEP_BB)
    y_s = _out(x_sample[:, 0], pg_s, yr_s, ym_s, op, nbs).reshape(nbs, 1, D_MODEL)

    return (y_p, y_s,
            xn_last[None], wkv_p[None], c_p[None], n_p[:, :, 0, :][None], m_p[:, :, 0, 0][None], conv_p[None],
            xn_s[None], wkv_s[None], c_s[None], n_s.reshape(nbs, M_HEADS, M_HEAD_DIM)[None], m_s[None],
            jnp.transpose(conv_s, (1, 0, 2))[None])
```

```python
import functools
import math

import jax
import jax.numpy as jnp
from jax import lax
from jax.experimental import pallas as pl
from jax.experimental.pallas import tpu as pltpu

F32 = jnp.float32
BF16 = jnp.bfloat16

D_MODEL = 1024
R_HEADS = 8
R_HEAD_DIM = 64
R_WIDTH = R_HEADS * R_HEAD_DIM
GROUP_HEADS = 4
R_GROUPS = R_HEADS // GROUP_HEADS
GROUP_W = GROUP_HEADS * R_HEAD_DIM
W_LORA = 64
A_LORA = 64
G_LORA = 128
R_GN_EPS = 64e-5
M_HEADS = 4
M_HEAD_DIM = 128
M_WIDTH = M_HEADS * M_HEAD_DIM
CONV_W = 4
M_GN_EPS = 1e-5
D_FF = 4 * D_MODEL
RMS_EPS = 1e-6
R_SHIFT_COLS = 3 * R_WIDTH + W_LORA + A_LORA + G_LORA
M_COLS = 3 * M_WIDTH + 2 * M_HEADS
GATE_COLS = 2 * D_MODEL

LANES = 128
SUBLANES = 8
CHUNK = 64
IN_COLS = R_SHIFT_COLS + M_COLS + GATE_COLS
W_GATE0 = R_SHIFT_COLS
W_M0 = R_SHIFT_COLS + GATE_COLS
VMEM_LIMIT = 56 * 1024 * 1024


def _mm(a, b, dims="nn"):
    ca = 1 if dims[0] == "n" else 0
    cb = 0 if dims[1] == "n" else 1
    dn = (((ca,), (cb,)), ((), ()))
    return lax.dot_general(a.astype(BF16), b.astype(BF16), dn, preferred_element_type=F32)


def _mm_split(a, b, dims="nn"):
    a_hi = a.astype(BF16)
    b_hi = b.astype(BF16)
    a_lo = a - a_hi.astype(F32)
    b_lo = b - b_hi.astype(F32)
    return _mm(a_hi, b_hi, dims) + (_mm(a_hi, b_lo, dims) + _mm(a_lo, b_hi, dims))


def _mm_mask_lhs(mask, b):
    b_hi = b.astype(BF16)
    return _mm(mask, b_hi) + _mm(mask, b - b_hi.astype(F32))


def _mm_mask_rhs(a, mask):
    a_hi = a.astype(BF16)
    return _mm(a_hi, mask) + _mm(a - a_hi.astype(F32), mask)


def _rms(x, g):
    return x * lax.rsqrt(jnp.mean(x * x, axis=-1, keepdims=True) + RMS_EPS) * g


def _sigmoid(x):
    return 1.0 / (1.0 + jnp.exp(-x))


def _log_sigmoid(x):
    return jnp.minimum(x, 0.0) - jnp.log(1.0 + jnp.exp(-jnp.abs(x)))


def _iota2(shape, axis):
    return lax.broadcasted_iota(jnp.int32, shape, axis)


def _const_spec(shape):
    nd = len(shape)
    return pl.BlockSpec(shape, lambda *_: (0,) * nd, pipeline_mode=pl.Buffered(1))


def _proj_cols(xb, wt_ref, lo, hi):
    return lax.dot_general(xb, wt_ref[lo:hi, :], (((1,), (1,)), ((), ())), preferred_element_type=F32)


def _proj_body(x_ref, g_ref, w_ref, gb_ref, pr_ref, pm_ref, pg_ref):
    xb = _rms(x_ref[...], g_ref[...]).astype(BF16)
    pr_ref[...] = _proj_cols(xb, w_ref, 0, R_SHIFT_COLS)
    pm_ref[...] = _proj_cols(xb, w_ref, W_M0, IN_COLS)
    pg_ref[...] = _proj_cols(xb, w_ref, W_GATE0, W_M0) + gb_ref[...]


def _proj(x2, g, w_cat, gate_b, tm):
    m = x2.shape[0]
    return pl.pallas_call(
        _proj_body,
        grid=(m // tm,),
        in_specs=[pl.BlockSpec((tm, D_MODEL), lambda i: (i, 0)),
                  _const_spec((1, D_MODEL)),
                  _const_spec((IN_COLS, D_MODEL)),
                  _const_spec((1, GATE_COLS))],
        out_specs=[pl.BlockSpec((tm, R_SHIFT_COLS), lambda i: (i, 0)),
                   pl.BlockSpec((tm, M_COLS), lambda i: (i, 0)),
                   pl.BlockSpec((tm, GATE_COLS), lambda i: (i, 0))],
        out_shape=[jax.ShapeDtypeStruct((m, R_SHIFT_COLS), F32),
                   jax.ShapeDtypeStruct((m, M_COLS), F32),
                   jax.ShapeDtypeStruct((m, GATE_COLS), F32)],
        compiler_params=pltpu.CompilerParams(dimension_semantics=("arbitrary",), vmem_limit_bytes=VMEM_LIMIT),
        name="proj",
    )(x2, g, w_cat, gate_b)


def _proj_step_body(xs_ref, sh_ref, xl_ref, g_ref, w_ref, gb_ref,
                    xns_ref, xnl_ref, pr_ref, pm_ref, pg_ref, prev_ref):
    g = g_ref[...]
    xn = _rms(xs_ref[...], g)
    xns_ref[...] = xn
    xnl_ref[...] = _rms(xl_ref[...], g)
    xb = xn.astype(BF16)
    pr_ref[...] = _proj_cols(xb, w_ref, 0, R_SHIFT_COLS)
    pm_ref[...] = _proj_cols(xb, w_ref, W_M0, IN_COLS)
    pg_ref[...] = _proj_cols(xb, w_ref, W_GATE0, W_M0) + gb_ref[...]
    prev_ref[...] = _proj_cols(sh_ref[...].astype(BF16), w_ref, 0, R_SHIFT_COLS)


def _proj_step(xs, shift0, xlast, g, w_cat, gate_b):
    nb = xs.shape[0]
    nl = xlast.shape[0]
    return pl.pallas_call(
        _proj_step_body,
        out_shape=[jax.ShapeDtypeStruct((nb, D_MODEL), F32),
                   jax.ShapeDtypeStruct((nl, D_MODEL), F32),
                   jax.ShapeDtypeStruct((nb, R_SHIFT_COLS), F32),
                   jax.ShapeDtypeStruct((nb, M_COLS), F32),
                   jax.ShapeDtypeStruct((nb, GATE_COLS), F32),
                   jax.ShapeDtypeStruct((nb, R_SHIFT_COLS), F32)],
        compiler_params=pltpu.CompilerParams(vmem_limit_bytes=VMEM_LIMIT),
        name="proj_step",
    )(xs, shift0, xlast, g, w_cat, gate_b)


LOG_DECAY_SCALE = -math.exp(-0.5)


def _rwkv_mix(pr, prev, mu, w0, w2, a0, a2, g2, r_kk, r_ka):
    mixed = pr + (prev - pr) * mu
    r = mixed[:, 0:R_WIDTH]
    k = mixed[:, R_WIDTH:2 * R_WIDTH]
    v = mixed[:, 2 * R_WIDTH:3 * R_WIDTH]
    xw = mixed[:, 3 * R_WIDTH:3 * R_WIDTH + W_LORA]
    xa = mixed[:, 3 * R_WIDTH + W_LORA:3 * R_WIDTH + W_LORA + A_LORA]
    xg = mixed[:, 3 * R_WIDTH + W_LORA + A_LORA:]
    w = w0 + _mm(jnp.tanh(xw), w2)
    logw = LOG_DECAY_SCALE * _sigmoid(w)
    a = _sigmoid(a0 + _mm(xa, a2))
    g = _mm(_sigmoid(xg), g2)
    kk0 = k * r_kk
    k2 = k * (1.0 + (a - 1.0) * r_ka)
    return r, k2, v, logw, a, g, kk0


def _rwkv_tile(pr_ref, mu_ref, w0_ref, w2_ref, a0_ref, a2_ref, g2_ref, kk_ref, ka_ref, rk_ref,
               gnw_ref, gnb_ref, seg_ref, s_scr, carry_scr, tt):
    pr = pr_ref[...]
    row = _iota2(pr.shape, 0)
    prev = jnp.where(row == 0, carry_scr[...], pltpu.roll(pr, 1, axis=0))
    carry_scr[...] = pr[tt - 1:tt, :]

    r, k2, v, logw, a, g, kk0 = _rwkv_mix(pr, prev, mu_ref[...], w0_ref[...], w2_ref[...], a0_ref[...],
                                          a2_ref[...], g2_ref[...], kk_ref[...], ka_ref[...])
    yield
    ri = _iota2((CHUNK, GROUP_W), 0)
    ci = _iota2((CHUNK, GROUP_W), 1)
    src = ci % R_HEAD_DIM
    lane_head = ci // R_HEAD_DIM
    low_incl = src <= ri
    low_strict = src < ri
    eye = (src == ri).astype(F32)
    rr = _iota2((GROUP_W, GROUP_W), 0)
    cc = _iota2((GROUP_W, GROUP_W), 1)
    same_head = (rr // R_HEAD_DIM) == (cc // R_HEAD_DIM)
    eye_bd = (rr == cc).astype(F32)
    seg = seg_ref[...]

    def seg_sum(x, mm=_mm):
        return jnp.concatenate([mm(x[:, q * GROUP_W:(q + 1) * GROUP_W], seg) for q in range(R_GROUPS)], axis=1)
    tril = (_iota2((CHUNK, CHUNK), 1) <= _iota2((CHUNK, CHUNK), 0)).astype(F32)

    nc = tt // CHUNK
    cum = jnp.concatenate([_mm_mask_lhs(tril, logw[c * CHUNK:(c + 1) * CHUNK]) for c in range(nc)], axis=0)
    yield
    cum_last = jnp.concatenate(
        [jnp.broadcast_to(cum[(c + 1) * CHUNK - 1:(c + 1) * CHUNK, :], (CHUNK, R_WIDTH)) for c in range(nc)], axis=0)
    d_inv = jnp.exp(-cum)
    d_end = jnp.exp(cum_last - cum)
    d_last = jnp.exp(cum_last)
    kk = kk0 / jnp.maximum(jnp.sqrt(seg_sum(kk0 * kk0)), 1e-12)
    yield
    kka = kk * a
    at = kk * jnp.exp(cum - logw)
    rt = r * jnp.exp(cum)
    bt = kka * d_inv
    kt = k2 * d_inv
    bh = kka * d_end
    kh = k2 * d_end

    items = [(c, p) for c in range(nc) for p in range(R_GROUPS)]
    blk = lambda x, c, p: x[c * CHUNK:(c + 1) * CHUNK, p * GROUP_W:(p + 1) * GROUP_W]
    each = lambda f, *ls: [f(*xs) for xs in zip(*ls)]
    cat0 = lambda *xs: jnp.concatenate(xs, axis=0)
    cat1 = lambda *xs: jnp.concatenate(xs, axis=1)

    def bd(x):
        return cat0(*[jnp.where(lane_head == j, x, 0.0) for j in range(GROUP_HEADS)])

    a_ = [blk(at, c, p) for c, p in items]
    b_ = [blk(bt, c, p) for c, p in items]
    bh_ = [blk(bh, c, p) for c, p in items]
    r_ = [blk(rt, c, p) for c, p in items]
    k_ = [blk(kt, c, p) for c, p in items]
    kh_ = [blk(kh, c, p) for c, p in items]
    v_ = [blk(v, c, p) for c, p in items]
    vbd = [bd(x) for x in v_]
    pq = each(lambda a1, r1, b1, k1: _mm(cat0(a1, r1), cat0(bd(b1), bd(k1)), "nt"), a_, r_, b_, k_)
    yield
    t_ab = [jnp.where(low_strict, x[:CHUNK, :GROUP_W], 0.0) for x in pq]
    t_ak = [jnp.where(low_strict, x[:CHUNK, GROUP_W:], 0.0) for x in pq]
    m_rb = [jnp.where(low_incl, x[CHUNK:, :GROUP_W], 0.0) for x in pq]
    m_rk = [jnp.where(low_incl, x[CHUNK:, GROUP_W:], 0.0) for x in pq]
    q_pow = each(lambda t: _mm(t, bd(t)), t_ab)
    yield
    x_inv = [eye - t for t in t_ab]
    for _ in range(4):
        z = each(lambda x, q: _mm(cat0(x, q), bd(q)), x_inv, q_pow)
        x_inv = [x + zz[:CHUNK] for x, zz in zip(x_inv, z)]
        q_pow = [zz[CHUNK:] for zz in z]
        yield
    x_inv = each(lambda x, q: x + _mm(x, bd(q)), x_inv, q_pow)
    yield
    tv = each(_mm, t_ak, vbd)
    yield
    xw = each(lambda x, a1, t: _mm(x, cat1(bd(a1), bd(t))), x_inv, a_, tv)
    yield
    mw = each(lambda m1, x: _mm(m1, cat1(bd(x[:, :GROUP_W]), bd(x[:, GROUP_W:]))), m_rb, xw)
    yield
    q_eff = [r1 - m1[:, :GROUP_W] for r1, m1 in zip(r_, mw)]
    y_loc = each(lambda m1, vb, m2: _mm(m1, vb) - m2[:, GROUP_W:], m_rk, vbd, mw)
    yield
    g_bd = [eye_bd * blk(d_last, c, p)[0:1] - jnp.where(same_head, _mm(x[:, :GROUP_W], b1, "tn"), 0.0)
            for (c, p), x, b1 in zip(items, xw, bh_)]
    yield
    h_full = each(lambda vv, x, k1, b1: _mm(cat0(vv, x[:, GROUP_W:]), cat0(k1, -b1), "tn"), v_, xw, kh_, bh_)
    yield
    h_pair = []
    for x in h_full:
        acc = x[:CHUNK]
        for j in range(1, GROUP_HEADS):
            acc = jnp.where(lane_head == j, x[j * CHUNK:(j + 1) * CHUNK], acc)
        h_pair.append(acc)

    state = [s_scr[p] for p in range(R_GROUPS)]
    ys = []
    for i, (c, p) in enumerate(items):
        ys.append(_mm(q_eff[i], bd(state[p]), "nt") + y_loc[i])
        state[p] = _mm(state[p], g_bd[i]) + h_pair[i]
        if p == R_GROUPS - 1:
            yield
    for p in range(R_GROUPS):
        s_scr[p] = state[p]

    y_all = cat0(*[cat1(*ys[c * R_GROUPS:(c + 1) * R_GROUPS]) for c in range(nc)])
    yc = y_all - seg_sum(y_all) * (1.0 / R_HEAD_DIM)
    yield
    var = seg_sum(yc * yc) * (1.0 / R_HEAD_DIM)
    yield
    bonus = seg_sum(r * k2 * rk_ref[...], _mm_mask_rhs)
    return (yc * lax.rsqrt(var + R_GN_EPS) * gnw_ref[...] + gnb_ref[...] + bonus * v) * g


N_RWKV_PARAMS = 12


def _rwkv_seq_body(*refs, tt):
    pr_ref, params = refs[0], refs[1:1 + N_RWKV_PARAMS]
    y_ref, s_out_ref, s_scr, carry_scr = refs[1 + N_RWKV_PARAMS:]
    t_idx = pl.program_id(1)

    @pl.when(t_idx == 0)
    def _():
        s_scr[...] = jnp.zeros_like(s_scr)
        carry_scr[...] = jnp.zeros_like(carry_scr)

    y_ref[...] = _run(_rwkv_tile(pr_ref, *params, s_scr, carry_scr, tt))

    @pl.when(t_idx == pl.num_programs(1) - 1)
    def _():
        for p in range(R_GROUPS):
            for j in range(GROUP_HEADS):
                s_out_ref[0, GROUP_HEADS * p + j] = s_scr[p][:, j * R_HEAD_DIM:(j + 1) * R_HEAD_DIM]


def _rwkv_seq(pr, p, nb, nt, tt):
    m = pr.shape[0]
    row = lambda n: _const_spec((1, n))
    return pl.pallas_call(
        functools.partial(_rwkv_seq_body, tt=tt),
        grid=(nb, nt),
        in_specs=[pl.BlockSpec((tt, R_SHIFT_COLS), lambda b, t: (b * nt + t, 0)),
                  row(R_SHIFT_COLS), row(R_WIDTH), _const_spec((W_LORA, R_WIDTH)), row(R_WIDTH),
                  _const_spec((A_LORA, R_WIDTH)), _const_spec((G_LORA, R_WIDTH)), row(R_WIDTH), row(R_WIDTH),
                  row(R_WIDTH), row(R_WIDTH), row(R_WIDTH), _const_spec((GROUP_W, GROUP_W))],
        out_specs=[pl.BlockSpec((tt, R_WIDTH), lambda b, t: (b * nt + t, 0)),
                   pl.BlockSpec((1, R_HEADS, R_HEAD_DIM, R_HEAD_DIM), lambda b, t: (b, 0, 0, 0))],
        out_shape=[jax.ShapeDtypeStruct((m, R_WIDTH), F32),
                   jax.ShapeDtypeStruct((nb, R_HEADS, R_HEAD_DIM, R_HEAD_DIM), F32)],
        scratch_shapes=[pltpu.VMEM((R_GROUPS, R_HEAD_DIM, GROUP_W), F32),
                        pltpu.VMEM((1, R_SHIFT_COLS), F32)],
        compiler_params=pltpu.CompilerParams(dimension_semantics=("arbitrary", "arbitrary"),
                                             vmem_limit_bytes=VMEM_LIMIT),
        name="rwkv_seq",
    )(pr, p["mu"], p["w0"], p["w2"], p["a0"], p["a2"], p["g2"], p["kk"], p["ka"], p["rk"],
      p["gn_w"], p["gn_b"], p["seg"])


def _rwkv_step_body(pr_ref, prev_ref, s_ref, mu_ref, w0_ref, w2_ref, a0_ref, a2_ref, g2_ref, kk_ref, ka_ref,
                    rk_ref, gnw_ref, gnb_ref, seg_ref, y_ref, s_out_ref,
                    kk_scr, kka_scr, dec_scr, k_scr, v_scr, r_scr, yt_scr, g_scr, vrow_scr, bonus_scr):
    h = pl.program_id(0)
    seg = seg_ref[...]

    def seg_sum(x, mm=_mm):
        return jnp.concatenate([mm(x[:, q * GROUP_W:(q + 1) * GROUP_W], seg) for q in range(R_GROUPS)], axis=1)

    @pl.when(h == 0)
    def _():
        r, k2, v, logw, a, g, kk0 = _rwkv_mix(pr_ref[...], prev_ref[...], mu_ref[...], w0_ref[...], w2_ref[...],
                                              a0_ref[...], a2_ref[...], g2_ref[...], kk_ref[...], ka_ref[...])
        kk = kk0 / jnp.maximum(jnp.sqrt(seg_sum(kk0 * kk0)), 1e-12)
        kk_scr[...] = kk.T
        kka_scr[...] = (kk * a).T
        dec_scr[...] = jnp.exp(logw).T
        k_scr[...] = k2.T
        v_scr[...] = v.T
        r_scr[...] = r.T
        g_scr[...] = g
        vrow_scr[...] = v
        bonus_scr[...] = seg_sum(r * k2 * rk_ref[...], _mm_mask_rhs)

    base = pl.multiple_of(h * R_HEAD_DIM, R_HEAD_DIM)
    hs = pl.ds(base, R_HEAD_DIM)
    kk_t, kka_t, dec_t, k_t, r_t = kk_scr[hs, :], kka_scr[hs, :], dec_scr[hs, :], k_scr[hs, :], r_scr[hs, :]
    for i in range(R_HEAD_DIM):
        s0 = s_ref[0, i]
        s_kk = jnp.sum(s0 * kk_t, axis=0, keepdims=True)
        s1 = s0 * dec_t - s_kk * kka_t + v_scr[pl.ds(base + i, 1), :] * k_t
        s_out_ref[0, i] = s1
        yt_scr[pl.ds(base + i, 1), :] = jnp.sum(s1 * r_t, axis=0, keepdims=True)

    @pl.when(h == pl.num_programs(0) - 1)
    def _():
        y = yt_scr[...].T
        yc = y - seg_sum(y) * (1.0 / R_HEAD_DIM)
        var = seg_sum(yc * yc) * (1.0 / R_HEAD_DIM)
        y_ref[...] = (yc * lax.rsqrt(var + R_GN_EPS) * gnw_ref[...] + gnb_ref[...]
                      + bonus_scr[...] * vrow_scr[...]) * g_scr[...]


def _rwkv_step(pr, prev, s0_t, p):
    nb = pr.shape[0]
    row = lambda n: _const_spec((1, n))
    state_spec = pl.BlockSpec((1, R_HEAD_DIM, R_HEAD_DIM, nb), lambda h: (h, 0, 0, 0))
    tr = pltpu.VMEM((R_WIDTH, nb), F32)
    rw = pltpu.VMEM((nb, R_WIDTH), F32)
    return pl.pallas_call(
        _rwkv_step_body,
        grid=(R_HEADS,),
        in_specs=[_const_spec((nb, R_SHIFT_COLS)), _const_spec((nb, R_SHIFT_COLS)), state_spec,
                  row(R_SHIFT_COLS), row(R_WIDTH), _const_spec((W_LORA, R_WIDTH)), row(R_WIDTH),
                  _const_spec((A_LORA, R_WIDTH)), _const_spec((G_LORA, R_WIDTH)), row(R_WIDTH), row(R_WIDTH),
                  row(R_WIDTH), row(R_WIDTH), row(R_WIDTH), _const_spec((GROUP_W, GROUP_W))],
        out_specs=[pl.BlockSpec((nb, R_WIDTH), lambda h: (0, 0)), state_spec],
        out_shape=[jax.ShapeDtypeStruct((nb, R_WIDTH), F32),
                   jax.ShapeDtypeStruct(s0_t.shape, F32)],
        scratch_shapes=[tr, tr, tr, tr, tr, tr, tr, rw, rw, rw],
        compiler_params=pltpu.CompilerParams(dimension_semantics=("arbitrary",), vmem_limit_bytes=VMEM_LIMIT),
        name="rwkv_step",
    )(pr, prev, s0_t, p["mu"], p["w0"], p["w2"], p["a0"], p["a2"], p["g2"], p["kk"], p["ka"], p["rk"],
      p["gn_w"], p["gn_b"], p["seg"])


GATE_LANE0 = 3 * M_WIDTH


def _gate_lanes(pm_ref):
    gates = pm_ref[:, GATE_LANE0:GATE_LANE0 + 2 * M_HEADS]
    return jnp.concatenate([gates, jnp.zeros((gates.shape[0], LANES - 2 * M_HEADS), F32)], axis=1)


def _head_norm_rows(x, eps):
    xc = x - jnp.mean(x, axis=1, keepdims=True)
    var = jnp.mean(xc * xc, axis=1, keepdims=True)
    return xc * lax.rsqrt(var + eps)


def _cummax_rows(x):
    n = x.shape[0]
    row = _iota2(x.shape, 0)
    d = 1
    while d < n:
        if d < SUBLANES:
            shifted = jnp.where(row < d, -jnp.inf, pltpu.roll(x, d, axis=0))
        else:
            shifted = jnp.concatenate([jnp.full((d, x.shape[1]), -jnp.inf, x.dtype), x[:n - d]], axis=0)
        x = jnp.maximum(x, shifted)
        d *= 2
    return x


def _mlstm_seq_body(pm_ref, cw_ref, cb_ref, wq_ref, wk_ref, ib_ref, fb_ref, gnw_ref, skip_ref,
                    y_ref, c_out_ref, n_out_ref, m_out_ref, c_scr, n_scr, m_scr, carry_scr, *, tt):
    t_idx = pl.program_id(1)

    @pl.when(t_idx == 0)
    def _():
        c_scr[...] = jnp.zeros_like(c_scr)
        n_scr[...] = jnp.zeros_like(n_scr)
        m_scr[...] = jnp.zeros_like(m_scr)
        carry_scr[...] = jnp.zeros_like(carry_scr)

    xm = pm_ref[:, 0:M_WIDTH]
    carry = carry_scr[...]
    carry_scr[...] = xm[tt - SUBLANES:tt, :]
    row8 = _iota2((SUBLANES, M_WIDTH), 0)
    cw = cw_ref[...]
    xc = cb_ref[...] + xm * cw[CONV_W - 1:CONV_W, :]
    for s in range(1, CONV_W):
        rolled = pltpu.roll(xm, s, axis=0)
        top = jnp.where(row8 < s, pltpu.roll(carry, s, axis=0), rolled[0:SUBLANES])
        shifted = jnp.concatenate([top, rolled[SUBLANES:]], axis=0)
        xc = xc + shifted * cw[CONV_W - 1 - s:CONV_W - s, :]
    xc = xc * _sigmoid(xc)

    gt = _gate_lanes(pm_ref)
    li_all = gt + ib_ref[...]
    lf_all = _log_sigmoid(gt + fb_ref[...])

    ri = _iota2((CHUNK, CHUNK), 0)
    ci = _iota2((CHUNK, CHUNK), 1)
    causal = ci <= ri
    tril = causal.astype(F32)
    gnw = gnw_ref[...]
    skip = skip_ref[...]

    ones_cl = jnp.ones((CHUNK, LANES), BF16)
    ones_ll = jnp.ones((LANES, LANES), BF16)
    nc = tt // CHUNK
    rows = lambda c: slice(c * CHUNK, (c + 1) * CHUNK)
    lanes = lambda h: slice(h * M_HEAD_DIM, (h + 1) * M_HEAD_DIM)

    bc_all = jnp.concatenate([_mm_mask_lhs(tril, lf_all[rows(c)]) for c in range(nc)], axis=0)
    bc_al = pltpu.roll(bc_all, LANES - M_HEADS, axis=1)
    u_all = li_all - bc_al
    cm_all = jnp.concatenate([_cummax_rows(u_all[rows(c)]) for c in range(nc)], axis=0)
    u_t = u_all.T

    items = [(c, h) for c in range(nc) for h in range(M_HEADS)]
    rep = lambda x, c, h: jnp.broadcast_to(x[rows(c), h:h + 1], (CHUNK, LANES))
    xc_b = [xc[rows(c), lanes(h)] for c, h in items]
    q = [_mm(x, wq_ref[h]) * (M_HEAD_DIM ** -0.5) for (c, h), x in zip(items, xc_b)]
    k = [_mm(x, wk_ref[h]) for (c, h), x in zip(items, xc_b)]
    vv = [pm_ref[rows(c), M_WIDTH + h * M_HEAD_DIM:M_WIDTH + (h + 1) * M_HEAD_DIM] for c, h in items]
    cm_r = [rep(cm_all, c, h) for c, h in items]
    bc_r = [rep(bc_al, c, h) for c, h in items]
    u_r = [rep(u_all, c, h) for c, h in items]

    m_run = [m_scr[h][0:1, 0:1] for h in range(M_HEADS)]
    m_prev, m_cap, m_end = [], [], []
    for i, (c, h) in enumerate(items):
        m_prev.append(m_run[h])
        m_cap.append(jnp.maximum(m_run[h], cm_r[i]))
        m_end.append(m_cap[i][CHUNK - 1:CHUNK, 0:1])
        m_run[h] = bc_r[i][CHUNK - 1:CHUNK, 0:1] + m_end[i]

    w_st = [jnp.exp(mp - mc) for mp, mc in zip(m_prev, m_cap)]
    floor = [jnp.exp(-(b + mc)) for b, mc in zip(bc_r, m_cap)]
    w_in = [jnp.exp(jnp.where(causal, u_t[h:h + 1, rows(c)] - mc[:, :CHUNK], -jnp.inf))
            for (c, h), mc in zip(items, m_cap)]
    s = [_mm(qq, kk, "nt") * w for qq, kk, w in zip(q, k, w_in)]
    sv = [_mm(ss, v1) for ss, v1 in zip(s, vv)]
    s_sum = [_mm(ss, ones_cl) for ss in s]
    we = [jnp.exp(u - me) for u, me in zip(u_r, m_end)]
    ge = [jnp.exp(mp - me) for mp, me in zip(m_prev, m_end)]
    kv = [_mm(kk, w1 * v1, "tn") for kk, w1, v1 in zip(k, we, vv)]
    k_sum = [jnp.sum(w1 * kk, axis=0, keepdims=True) for w1, kk in zip(we, k)]

    c_run = [c_scr[h] for h in range(M_HEADS)]
    n_run = [n_scr[h][0:1, :] for h in range(M_HEADS)]
    num, den = [], []
    for i, (c, h) in enumerate(items):
        num.append(w_st[i] * _mm(q[i], c_run[h]) + sv[i])
        den.append(w_st[i] * _mm(q[i] * n_run[h], ones_ll) + s_sum[i])
        c_run[h] = ge[i] * c_run[h] + kv[i]
        n_run[h] = ge[i] * n_run[h] + k_sum[i]
    for h in range(M_HEADS):
        c_scr[h] = c_run[h]
        n_scr[h] = jnp.broadcast_to(n_run[h], (SUBLANES, M_HEAD_DIM))
        m_scr[h] = jnp.broadcast_to(m_run[h], (SUBLANES, LANES))

    for i, (c, h) in enumerate(items):
        hh = num[i] / jnp.maximum(jnp.abs(den[i]), floor[i])
        hc = hh - _mm(hh, ones_ll) * (1.0 / M_HEAD_DIM)
        var = _mm(hc * hc, ones_ll) * (1.0 / M_HEAD_DIM)
        hn = hc * lax.rsqrt(var + M_GN_EPS) * gnw[:, lanes(h)] + skip[:, lanes(h)] * xc_b[i]
        o = pm_ref[rows(c), 2 * M_WIDTH + h * M_HEAD_DIM:2 * M_WIDTH + (h + 1) * M_HEAD_DIM]
        y_ref[rows(c), lanes(h)] = _sigmoid(o) * hn

    @pl.when(t_idx == pl.num_programs(1) - 1)
    def _():
        c_out_ref[0] = c_scr[...]
        n_out_ref[0] = n_scr[...]
        m_out_ref[0] = m_scr[...]


def _mlstm_seq(pm, p, nb, nt, tt):
    m = pm.shape[0]
    row = lambda n: _const_spec((1, n))
    return pl.pallas_call(
        functools.partial(_mlstm_seq_body, tt=tt),
        grid=(nb, nt),
        in_specs=[pl.BlockSpec((tt, M_COLS), lambda b, t: (b * nt + t, 0)),
                  _const_spec((CONV_W, M_WIDTH)), row(M_WIDTH),
                  _const_spec((M_HEADS, M_HEAD_DIM, M_HEAD_DIM)), _const_spec((M_HEADS, M_HEAD_DIM, M_HEAD_DIM)),
                  row(LANES), row(LANES), row(M_WIDTH), row(M_WIDTH)],
        out_specs=[pl.BlockSpec((tt, M_WIDTH), lambda b, t: (b * nt + t, 0)),
                   pl.BlockSpec((1, M_HEADS, M_HEAD_DIM, M_HEAD_DIM), lambda b, t: (b, 0, 0, 0)),
                   pl.BlockSpec((1, M_HEADS, SUBLANES, M_HEAD_DIM), lambda b, t: (b, 0, 0, 0)),
                   pl.BlockSpec((1, M_HEADS, SUBLANES, LANES), lambda b, t: (b, 0, 0, 0))],
        out_shape=[jax.ShapeDtypeStruct((m, M_WIDTH), F32),
                   jax.ShapeDtypeStruct((nb, M_HEADS, M_HEAD_DIM, M_HEAD_DIM), F32),
                   jax.ShapeDtypeStruct((nb, M_HEADS, SUBLANES, M_HEAD_DIM), F32),
                   jax.ShapeDtypeStruct((nb, M_HEADS, SUBLANES, LANES), F32)],
        scratch_shapes=[pltpu.VMEM((M_HEADS, M_HEAD_DIM, M_HEAD_DIM), F32),
                        pltpu.VMEM((M_HEADS, SUBLANES, M_HEAD_DIM), F32),
                        pltpu.VMEM((M_HEADS, SUBLANES, LANES), F32),
                        pltpu.VMEM((SUBLANES, M_WIDTH), F32)],
        compiler_params=pltpu.CompilerParams(dimension_semantics=("arbitrary", "arbitrary"),
                                             vmem_limit_bytes=VMEM_LIMIT),
        name="mlstm_seq",
    )(pm, p["conv_w"], p["conv_b"], p["wq"], p["wk"], p["i_b"], p["f_b"], p["gn_w"], p["skip"])


def _mlstm_step_body(pm_ref, conv_ref, c_ref, n_ref, m_ref, cw_ref, cb_ref, wq_ref, wk_ref, ib_ref, fb_ref,
                     gnw_ref, skip_ref, y_ref, c_out_ref, n_out_ref, m_out_ref, conv_out_ref, hv_scr, *, bb):
    xm = pm_ref[:, 0:M_WIDTH]
    cw = cw_ref[...]
    xc = cb_ref[...] + xm * cw[CONV_W - 1:CONV_W, :]
    for j in range(CONV_W - 1):
        xc = xc + conv_ref[j] * cw[j:j + 1, :]
    xc = xc * _sigmoid(xc)
    for j in range(CONV_W - 2):
        conv_out_ref[j] = conv_ref[j + 1]
    conv_out_ref[CONV_W - 2] = xm

    gt = _gate_lanes(pm_ref)
    li = (gt + ib_ref[...])[:, 0:M_HEADS]
    lf = _log_sigmoid(gt + fb_ref[...])[:, M_HEADS:2 * M_HEADS]
    m_prev = m_ref[...]
    g_st = lf + m_prev
    m_t = jnp.maximum(g_st, (lf - lf) + li)
    w_in = jnp.exp(((lf - lf) + li) - m_t)
    w_st = jnp.exp(g_st - m_t)
    floor = jnp.exp(-m_t)
    m_out_ref[...] = m_t

    ri = _iota2((M_HEAD_DIM, M_HEAD_DIM), 0)
    ci = _iota2((M_HEAD_DIM, M_HEAD_DIM), 1)
    eye = ri == ci
    for h in range(M_HEADS):
        hs = slice(h * M_HEAD_DIM, (h + 1) * M_HEAD_DIM)
        xc_h = xc[:, hs]
        q = _mm(xc_h, wq_ref[h]) * (M_HEAD_DIM ** -0.5)
        k = _mm(xc_h, wk_ref[h])
        vv = pm_ref[:, M_WIDTH + h * M_HEAD_DIM:M_WIDTH + (h + 1) * M_HEAD_DIM]
        n0 = n_ref[:, hs]
        qk = jnp.sum(q * k, axis=1, keepdims=True)
        qn = jnp.sum(q * n0, axis=1, keepdims=True)
        s = qk * w_in[:, h:h + 1]
        den = w_st[:, h:h + 1] * qn + s
        inv = 1.0 / jnp.maximum(jnp.abs(den), floor[:, h:h + 1])
        n_out_ref[:, hs] = w_st[:, h:h + 1] * n0 + w_in[:, h:h + 1] * k
        for b in range(bb):
            bs = slice(b, b + 1)
            c0 = c_ref[b, h]
            q_col = jnp.sum(jnp.where(eye, q[bs], 0.0), axis=1, keepdims=True)
            k_col = jnp.sum(jnp.where(eye, k[bs], 0.0), axis=1, keepdims=True)
            qc = jnp.sum(c0 * q_col, axis=0, keepdims=True)
            ge = w_st[bs, h:h + 1]
            we = w_in[bs, h:h + 1]
            c_out_ref[b, h] = ge * c0 + (we * k_col) * vv[bs]
            hv_scr[bs, hs] = (ge * qc + s[bs] * vv[bs]) * inv[bs]
    hv = hv_scr[...]
    gnw = gnw_ref[...]
    skip = skip_ref[...]
    outs = []
    for h in range(M_HEADS):
        hs = slice(h * M_HEAD_DIM, (h + 1) * M_HEAD_DIM)
        outs.append(_head_norm_rows(hv[:, hs], M_GN_EPS) * gnw[:, hs] + skip[:, hs] * xc[:, hs])
    o = pm_ref[:, 2 * M_WIDTH:3 * M_WIDTH]
    y_ref[...] = _sigmoid(o) * jnp.concatenate(outs, axis=1)


def _mlstm_step(pm, conv0, c0, n0, m0, p, bb):
    nb = pm.shape[0]
    row = lambda n: _const_spec((1, n))
    c_spec = pl.BlockSpec((bb, M_HEADS, M_HEAD_DIM, M_HEAD_DIM), lambda i: (i, 0, 0, 0))
    n_spec = pl.BlockSpec((bb, M_WIDTH), lambda i: (i, 0))
    m_spec = pl.BlockSpec((bb, M_HEADS), lambda i: (i, 0))
    conv_spec = pl.BlockSpec((CONV_W - 1, bb, M_WIDTH), lambda i: (0, i, 0))
    return pl.pallas_call(
        functools.partial(_mlstm_step_body, bb=bb),
        grid=(nb // bb,),
        in_specs=[pl.BlockSpec((bb, M_COLS), lambda i: (i, 0)), conv_spec, c_spec, n_spec, m_spec,
                  _const_spec((CONV_W, M_WIDTH)), row(M_WIDTH),
                  _const_spec((M_HEADS, M_HEAD_DIM, M_HEAD_DIM)), _const_spec((M_HEADS, M_HEAD_DIM, M_HEAD_DIM)),
                  row(LANES), row(LANES), row(M_WIDTH), row(M_WIDTH)],
        out_specs=[pl.BlockSpec((bb, M_WIDTH), lambda i: (i, 0)), c_spec, n_spec, m_spec, conv_spec],
        out_shape=[jax.ShapeDtypeStruct((nb, M_WIDTH), F32),
                   jax.ShapeDtypeStruct(c0.shape, F32),
                   jax.ShapeDtypeStruct(n0.shape, F32),
                   jax.ShapeDtypeStruct(m0.shape, F32),
                   jax.ShapeDtypeStruct(conv0.shape, F32)],
        scratch_shapes=[pltpu.VMEM((bb, M_WIDTH), F32)],
        compiler_params=pltpu.CompilerParams(dimension_semantics=("arbitrary",), vmem_limit_bytes=VMEM_LIMIT),
        name="mlstm_step",
    )(pm, conv0, c0, n0, m0, p["conv_w"], p["conv_b"], p["wq"], p["wk"], p["i_b"], p["f_b"],
      p["gn_w"], p["skip"])


FFN_BLOCK = 1024


def _ffn_tile(x_ref, pg_ref, yr, ym, rup_ref, mup_ref, wout_ref, gffn_ref, w1_ref, w2_ref, gfin_ref):
    up_r = jnp.dot(yr.astype(BF16), rup_ref[...], preferred_element_type=F32)
    yield
    up_m = jnp.dot(ym.astype(BF16), mup_ref[...], preferred_element_type=F32)
    merged = _sigmoid(pg_ref[:, 0:D_MODEL]) * up_r + _sigmoid(pg_ref[:, D_MODEL:]) * up_m
    yield
    x1 = x_ref[...] + jnp.dot(merged.astype(BF16), wout_ref[...], preferred_element_type=F32)
    hn = _rms(x1, gffn_ref[...]).astype(BF16)
    yield
    x2 = x1
    for j in range(D_FF // FFN_BLOCK):
        cols = slice(j * FFN_BLOCK, (j + 1) * FFN_BLOCK)
        hid = jnp.maximum(jnp.dot(hn, w1_ref[:, cols], preferred_element_type=F32), 0.0)
        x2 = x2 + jnp.dot((hid * hid).astype(BF16), w2_ref[cols, :], preferred_element_type=F32)
        yield
    return _rms(x2, gfin_ref[...])


def _run(gen):
    while True:
        try:
            next(gen)
        except StopIteration as stop:
            return stop.value


def _out_body(x_ref, pg_ref, yr_ref, ym_ref, *rest):
    y_ref = rest[-1]
    y_ref[...] = _run(_ffn_tile(x_ref, pg_ref, yr_ref[...], ym_ref[...], *rest[:-1]))


def _out(x2, pg, yr, ym, p, tm):
    m = x2.shape[0]
    tile = lambda n: pl.BlockSpec((tm, n), lambda i: (i, 0))
    return pl.pallas_call(
        _out_body,
        grid=(m // tm,),
        in_specs=[tile(D_MODEL), tile(GATE_COLS), tile(R_WIDTH), tile(M_WIDTH),
                  _const_spec((R_WIDTH, D_MODEL)), _const_spec((M_WIDTH, D_MODEL)),
                  _const_spec((D_MODEL, D_MODEL)), _const_spec((1, D_MODEL)),
                  _const_spec((D_MODEL, D_FF)), _const_spec((D_FF, D_MODEL)), _const_spec((1, D_MODEL))],
        out_specs=tile(D_MODEL),
        out_shape=jax.ShapeDtypeStruct((m, D_MODEL), F32),
        compiler_params=pltpu.CompilerParams(dimension_semantics=("arbitrary",), vmem_limit_bytes=VMEM_LIMIT),
        name="merge_ffn",
    )(x2, pg, yr, ym, p["r_up"], p["m_up"], p["w_out"], p["g_ffn"], p["w1"], p["w2"], p["g_fin"])


PROJ_TM = 512
RWKV_TT = 512
MLSTM_TT = 512
OUT_TM = 512
STEP_BB = 8


def _head_segments():
    head = jnp.arange(GROUP_W) // R_HEAD_DIM
    return (head[:, None] == head[None, :]).astype(BF16)


def _pad_lanes(v, start):
    out = jnp.zeros((1, LANES), F32)
    return lax.dynamic_update_slice(out, v.reshape(1, -1), (0, start))


def kernel(x_prompt, x_sample, state_rwkv_shift, state_rwkv_wkv, state_mlstm_C, state_mlstm_n, state_mlstm_m, state_mlstm_conv, norm_mix_g, w_in, r_mu, r_w0, r_w2, r_a0, r_a2, r_g2, r_kk, r_ka, r_rk, r_gn_w, r_gn_b, r_up, m_conv_w, m_conv_b, m_wq, m_wk, m_i_b, m_f_b, m_gn_w, m_skip, m_up, gate_b, w_out, norm_ffn_g, ffn_w1, ffn_w2, norm_final_g):
    nbp, seq, _ = x_prompt.shape
    nbs = x_sample.shape[0]
    w = w_in[0]
    wt = jnp.transpose(w)
    w_cat = jnp.concatenate([wt[:R_SHIFT_COLS], wt[R_SHIFT_COLS + M_COLS:],
                             wt[R_SHIFT_COLS:R_SHIFT_COLS + M_COLS]], axis=0).astype(BF16)
    g_mix = norm_mix_g[0].reshape(1, D_MODEL)
    gb = gate_b[0].reshape(1, GATE_COLS)
    rp = dict(mu=r_mu[0].reshape(1, -1), w0=r_w0[0].reshape(1, -1), w2=r_w2[0].astype(BF16),
              a0=r_a0[0].reshape(1, -1), a2=r_a2[0].astype(BF16), g2=r_g2[0].astype(BF16),
              kk=r_kk[0].reshape(1, -1), ka=r_ka[0].reshape(1, -1), rk=r_rk[0].reshape(1, -1),
              gn_w=r_gn_w[0].reshape(1, -1), gn_b=r_gn_b[0].reshape(1, -1), seg=_head_segments())
    mp = dict(conv_w=m_conv_w[0], conv_b=m_conv_b[0].reshape(1, -1), wq=m_wq[0].astype(BF16),
              wk=m_wk[0].astype(BF16), i_b=_pad_lanes(m_i_b[0], 0), f_b=_pad_lanes(m_f_b[0], M_HEADS),
              gn_w=m_gn_w[0].reshape(1, -1), skip=m_skip[0].reshape(1, -1))
    op = dict(r_up=r_up[0].astype(BF16), m_up=m_up[0].astype(BF16), w_out=w_out[0].astype(BF16),
              g_ffn=norm_ffn_g[0].reshape(1, -1), w1=ffn_w1[0].astype(BF16), w2=ffn_w2[0].astype(BF16),
              g_fin=norm_final_g.reshape(1, -1))

    xp = x_prompt.reshape(nbp * seq, D_MODEL)
    pr, pm, pg = _proj(xp, g_mix, w_cat, gb, PROJ_TM)
    y_r, wkv_p = _rwkv_seq(pr, rp, nbp, seq // RWKV_TT, RWKV_TT)
    y_m, c_p, n_p, m_p = _mlstm_seq(pm, mp, nbp, seq // MLSTM_TT, MLSTM_TT)
    y_p = _out(xp, pg, y_r, y_m, op, OUT_TM).reshape(nbp, seq, D_MODEL)
    conv_p = pm.reshape(nbp, seq, M_COLS)[:, seq - (CONV_W - 1):, :M_WIDTH]

    xn_s, xn_last, pr_s, pm_s, pg_s, prev_s = _proj_step(
        x_sample[:, 0], state_rwkv_shift[0], x_prompt[:, seq - 1], g_mix, w_cat, gb)
    yr_s, wkv_t = _rwkv_step(pr_s, prev_s, jnp.transpose(state_rwkv_wkv[0], (1, 2, 3, 0)), rp)
    wkv_s = jnp.transpose(wkv_t, (3, 0, 1, 2))
    conv0 = jnp.transpose(state_mlstm_conv[0], (1, 0, 2))
    ym_s, c_s, n_s, m_s, conv_s = _mlstm_step(pm_s, conv0, state_mlstm_C[0],
                                              state_mlstm_n[0].reshape(nbs, M_WIDTH),
                                              state_mlstm_m[0], mp, STEP_BB)
    y_s = _out(x_sample[:, 0], pg_s, yr_s, ym_s, op, nbs).reshape(nbs, 1, D_MODEL)

    return (y_p, y_s,
            xn_last[None], wkv_p[None], c_p[None], n_p[:, :, 0, :][None], m_p[:, :, 0, 0][None], conv_p[None],
            xn_s[None], wkv_s[None], c_s[None], n_s.reshape(nbs, M_HEADS, M_HEAD_DIM)[None], m_s[None],
            jnp.transpose(conv_s, (1, 0, 2))[None])
```

```python
import functools
import math

import jax
import jax.numpy as jnp
from jax import lax
from jax.experimental import pallas as pl
from jax.experimental.pallas import tpu as pltpu

F32 = jnp.float32
BF16 = jnp.bfloat16

D_MODEL = 1024
R_HEADS = 8
R_HEAD_DIM = 64
R_WIDTH = R_HEADS * R_HEAD_DIM
GROUP_HEADS = 4
R_GROUPS = R_HEADS // GROUP_HEADS
GROUP_W = GROUP_HEADS * R_HEAD_DIM
W_LORA = 64
A_LORA = 64
G_LORA = 128
R_GN_EPS = 64e-5
M_HEADS = 4
M_HEAD_DIM = 128
M_WIDTH = M_HEADS * M_HEAD_DIM
CONV_W = 4
M_GN_EPS = 1e-5
D_FF = 4 * D_MODEL
RMS_EPS = 1e-6
R_SHIFT_COLS = 3 * R_WIDTH + W_LORA + A_LORA + G_LORA
M_COLS = 3 * M_WIDTH + 2 * M_HEADS
GATE_COLS = 2 * D_MODEL

LANES = 128
SUBLANES = 8
CHUNK = 64
RM_COLS = R_SHIFT_COLS + M_COLS
VMEM_LIMIT = 56 * 1024 * 1024


def _mm(a, b, dims="nn"):
    ca = 1 if dims[0] == "n" else 0
    cb = 0 if dims[1] == "n" else 1
    dn = (((ca,), (cb,)), ((), ()))
    return lax.dot_general(a.astype(BF16), b.astype(BF16), dn, preferred_element_type=F32)


def _mm_split(a, b, dims="nn"):
    a_hi = a.astype(BF16)
    b_hi = b.astype(BF16)
    a_lo = a - a_hi.astype(F32)
    b_lo = b - b_hi.astype(F32)
    return _mm(a_hi, b_hi, dims) + (_mm(a_hi, b_lo, dims) + _mm(a_lo, b_hi, dims))


def _mm_mask_lhs(mask, b):
    b_hi = b.astype(BF16)
    return _mm(mask, b_hi) + _mm(mask, b - b_hi.astype(F32))


def _mm_mask_rhs(a, mask):
    a_hi = a.astype(BF16)
    return _mm(a_hi, mask) + _mm(a - a_hi.astype(F32), mask)


def _rms(x, g):
    return x * lax.rsqrt(jnp.mean(x * x, axis=-1, keepdims=True) + RMS_EPS) * g


def _sigmoid(x):
    return 1.0 / (1.0 + jnp.exp(-x))


def _log_sigmoid(x):
    return jnp.minimum(x, 0.0) - jnp.log(1.0 + jnp.exp(-jnp.abs(x)))


def _iota2(shape, axis):
    return lax.broadcasted_iota(jnp.int32, shape, axis)


def _const_spec(shape):
    nd = len(shape)
    return pl.BlockSpec(shape, lambda *_: (0,) * nd, pipeline_mode=pl.Buffered(1))


def _proj_cols(xb, wt_ref, lo, hi):
    return lax.dot_general(xb, wt_ref[lo:hi, :], (((1,), (1,)), ((), ())), preferred_element_type=F32)


def _proj_body(x_ref, g_ref, wrm_ref, wg_ref, gb_ref, pr_ref, pm_ref, pg_ref):
    xb = _rms(x_ref[...], g_ref[...]).astype(BF16)
    pr_ref[...] = _proj_cols(xb, wrm_ref, 0, R_SHIFT_COLS)
    pm_ref[...] = _proj_cols(xb, wrm_ref, R_SHIFT_COLS, RM_COLS)
    pg_ref[...] = _proj_cols(xb, wg_ref, 0, GATE_COLS) + gb_ref[...]


def _proj(x2, g, w_rm, w_g, gate_b, tm):
    m = x2.shape[0]
    return pl.pallas_call(
        _proj_body,
        grid=(m // tm,),
        in_specs=[pl.BlockSpec((tm, D_MODEL), lambda i: (i, 0)),
                  _const_spec((1, D_MODEL)),
                  _const_spec((RM_COLS, D_MODEL)), _const_spec((GATE_COLS, D_MODEL)),
                  _const_spec((1, GATE_COLS))],
        out_specs=[pl.BlockSpec((tm, R_SHIFT_COLS), lambda i: (i, 0)),
                   pl.BlockSpec((tm, M_COLS), lambda i: (i, 0)),
                   pl.BlockSpec((tm, GATE_COLS), lambda i: (i, 0))],
        out_shape=[jax.ShapeDtypeStruct((m, R_SHIFT_COLS), F32),
                   jax.ShapeDtypeStruct((m, M_COLS), F32),
                   jax.ShapeDtypeStruct((m, GATE_COLS), F32)],
        compiler_params=pltpu.CompilerParams(dimension_semantics=("arbitrary",), vmem_limit_bytes=VMEM_LIMIT),
        name="proj",
    )(x2, g, w_rm, w_g, gate_b)


def _proj_step_body(xs_ref, sh_ref, xl_ref, g_ref, wrm_ref, wg_ref, gb_ref,
                    xns_ref, xnl_ref, pr_ref, pm_ref, pg_ref, prev_ref):
    g = g_ref[...]
    xn = _rms(xs_ref[...], g)
    xns_ref[...] = xn
    xnl_ref[...] = _rms(xl_ref[...], g)
    xb = xn.astype(BF16)
    pr_ref[...] = _proj_cols(xb, wrm_ref, 0, R_SHIFT_COLS)
    pm_ref[...] = _proj_cols(xb, wrm_ref, R_SHIFT_COLS, RM_COLS)
    pg_ref[...] = _proj_cols(xb, wg_ref, 0, GATE_COLS) + gb_ref[...]
    prev_ref[...] = _proj_cols(sh_ref[...].astype(BF16), wrm_ref, 0, R_SHIFT_COLS)


def _proj_step(xs, shift0, xlast, g, w_rm, w_g, gate_b):
    nb = xs.shape[0]
    nl = xlast.shape[0]
    return pl.pallas_call(
        _proj_step_body,
        out_shape=[jax.ShapeDtypeStruct((nb, D_MODEL), F32),
                   jax.ShapeDtypeStruct((nl, D_MODEL), F32),
                   jax.ShapeDtypeStruct((nb, R_SHIFT_COLS), F32),
                   jax.ShapeDtypeStruct((nb, M_COLS), F32),
                   jax.ShapeDtypeStruct((nb, GATE_COLS), F32),
                   jax.ShapeDtypeStruct((nb, R_SHIFT_COLS), F32)],
        compiler_params=pltpu.CompilerParams(vmem_limit_bytes=VMEM_LIMIT),
        name="proj_step",
    )(xs, shift0, xlast, g, w_rm, w_g, gate_b)


LOG_DECAY_SCALE = -math.exp(-0.5)


def _rwkv_mix(pr, prev, mu, w0, w2, a0, a2, g2, r_kk, r_ka):
    mixed = pr + (prev - pr) * mu
    r = mixed[:, 0:R_WIDTH]
    k = mixed[:, R_WIDTH:2 * R_WIDTH]
    v = mixed[:, 2 * R_WIDTH:3 * R_WIDTH]
    xw = mixed[:, 3 * R_WIDTH:3 * R_WIDTH + W_LORA]
    xa = mixed[:, 3 * R_WIDTH + W_LORA:3 * R_WIDTH + W_LORA + A_LORA]
    xg = mixed[:, 3 * R_WIDTH + W_LORA + A_LORA:]
    w = w0 + _mm(jnp.tanh(xw), w2)
    logw = LOG_DECAY_SCALE * _sigmoid(w)
    a = _sigmoid(a0 + _mm(xa, a2))
    g = _mm(_sigmoid(xg), g2)
    kk0 = k * r_kk
    k2 = k * (1.0 + (a - 1.0) * r_ka)
    return r, k2, v, logw, a, g, kk0


def _rwkv_tile(pr_ref, mu_ref, w0_ref, w2_ref, a0_ref, a2_ref, g2_ref, kk_ref, ka_ref, rk_ref,
               gnw_ref, gnb_ref, seg_ref, s_scr, carry_scr, tt):
    pr = pr_ref[...]
    row = _iota2(pr.shape, 0)
    prev = jnp.where(row == 0, carry_scr[...], pltpu.roll(pr, 1, axis=0))
    carry_scr[...] = pr[tt - 1:tt, :]

    r, k2, v, logw, a, g, kk0 = _rwkv_mix(pr, prev, mu_ref[...], w0_ref[...], w2_ref[...], a0_ref[...],
                                          a2_ref[...], g2_ref[...], kk_ref[...], ka_ref[...])
    yield
    ri = _iota2((CHUNK, GROUP_W), 0)
    ci = _iota2((CHUNK, GROUP_W), 1)
    src = ci % R_HEAD_DIM
    lane_head = ci // R_HEAD_DIM
    low_incl = src <= ri
    low_strict = src < ri
    eye = (src == ri).astype(F32)
    rr = _iota2((GROUP_W, GROUP_W), 0)
    cc = _iota2((GROUP_W, GROUP_W), 1)
    same_head = (rr // R_HEAD_DIM) == (cc // R_HEAD_DIM)
    eye_bd = (rr == cc).astype(F32)
    seg = seg_ref[...]

    def seg_sum(x, mm=_mm):
        return jnp.concatenate([mm(x[:, q * GROUP_W:(q + 1) * GROUP_W], seg) for q in range(R_GROUPS)], axis=1)
    tril = (_iota2((CHUNK, CHUNK), 1) <= _iota2((CHUNK, CHUNK), 0)).astype(F32)

    nc = tt // CHUNK
    cum = jnp.concatenate([_mm_mask_lhs(tril, logw[c * CHUNK:(c + 1) * CHUNK]) for c in range(nc)], axis=0)
    yield
    cum_last = jnp.concatenate(
        [jnp.broadcast_to(cum[(c + 1) * CHUNK - 1:(c + 1) * CHUNK, :], (CHUNK, R_WIDTH)) for c in range(nc)], axis=0)
    d_inv = jnp.exp(-cum)
    d_end = jnp.exp(cum_last - cum)
    d_last = jnp.exp(cum_last)
    kk = kk0 / jnp.maximum(jnp.sqrt(seg_sum(kk0 * kk0)), 1e-12)
    yield
    kka = kk * a
    at = kk * jnp.exp(cum - logw)
    rt = r * jnp.exp(cum)
    bt = kka * d_inv
    kt = k2 * d_inv
    bh = kka * d_end
    kh = k2 * d_end

    items = [(c, p) for c in range(nc) for p in range(R_GROUPS)]
    blk = lambda x, c, p: x[c * CHUNK:(c + 1) * CHUNK, p * GROUP_W:(p + 1) * GROUP_W]
    each = lambda f, *ls: [f(*xs) for xs in zip(*ls)]
    cat0 = lambda *xs: jnp.concatenate(xs, axis=0)
    cat1 = lambda *xs: jnp.concatenate(xs, axis=1)

    def bd(x):
        return cat0(*[jnp.where(lane_head == j, x, 0.0) for j in range(GROUP_HEADS)])

    a_ = [blk(at, c, p) for c, p in items]
    b_ = [blk(bt, c, p) for c, p in items]
    bh_ = [blk(bh, c, p) for c, p in items]
    r_ = [blk(rt, c, p) for c, p in items]
    k_ = [blk(kt, c, p) for c, p in items]
    kh_ = [blk(kh, c, p) for c, p in items]
    v_ = [blk(v, c, p) for c, p in items]
    vbd = [bd(x) for x in v_]
    pq = each(lambda a1, r1, b1, k1: _mm(cat0(a1, r1), cat0(bd(b1), bd(k1)), "nt"), a_, r_, b_, k_)
    yield
    t_ab = [jnp.where(low_strict, x[:CHUNK, :GROUP_W], 0.0) for x in pq]
    t_ak = [jnp.where(low_strict, x[:CHUNK, GROUP_W:], 0.0) for x in pq]
    m_rb = [jnp.where(low_incl, x[CHUNK:, :GROUP_W], 0.0) for x in pq]
    m_rk = [jnp.where(low_incl, x[CHUNK:, GROUP_W:], 0.0) for x in pq]
    q_pow = each(lambda t: _mm(t, bd(t)), t_ab)
    yield
    x_inv = [eye - t for t in t_ab]
    for _ in range(4):
        z = each(lambda x, q: _mm(cat0(x, q), bd(q)), x_inv, q_pow)
        x_inv = [x + zz[:CHUNK] for x, zz in zip(x_inv, z)]
        q_pow = [zz[CHUNK:] for zz in z]
        yield
    x_inv = each(lambda x, q: x + _mm(x, bd(q)), x_inv, q_pow)
    yield
    tv = each(_mm, t_ak, vbd)
    yield
    xw = each(lambda x, a1, t: _mm(x, cat1(bd(a1), bd(t))), x_inv, a_, tv)
    yield
    mw = each(lambda m1, x: _mm(m1, cat1(bd(x[:, :GROUP_W]), bd(x[:, GROUP_W:]))), m_rb, xw)
    yield
    q_eff = [r1 - m1[:, :GROUP_W] for r1, m1 in zip(r_, mw)]
    y_loc = each(lambda m1, vb, m2: _mm(m1, vb) - m2[:, GROUP_W:], m_rk, vbd, mw)
    yield
    g_bd = [eye_bd * blk(d_last, c, p)[0:1] - jnp.where(same_head, _mm(x[:, :GROUP_W], b1, "tn"), 0.0)
            for (c, p), x, b1 in zip(items, xw, bh_)]
    yield
    h_full = each(lambda vv, x, k1, b1: _mm(cat0(vv, x[:, GROUP_W:]), cat0(k1, -b1), "tn"), v_, xw, kh_, bh_)
    yield
    h_pair = []
    for x in h_full:
        acc = x[:CHUNK]
        for j in range(1, GROUP_HEADS):
            acc = jnp.where(lane_head == j, x[j * CHUNK:(j + 1) * CHUNK], acc)
        h_pair.append(acc)

    state = [s_scr[p] for p in range(R_GROUPS)]
    ys = []
    for i, (c, p) in enumerate(items):
        ys.append(_mm(q_eff[i], bd(state[p]), "nt") + y_loc[i])
        state[p] = _mm(state[p], g_bd[i]) + h_pair[i]
        if p == R_GROUPS - 1:
            yield
    for p in range(R_GROUPS):
        s_scr[p] = state[p]

    y_all = cat0(*[cat1(*ys[c * R_GROUPS:(c + 1) * R_GROUPS]) for c in range(nc)])
    yc = y_all - seg_sum(y_all) * (1.0 / R_HEAD_DIM)
    yield
    var = seg_sum(yc * yc) * (1.0 / R_HEAD_DIM)
    yield
    bonus = seg_sum(r * k2 * rk_ref[...], _mm_mask_rhs)
    return (yc * lax.rsqrt(var + R_GN_EPS) * gnw_ref[...] + gnb_ref[...] + bonus * v) * g


N_RWKV_PARAMS = 12


SEQ_WAYS = 2
SEQ_LAG = 4


def _staggered(gens, lag):
    out = [None] * len(gens)
    live = set(range(len(gens)))
    step = 0
    while live:
        for j in sorted(live):
            if step >= j * lag:
                try:
                    next(gens[j])
                except StopIteration as stop:
                    out[j] = stop.value
                    live.discard(j)
        step += 1
    return out


def _rwkv_seq_body(*refs, tt):
    pr_ref, params = refs[0], refs[1:1 + N_RWKV_PARAMS]
    y_ref, s_out_ref, s_scr, carry_scr = refs[1 + N_RWKV_PARAMS:]
    t_idx = pl.program_id(1)

    @pl.when(t_idx == 0)
    def _():
        s_scr[...] = jnp.zeros_like(s_scr)
        carry_scr[...] = jnp.zeros_like(carry_scr)

    ys = _staggered([_rwkv_tile(pr_ref.at[w], *params, s_scr.at[w], carry_scr.at[w], tt)
                     for w in range(SEQ_WAYS)], SEQ_LAG)
    for w in range(SEQ_WAYS):
        y_ref[w] = ys[w]

    @pl.when(t_idx == pl.num_programs(1) - 1)
    def _():
        for w in range(SEQ_WAYS):
            for p in range(R_GROUPS):
                for j in range(GROUP_HEADS):
                    s_out_ref[w, 0, GROUP_HEADS * p + j] = s_scr[w, p][:, j * R_HEAD_DIM:(j + 1) * R_HEAD_DIM]


def _rwkv_seq(pr, p, nb, nt, tt):
    m = pr.shape[1]
    row = lambda n: _const_spec((1, n))
    return pl.pallas_call(
        functools.partial(_rwkv_seq_body, tt=tt),
        grid=(nb // SEQ_WAYS, nt),
        in_specs=[pl.BlockSpec((SEQ_WAYS, tt, R_SHIFT_COLS), lambda b, t: (0, b * nt + t, 0)),
                  row(R_SHIFT_COLS), row(R_WIDTH), _const_spec((W_LORA, R_WIDTH)), row(R_WIDTH),
                  _const_spec((A_LORA, R_WIDTH)), _const_spec((G_LORA, R_WIDTH)), row(R_WIDTH), row(R_WIDTH),
                  row(R_WIDTH), row(R_WIDTH), row(R_WIDTH), _const_spec((GROUP_W, GROUP_W))],
        out_specs=[pl.BlockSpec((SEQ_WAYS, tt, R_WIDTH), lambda b, t: (0, b * nt + t, 0)),
                   pl.BlockSpec((SEQ_WAYS, 1, R_HEADS, R_HEAD_DIM, R_HEAD_DIM), lambda b, t: (0, b, 0, 0, 0))],
        out_shape=[jax.ShapeDtypeStruct((SEQ_WAYS, m, R_WIDTH), F32),
                   jax.ShapeDtypeStruct((SEQ_WAYS, nb // SEQ_WAYS, R_HEADS, R_HEAD_DIM, R_HEAD_DIM), F32)],
        scratch_shapes=[pltpu.VMEM((SEQ_WAYS, R_GROUPS, R_HEAD_DIM, GROUP_W), F32),
                        pltpu.VMEM((SEQ_WAYS, 1, R_SHIFT_COLS), F32)],
        compiler_params=pltpu.CompilerParams(dimension_semantics=("arbitrary", "arbitrary"),
                                             vmem_limit_bytes=VMEM_LIMIT),
        name="rwkv_seq",
    )(pr, p["mu"], p["w0"], p["w2"], p["a0"], p["a2"], p["g2"], p["kk"], p["ka"], p["rk"],
      p["gn_w"], p["gn_b"], p["seg"])


def _rwkv_step_body(pr_ref, prev_ref, s_ref, mu_ref, w0_ref, w2_ref, a0_ref, a2_ref, g2_ref, kk_ref, ka_ref,
                    rk_ref, gnw_ref, gnb_ref, seg_ref, y_ref, s_out_ref,
                    kk_scr, kka_scr, dec_scr, k_scr, v_scr, r_scr, yt_scr, g_scr, vrow_scr, bonus_scr):
    h = pl.program_id(0)
    seg = seg_ref[...]

    def seg_sum(x, mm=_mm):
        return jnp.concatenate([mm(x[:, q * GROUP_W:(q + 1) * GROUP_W], seg) for q in range(R_GROUPS)], axis=1)

    @pl.when(h == 0)
    def _():
        r, k2, v, logw, a, g, kk0 = _rwkv_mix(pr_ref[...], prev_ref[...], mu_ref[...], w0_ref[...], w2_ref[...],
                                              a0_ref[...], a2_ref[...], g2_ref[...], kk_ref[...], ka_ref[...])
        kk = kk0 / jnp.maximum(jnp.sqrt(seg_sum(kk0 * kk0)), 1e-12)
        kk_scr[...] = kk.T
        kka_scr[...] = (kk * a).T
        dec_scr[...] = jnp.exp(logw).T
        k_scr[...] = k2.T
        v_scr[...] = v.T
        r_scr[...] = r.T
        g_scr[...] = g
        vrow_scr[...] = v
        bonus_scr[...] = seg_sum(r * k2 * rk_ref[...], _mm_mask_rhs)

    base = pl.multiple_of(h * R_HEAD_DIM, R_HEAD_DIM)
    hs = pl.ds(base, R_HEAD_DIM)
    kk_t, kka_t, dec_t, k_t, r_t = kk_scr[hs, :], kka_scr[hs, :], dec_scr[hs, :], k_scr[hs, :], r_scr[hs, :]
    for i in range(R_HEAD_DIM):
        s0 = s_ref[0, i]
        s_kk = jnp.sum(s0 * kk_t, axis=0, keepdims=True)
        s1 = s0 * dec_t - s_kk * kka_t + v_scr[pl.ds(base + i, 1), :] * k_t
        s_out_ref[0, i] = s1
        yt_scr[pl.ds(base + i, 1), :] = jnp.sum(s1 * r_t, axis=0, keepdims=True)

    @pl.when(h == pl.num_programs(0) - 1)
    def _():
        y = yt_scr[...].T
        yc = y - seg_sum(y) * (1.0 / R_HEAD_DIM)
        var = seg_sum(yc * yc) * (1.0 / R_HEAD_DIM)
        y_ref[...] = (yc * lax.rsqrt(var + R_GN_EPS) * gnw_ref[...] + gnb_ref[...]
                      + bonus_scr[...] * vrow_scr[...]) * g_scr[...]


def _rwkv_step(pr, prev, s0_t, p):
    nb = pr.shape[0]
    row = lambda n: _const_spec((1, n))
    state_spec = pl.BlockSpec((1, R_HEAD_DIM, R_HEAD_DIM, nb), lambda h: (h, 0, 0, 0))
    tr = pltpu.VMEM((R_WIDTH, nb), F32)
    rw = pltpu.VMEM((nb, R_WIDTH), F32)
    return pl.pallas_call(
        _rwkv_step_body,
        grid=(R_HEADS,),
        in_specs=[_const_spec((nb, R_SHIFT_COLS)), _const_spec((nb, R_SHIFT_COLS)), state_spec,
                  row(R_SHIFT_COLS), row(R_WIDTH), _const_spec((W_LORA, R_WIDTH)), row(R_WIDTH),
                  _const_spec((A_LORA, R_WIDTH)), _const_spec((G_LORA, R_WIDTH)), row(R_WIDTH), row(R_WIDTH),
                  row(R_WIDTH), row(R_WIDTH), row(R_WIDTH), _const_spec((GROUP_W, GROUP_W))],
        out_specs=[pl.BlockSpec((nb, R_WIDTH), lambda h: (0, 0)), state_spec],
        out_shape=[jax.ShapeDtypeStruct((nb, R_WIDTH), F32),
                   jax.ShapeDtypeStruct(s0_t.shape, F32)],
        scratch_shapes=[tr, tr, tr, tr, tr, tr, tr, rw, rw, rw],
        compiler_params=pltpu.CompilerParams(dimension_semantics=("arbitrary",), vmem_limit_bytes=VMEM_LIMIT),
        name="rwkv_step",
    )(pr, prev, s0_t, p["mu"], p["w0"], p["w2"], p["a0"], p["a2"], p["g2"], p["kk"], p["ka"], p["rk"],
      p["gn_w"], p["gn_b"], p["seg"])


GATE_LANE0 = 3 * M_WIDTH


def _gate_lanes(pm_ref):
    gates = pm_ref[:, GATE_LANE0:GATE_LANE0 + 2 * M_HEADS]
    return jnp.concatenate([gates, jnp.zeros((gates.shape[0], LANES - 2 * M_HEADS), F32)], axis=1)


def _head_norm_rows(x, eps):
    xc = x - jnp.mean(x, axis=1, keepdims=True)
    var = jnp.mean(xc * xc, axis=1, keepdims=True)
    return xc * lax.rsqrt(var + eps)


def _cummax_rows(x):
    n = x.shape[0]
    row = _iota2(x.shape, 0)
    d = 1
    while d < n:
        if d < SUBLANES:
            shifted = jnp.where(row < d, -jnp.inf, pltpu.roll(x, d, axis=0))
        else:
            shifted = jnp.concatenate([jnp.full((d, x.shape[1]), -jnp.inf, x.dtype), x[:n - d]], axis=0)
        x = jnp.maximum(x, shifted)
        d *= 2
    return x


def _mlstm_seq_body(pm_ref, cw_ref, cb_ref, wq_ref, wk_ref, ib_ref, fb_ref, gnw_ref, skip_ref,
                    y_ref, c_out_ref, n_out_ref, m_out_ref, c_scr, n_scr, m_scr, carry_scr, *, tt):
    t_idx = pl.program_id(1)

    @pl.when(t_idx == 0)
    def _():
        c_scr[...] = jnp.zeros_like(c_scr)
        n_scr[...] = jnp.zeros_like(n_scr)
        m_scr[...] = jnp.zeros_like(m_scr)
        carry_scr[...] = jnp.zeros_like(carry_scr)

    xm = pm_ref[:, 0:M_WIDTH]
    carry = carry_scr[...]
    carry_scr[...] = xm[tt - SUBLANES:tt, :]
    row8 = _iota2((SUBLANES, M_WIDTH), 0)
    cw = cw_ref[...]
    xc = cb_ref[...] + xm * cw[CONV_W - 1:CONV_W, :]
    for s in range(1, CONV_W):
        rolled = pltpu.roll(xm, s, axis=0)
        top = jnp.where(row8 < s, pltpu.roll(carry, s, axis=0), rolled[0:SUBLANES])
        shifted = jnp.concatenate([top, rolled[SUBLANES:]], axis=0)
        xc = xc + shifted * cw[CONV_W - 1 - s:CONV_W - s, :]
    xc = xc * _sigmoid(xc)

    gt = _gate_lanes(pm_ref)
    li_all = gt + ib_ref[...]
    lf_all = _log_sigmoid(gt + fb_ref[...])

    ri = _iota2((CHUNK, CHUNK), 0)
    ci = _iota2((CHUNK, CHUNK), 1)
    causal = ci <= ri
    tril = causal.astype(F32)
    gnw = gnw_ref[...]
    skip = skip_ref[...]

    ones_cl = jnp.ones((CHUNK, LANES), BF16)
    ones_ll = jnp.ones((LANES, LANES), BF16)
    nc = tt // CHUNK
    rows = lambda c: slice(c * CHUNK, (c + 1) * CHUNK)
    lanes = lambda h: slice(h * M_HEAD_DIM, (h + 1) * M_HEAD_DIM)

    bc_all = jnp.concatenate([_mm_mask_lhs(tril, lf_all[rows(c)]) for c in range(nc)], axis=0)
    bc_al = pltpu.roll(bc_all, LANES - M_HEADS, axis=1)
    u_all = li_all - bc_al
    cm_all = jnp.concatenate([_cummax_rows(u_all[rows(c)]) for c in range(nc)], axis=0)
    u_t = u_all.T

    items = [(c, h) for c in range(nc) for h in range(M_HEADS)]
    rep = lambda x, c, h: jnp.broadcast_to(x[rows(c), h:h + 1], (CHUNK, LANES))
    xc_b = [xc[rows(c), lanes(h)] for c, h in items]
    q = [_mm(x, wq_ref[h]) * (M_HEAD_DIM ** -0.5) for (c, h), x in zip(items, xc_b)]
    k = [_mm(x, wk_ref[h]) for (c, h), x in zip(items, xc_b)]
    vv = [pm_ref[rows(c), M_WIDTH + h * M_HEAD_DIM:M_WIDTH + (h + 1) * M_HEAD_DIM] for c, h in items]
    cm_r = [rep(cm_all, c, h) for c, h in items]
    bc_r = [rep(bc_al, c, h) for c, h in items]
    u_r = [rep(u_all, c, h) for c, h in items]

    m_run = [m_scr[h][0:1, 0:1] for h in range(M_HEADS)]
    m_prev, m_cap, m_end = [], [], []
    for i, (c, h) in enumerate(items):
        m_prev.append(m_run[h])
        m_cap.append(jnp.maximum(m_run[h], cm_r[i]))
        m_end.append(m_cap[i][CHUNK - 1:CHUNK, 0:1])
        m_run[h] = bc_r[i][CHUNK - 1:CHUNK, 0:1] + m_end[i]

    w_st = [jnp.exp(mp - mc) for mp, mc in zip(m_prev, m_cap)]
    floor = [jnp.exp(-(b + mc)) for b, mc in zip(bc_r, m_cap)]
    w_in = [jnp.exp(jnp.where(causal, u_t[h:h + 1, rows(c)] - mc[:, :CHUNK], -jnp.inf))
            for (c, h), mc in zip(items, m_cap)]
    s = [_mm(qq, kk, "nt") * w for qq, kk, w in zip(q, k, w_in)]
    sv = [_mm(ss, v1) for ss, v1 in zip(s, vv)]
    s_sum = [_mm(ss, ones_cl) for ss in s]
    we = [jnp.exp(u - me) for u, me in zip(u_r, m_end)]
    ge = [jnp.exp(mp - me) for mp, me in zip(m_prev, m_end)]
    kv = [_mm(kk, w1 * v1, "tn") for kk, w1, v1 in zip(k, we, vv)]
    k_sum = [jnp.sum(w1 * kk, axis=0, keepdims=True) for w1, kk in zip(we, k)]

    c_run = [c_scr[h] for h in range(M_HEADS)]
    n_run = [n_scr[h][0:1, :] for h in range(M_HEADS)]
    num, den = [], []
    for i, (c, h) in enumerate(items):
        num.append(w_st[i] * _mm(q[i], c_run[h]) + sv[i])
        den.append(w_st[i] * _mm(q[i] * n_run[h], ones_ll) + s_sum[i])
        c_run[h] = ge[i] * c_run[h] + kv[i]
        n_run[h] = ge[i] * n_run[h] + k_sum[i]
    for h in range(M_HEADS):
        c_scr[h] = c_run[h]
        n_scr[h] = jnp.broadcast_to(n_run[h], (SUBLANES, M_HEAD_DIM))
        m_scr[h] = jnp.broadcast_to(m_run[h], (SUBLANES, LANES))

    for i, (c, h) in enumerate(items):
        hh = num[i] / jnp.maximum(jnp.abs(den[i]), floor[i])
        hc = hh - _mm(hh, ones_ll) * (1.0 / M_HEAD_DIM)
        var = _mm(hc * hc, ones_ll) * (1.0 / M_HEAD_DIM)
        hn = hc * lax.rsqrt(var + M_GN_EPS) * gnw[:, lanes(h)] + skip[:, lanes(h)] * xc_b[i]
        o = pm_ref[rows(c), 2 * M_WIDTH + h * M_HEAD_DIM:2 * M_WIDTH + (h + 1) * M_HEAD_DIM]
        y_ref[rows(c), lanes(h)] = _sigmoid(o) * hn

    @pl.when(t_idx == pl.num_programs(1) - 1)
    def _():
        c_out_ref[0] = c_scr[...]
        n_out_ref[0] = n_scr[...]
        m_out_ref[0] = m_scr[...]


def _mlstm_seq(pm, p, nb, nt, tt):
    m = pm.shape[0]
    row = lambda n: _const_spec((1, n))
    return pl.pallas_call(
        functools.partial(_mlstm_seq_body, tt=tt),
        grid=(nb, nt),
        in_specs=[pl.BlockSpec((tt, M_COLS), lambda b, t: (b * nt + t, 0)),
                  _const_spec((CONV_W, M_WIDTH)), row(M_WIDTH),
                  _const_spec((M_HEADS, M_HEAD_DIM, M_HEAD_DIM)), _const_spec((M_HEADS, M_HEAD_DIM, M_HEAD_DIM)),
                  row(LANES), row(LANES), row(M_WIDTH), row(M_WIDTH)],
        out_specs=[pl.BlockSpec((tt, M_WIDTH), lambda b, t: (b * nt + t, 0)),
                   pl.BlockSpec((1, M_HEADS, M_HEAD_DIM, M_HEAD_DIM), lambda b, t: (b, 0, 0, 0)),
                   pl.BlockSpec((1, M_HEADS, SUBLANES, M_HEAD_DIM), lambda b, t: (b, 0, 0, 0)),
                   pl.BlockSpec((1, M_HEADS, SUBLANES, LANES), lambda b, t: (b, 0, 0, 0))],
        out_shape=[jax.ShapeDtypeStruct((m, M_WIDTH), F32),
                   jax.ShapeDtypeStruct((nb, M_HEADS, M_HEAD_DIM, M_HEAD_DIM), F32),
                   jax.ShapeDtypeStruct((nb, M_HEADS, SUBLANES, M_HEAD_DIM), F32),
                   jax.ShapeDtypeStruct((nb, M_HEADS, SUBLANES, LANES), F32)],
        scratch_shapes=[pltpu.VMEM((M_HEADS, M_HEAD_DIM, M_HEAD_DIM), F32),
                        pltpu.VMEM((M_HEADS, SUBLANES, M_HEAD_DIM), F32),
                        pltpu.VMEM((M_HEADS, SUBLANES, LANES), F32),
                        pltpu.VMEM((SUBLANES, M_WIDTH), F32)],
        compiler_params=pltpu.CompilerParams(dimension_semantics=("arbitrary", "arbitrary"),
                                             vmem_limit_bytes=VMEM_LIMIT),
        name="mlstm_seq",
    )(pm, p["conv_w"], p["conv_b"], p["wq"], p["wk"], p["i_b"], p["f_b"], p["gn_w"], p["skip"])


def _mlstm_step_body(pm_ref, conv_ref, c_ref, n_ref, m_ref, cw_ref, cb_ref, wq_ref, wk_ref, ib_ref, fb_ref,
                     gnw_ref, skip_ref, y_ref, c_out_ref, n_out_ref, m_out_ref, conv_out_ref, hv_scr, *, bb):
    xm = pm_ref[:, 0:M_WIDTH]
    cw = cw_ref[...]
    xc = cb_ref[...] + xm * cw[CONV_W - 1:CONV_W, :]
    for j in range(CONV_W - 1):
        xc = xc + conv_ref[j] * cw[j:j + 1, :]
    xc = xc * _sigmoid(xc)
    for j in range(CONV_W - 2):
        conv_out_ref[j] = conv_ref[j + 1]
    conv_out_ref[CONV_W - 2] = xm

    gt = _gate_lanes(pm_ref)
    li = (gt + ib_ref[...])[:, 0:M_HEADS]
    lf = _log_sigmoid(gt + fb_ref[...])[:, M_HEADS:2 * M_HEADS]
    m_prev = m_ref[...]
    g_st = lf + m_prev
    m_t = jnp.maximum(g_st, (lf - lf) + li)
    w_in = jnp.exp(((lf - lf) + li) - m_t)
    w_st = jnp.exp(g_st - m_t)
    floor = jnp.exp(-m_t)
    m_out_ref[...] = m_t

    ri = _iota2((M_HEAD_DIM, M_HEAD_DIM), 0)
    ci = _iota2((M_HEAD_DIM, M_HEAD_DIM), 1)
    eye = ri == ci
    heads = [slice(h * M_HEAD_DIM, (h + 1) * M_HEAD_DIM) for h in range(M_HEADS)]
    q = [_mm(xc[:, hs], wq_ref[h]) * (M_HEAD_DIM ** -0.5) for h, hs in enumerate(heads)]
    k = [_mm(xc[:, hs], wk_ref[h]) for h, hs in enumerate(heads)]
    vv = [pm_ref[:, M_WIDTH + h * M_HEAD_DIM:M_WIDTH + (h + 1) * M_HEAD_DIM] for h in range(M_HEADS)]
    s, inv = [], []
    for h, hs in enumerate(heads):
        n0 = n_ref[:, hs]
        qk = jnp.sum(q[h] * k[h], axis=1, keepdims=True)
        qn = jnp.sum(q[h] * n0, axis=1, keepdims=True)
        s.append(qk * w_in[:, h:h + 1])
        den = w_st[:, h:h + 1] * qn + s[h]
        inv.append(1.0 / jnp.maximum(jnp.abs(den), floor[:, h:h + 1]))
        n_out_ref[:, hs] = w_st[:, h:h + 1] * n0 + w_in[:, h:h + 1] * k[h]
    items = [(h, b) for h in range(M_HEADS) for b in range(bb)]
    c0 = [c_ref[b, h] for h, b in items]
    q_col = [jnp.sum(jnp.where(eye, q[h][b:b + 1], 0.0), axis=1, keepdims=True) for h, b in items]
    k_col = [jnp.sum(jnp.where(eye, k[h][b:b + 1], 0.0), axis=1, keepdims=True) for h, b in items]
    qc = [jnp.sum(c * qq, axis=0, keepdims=True) for c, qq in zip(c0, q_col)]
    for i, (h, b) in enumerate(items):
        c_out_ref[b, h] = (w_st[b:b + 1, h:h + 1] * c0[i]
                           + (w_in[b:b + 1, h:h + 1] * k_col[i]) * vv[h][b:b + 1])
    for h, hs in enumerate(heads):
        qc_h = jnp.concatenate(qc[h * bb:(h + 1) * bb], axis=0)
        hv_scr[:, hs] = (w_st[:, h:h + 1] * qc_h + s[h] * vv[h]) * inv[h]
    hv = hv_scr[...]
    gnw = gnw_ref[...]
    skip = skip_ref[...]
    outs = []
    for h in range(M_HEADS):
        hs = slice(h * M_HEAD_DIM, (h + 1) * M_HEAD_DIM)
        outs.append(_head_norm_rows(hv[:, hs], M_GN_EPS) * gnw[:, hs] + skip[:, hs] * xc[:, hs])
    o = pm_ref[:, 2 * M_WIDTH:3 * M_WIDTH]
    y_ref[...] = _sigmoid(o) * jnp.concatenate(outs, axis=1)


def _mlstm_step(pm, conv0, c0, n0, m0, p, bb):
    nb = pm.shape[0]
    row = lambda n: _const_spec((1, n))
    c_spec = pl.BlockSpec((bb, M_HEADS, M_HEAD_DIM, M_HEAD_DIM), lambda i: (i, 0, 0, 0))
    n_spec = pl.BlockSpec((bb, M_WIDTH), lambda i: (i, 0))
    m_spec = pl.BlockSpec((bb, M_HEADS), lambda i: (i, 0))
    conv_spec = pl.BlockSpec((CONV_W - 1, bb, M_WIDTH), lambda i: (0, i, 0))
    return pl.pallas_call(
        functools.partial(_mlstm_step_body, bb=bb),
        grid=(nb // bb,),
        in_specs=[pl.BlockSpec((bb, M_COLS), lambda i: (i, 0)), conv_spec, c_spec, n_spec, m_spec,
                  _const_spec((CONV_W, M_WIDTH)), row(M_WIDTH),
                  _const_spec((M_HEADS, M_HEAD_DIM, M_HEAD_DIM)), _const_spec((M_HEADS, M_HEAD_DIM, M_HEAD_DIM)),
                  row(LANES), row(LANES), row(M_WIDTH), row(M_WIDTH)],
        out_specs=[pl.BlockSpec((bb, M_WIDTH), lambda i: (i, 0)), c_spec, n_spec, m_spec, conv_spec],
        out_shape=[jax.ShapeDtypeStruct((nb, M_WIDTH), F32),
                   jax.ShapeDtypeStruct(c0.shape, F32),
                   jax.ShapeDtypeStruct(n0.shape, F32),
                   jax.ShapeDtypeStruct(m0.shape, F32),
                   jax.ShapeDtypeStruct(conv0.shape, F32)],
        scratch_shapes=[pltpu.VMEM((bb, M_WIDTH), F32)],
        compiler_params=pltpu.CompilerParams(dimension_semantics=("arbitrary",), vmem_limit_bytes=VMEM_LIMIT),
        name="mlstm_step",
    )(pm, conv0, c0, n0, m0, p["conv_w"], p["conv_b"], p["wq"], p["wk"], p["i_b"], p["f_b"],
      p["gn_w"], p["skip"])


FFN_BLOCK = 1024


def _ffn_tile(x_ref, pg_ref, yr, ym, rup_ref, mup_ref, wout_ref, gffn_ref, w1_ref, w2_ref, gfin_ref):
    up_r = jnp.dot(yr.astype(BF16), rup_ref[...], preferred_element_type=F32)
    yield
    up_m = jnp.dot(ym.astype(BF16), mup_ref[...], preferred_element_type=F32)
    merged = _sigmoid(pg_ref[:, 0:D_MODEL]) * up_r + _sigmoid(pg_ref[:, D_MODEL:]) * up_m
    yield
    x1 = x_ref[...] + jnp.dot(merged.astype(BF16), wout_ref[...], preferred_element_type=F32)
    hn = _rms(x1, gffn_ref[...]).astype(BF16)
    yield
    x2 = x1
    for j in range(D_FF // FFN_BLOCK):
        cols = slice(j * FFN_BLOCK, (j + 1) * FFN_BLOCK)
        hid = jnp.maximum(jnp.dot(hn, w1_ref[:, cols], preferred_element_type=F32), 0.0)
        x2 = x2 + jnp.dot((hid * hid).astype(BF16), w2_ref[cols, :], preferred_element_type=F32)
        yield
    return _rms(x2, gfin_ref[...])


def _run(gen):
    while True:
        try:
            next(gen)
        except StopIteration as stop:
            return stop.value


def _out_body(x_ref, pg_ref, yr_ref, ym_ref, *rest):
    y_ref = rest[-1]
    y_ref[...] = _run(_ffn_tile(x_ref, pg_ref, yr_ref[...], ym_ref[...], *rest[:-1]))


def _out(x2, pg, yr, ym, p, tm):
    m = x2.shape[0]
    tile = lambda n: pl.BlockSpec((tm, n), lambda i: (i, 0))
    return pl.pallas_call(
        _out_body,
        grid=(m // tm,),
        in_specs=[tile(D_MODEL), tile(GATE_COLS), tile(R_WIDTH), tile(M_WIDTH),
                  _const_spec((R_WIDTH, D_MODEL)), _const_spec((M_WIDTH, D_MODEL)),
                  _const_spec((D_MODEL, D_MODEL)), _const_spec((1, D_MODEL)),
                  _const_spec((D_MODEL, D_FF)), _const_spec((D_FF, D_MODEL)), _const_spec((1, D_MODEL))],
        out_specs=tile(D_MODEL),
        out_shape=jax.ShapeDtypeStruct((m, D_MODEL), F32),
        compiler_params=pltpu.CompilerParams(dimension_semantics=("arbitrary",), vmem_limit_bytes=VMEM_LIMIT),
        name="merge_ffn",
    )(x2, pg, yr, ym, p["r_up"], p["m_up"], p["w_out"], p["g_ffn"], p["w1"], p["w2"], p["g_fin"])


PROJ_TM = 512
RWKV_TT = 512
MLSTM_TT = 512
OUT_TM = 512
STEP_BB = 8


def _head_segments():
    head = jnp.arange(GROUP_W) // R_HEAD_DIM
    return (head[:, None] == head[None, :]).astype(BF16)


def _pad_lanes(v, start):
    out = jnp.zeros((1, LANES), F32)
    return lax.dynamic_update_slice(out, v.reshape(1, -1), (0, start))


def kernel(x_prompt, x_sample, state_rwkv_shift, state_rwkv_wkv, state_mlstm_C, state_mlstm_n, state_mlstm_m, state_mlstm_conv, norm_mix_g, w_in, r_mu, r_w0, r_w2, r_a0, r_a2, r_g2, r_kk, r_ka, r_rk, r_gn_w, r_gn_b, r_up, m_conv_w, m_conv_b, m_wq, m_wk, m_i_b, m_f_b, m_gn_w, m_skip, m_up, gate_b, w_out, norm_ffn_g, ffn_w1, ffn_w2, norm_final_g):
    nbp, seq, _ = x_prompt.shape
    nbs = x_sample.shape[0]
    w = w_in[0]
    wt = jnp.transpose(w)
    w_rm = wt[:RM_COLS].astype(BF16)
    w_g = wt[RM_COLS:].astype(BF16)
    g_mix = norm_mix_g[0].reshape(1, D_MODEL)
    gb = gate_b[0].reshape(1, GATE_COLS)
    rp = dict(mu=r_mu[0].reshape(1, -1), w0=r_w0[0].reshape(1, -1), w2=r_w2[0].astype(BF16),
              a0=r_a0[0].reshape(1, -1), a2=r_a2[0].astype(BF16), g2=r_g2[0].astype(BF16),
              kk=r_kk[0].reshape(1, -1), ka=r_ka[0].reshape(1, -1), rk=r_rk[0].reshape(1, -1),
              gn_w=r_gn_w[0].reshape(1, -1), gn_b=r_gn_b[0].reshape(1, -1), seg=_head_segments())
    mp = dict(conv_w=m_conv_w[0], conv_b=m_conv_b[0].reshape(1, -1), wq=m_wq[0].astype(BF16),
              wk=m_wk[0].astype(BF16), i_b=_pad_lanes(m_i_b[0], 0), f_b=_pad_lanes(m_f_b[0], M_HEADS),
              gn_w=m_gn_w[0].reshape(1, -1), skip=m_skip[0].reshape(1, -1))
    op = dict(r_up=r_up[0].astype(BF16), m_up=m_up[0].astype(BF16), w_out=w_out[0].astype(BF16),
              g_ffn=norm_ffn_g[0].reshape(1, -1), w1=ffn_w1[0].astype(BF16), w2=ffn_w2[0].astype(BF16),
              g_fin=norm_final_g.reshape(1, -1))

    xp = x_prompt.reshape(nbp * seq, D_MODEL)
    pr, pm, pg = _proj(xp, g_mix, w_rm, w_g, gb, PROJ_TM)
    y_r, wkv_p = _rwkv_seq(pr.reshape(SEQ_WAYS, -1, R_SHIFT_COLS), rp, nbp, seq // RWKV_TT, RWKV_TT)
    y_r = y_r.reshape(nbp * seq, R_WIDTH)
    wkv_p = wkv_p.reshape(nbp, R_HEADS, R_HEAD_DIM, R_HEAD_DIM)
    y_m, c_p, n_p, m_p = _mlstm_seq(pm, mp, nbp, seq // MLSTM_TT, MLSTM_TT)
    y_p = _out(xp, pg, y_r, y_m, op, OUT_TM).reshape(nbp, seq, D_MODEL)
    conv_p = pm.reshape(nbp, seq, M_COLS)[:, seq - (CONV_W - 1):, :M_WIDTH]

    xn_s, xn_last, pr_s, pm_s, pg_s, prev_s = _proj_step(
        x_sample[:, 0], state_rwkv_shift[0], x_prompt[:, seq - 1], g_mix, w_rm, w_g, gb)
    yr_s, wkv_t = _rwkv_step(pr_s, prev_s, jnp.transpose(state_rwkv_wkv[0], (1, 2, 3, 0)), rp)
    wkv_s = jnp.transpose(wkv_t, (3, 0, 1, 2))
    conv0 = jnp.transpose(state_mlstm_conv[0], (1, 0, 2))
    ym_s, c_s, n_s, m_s, conv_s = _mlstm_step(pm_s, conv0, state_mlstm_C[0],
                                              state_mlstm_n[0].reshape(nbs, M_WIDTH),
                                              state_mlstm_m[0], mp, STEP_BB)
    y_s = _out(x_sample[:, 0], pg_s, yr_s, ym_s, op, nbs).reshape(nbs, 1, D_MODEL)

    return (y_p, y_s,
            xn_last[None], wkv_p[None], c_p[None], n_p[:, :, 0, :][None], m_p[:, :, 0, 0][None], conv_p[None],
            xn_s[None], wkv_s[None], c_s[None], n_s.reshape(nbs, M_HEADS, M_HEAD_DIM)[None], m_s[None],
            jnp.transpose(conv_s, (1, 0, 2))[None])
```

```python
import functools
import math

import jax
import jax.numpy as jnp
from jax import lax
from jax.experimental import pallas as pl
from jax.experimental.pallas import tpu as pltpu

F32 = jnp.float32
BF16 = jnp.bfloat16

D_MODEL = 1024
R_HEADS = 8
R_HEAD_DIM = 64
R_WIDTH = R_HEADS * R_HEAD_DIM
GROUP_HEADS = 4
R_GROUPS = R_HEADS // GROUP_HEADS
GROUP_W = GROUP_HEADS * R_HEAD_DIM
W_LORA = 64
A_LORA = 64
G_LORA = 128
R_GN_EPS = 64e-5
M_HEADS = 4
M_HEAD_DIM = 128
M_WIDTH = M_HEADS * M_HEAD_DIM
CONV_W = 4
M_GN_EPS = 1e-5
D_FF = 4 * D_MODEL
RMS_EPS = 1e-6
R_SHIFT_COLS = 3 * R_WIDTH + W_LORA + A_LORA + G_LORA
M_COLS = 3 * M_WIDTH + 2 * M_HEADS
GATE_COLS = 2 * D_MODEL

LANES = 128
SUBLANES = 8
CHUNK = 64
RM_COLS = R_SHIFT_COLS + M_COLS
VMEM_LIMIT = 56 * 1024 * 1024


def _mm(a, b, dims="nn"):
    ca = 1 if dims[0] == "n" else 0
    cb = 0 if dims[1] == "n" else 1
    dn = (((ca,), (cb,)), ((), ()))
    return lax.dot_general(a.astype(BF16), b.astype(BF16), dn, preferred_element_type=F32)


def _mm_mask_lhs(mask, b):
    b_hi = b.astype(BF16)
    return _mm(mask, b_hi) + _mm(mask, b - b_hi.astype(F32))


def _mm_mask_rhs(a, mask):
    a_hi = a.astype(BF16)
    return _mm(a_hi, mask) + _mm(a - a_hi.astype(F32), mask)


def _rms(x, g):
    return x * lax.rsqrt(jnp.mean(x * x, axis=-1, keepdims=True) + RMS_EPS) * g


def _sigmoid(x):
    return 1.0 / (1.0 + jnp.exp(-x))


def _log_sigmoid(x):
    return jnp.minimum(x, 0.0) - jnp.log(1.0 + jnp.exp(-jnp.abs(x)))


def _iota2(shape, axis):
    return lax.broadcasted_iota(jnp.int32, shape, axis)


def _const_spec(shape):
    nd = len(shape)
    return pl.BlockSpec(shape, lambda *_: (0,) * nd, pipeline_mode=pl.Buffered(1))


def _proj_cols(xb, wt_ref, lo, hi):
    return lax.dot_general(xb, wt_ref[lo:hi, :], (((1,), (1,)), ((), ())), preferred_element_type=F32)


def _proj_body(x_ref, g_ref, wrm_ref, wg_ref, gb_ref, pr_ref, pm_ref, pg_ref):
    xb = _rms(x_ref[...], g_ref[...]).astype(BF16)
    pr_ref[...] = _proj_cols(xb, wrm_ref, 0, R_SHIFT_COLS)
    pm_ref[...] = _proj_cols(xb, wrm_ref, R_SHIFT_COLS, RM_COLS)
    pg_ref[...] = _proj_cols(xb, wg_ref, 0, GATE_COLS) + gb_ref[...]


def _proj(x2, g, w_rm, w_g, gate_b, tm):
    m = x2.shape[0]
    return pl.pallas_call(
        _proj_body,
        grid=(m // tm,),
        in_specs=[pl.BlockSpec((tm, D_MODEL), lambda i: (i, 0)),
                  _const_spec((1, D_MODEL)),
                  _const_spec((RM_COLS, D_MODEL)), _const_spec((GATE_COLS, D_MODEL)),
                  _const_spec((1, GATE_COLS))],
        out_specs=[pl.BlockSpec((tm, R_SHIFT_COLS), lambda i: (i, 0)),
                   pl.BlockSpec((tm, M_COLS), lambda i: (i, 0)),
                   pl.BlockSpec((tm, GATE_COLS), lambda i: (i, 0))],
        out_shape=[jax.ShapeDtypeStruct((m, R_SHIFT_COLS), F32),
                   jax.ShapeDtypeStruct((m, M_COLS), F32),
                   jax.ShapeDtypeStruct((m, GATE_COLS), F32)],
        compiler_params=pltpu.CompilerParams(dimension_semantics=("arbitrary",), vmem_limit_bytes=VMEM_LIMIT),
        name="proj",
    )(x2, g, w_rm, w_g, gate_b)


def _proj_step_body(xs_ref, sh_ref, xl_ref, g_ref, wrm_ref, wg_ref, gb_ref,
                    xns_ref, xnl_ref, pr_ref, pm_ref, pg_ref, prev_ref):
    g = g_ref[...]
    xn = _rms(xs_ref[...], g)
    xns_ref[...] = xn
    xnl_ref[...] = _rms(xl_ref[...], g)
    xb = xn.astype(BF16)
    pr_ref[...] = _proj_cols(xb, wrm_ref, 0, R_SHIFT_COLS)
    pm_ref[...] = _proj_cols(xb, wrm_ref, R_SHIFT_COLS, RM_COLS)
    pg_ref[...] = _proj_cols(xb, wg_ref, 0, GATE_COLS) + gb_ref[...]
    prev_ref[...] = _proj_cols(sh_ref[...].astype(BF16), wrm_ref, 0, R_SHIFT_COLS)


def _proj_step(xs, shift0, xlast, g, w_rm, w_g, gate_b):
    nb = xs.shape[0]
    nl = xlast.shape[0]
    return pl.pallas_call(
        _proj_step_body,
        out_shape=[jax.ShapeDtypeStruct((nb, D_MODEL), F32),
                   jax.ShapeDtypeStruct((nl, D_MODEL), F32),
                   jax.ShapeDtypeStruct((nb, R_SHIFT_COLS), F32),
                   jax.ShapeDtypeStruct((nb, M_COLS), F32),
                   jax.ShapeDtypeStruct((nb, GATE_COLS), F32),
                   jax.ShapeDtypeStruct((nb, R_SHIFT_COLS), F32)],
        compiler_params=pltpu.CompilerParams(vmem_limit_bytes=VMEM_LIMIT),
        name="proj_step",
    )(xs, shift0, xlast, g, w_rm, w_g, gate_b)


LOG_DECAY_SCALE = -math.exp(-0.5)


def _rwkv_mix(pr, prev, mu, w0, w2, a0, a2, g2, r_kk, r_ka):
    mixed = pr + (prev - pr) * mu
    r = mixed[:, 0:R_WIDTH]
    k = mixed[:, R_WIDTH:2 * R_WIDTH]
    v = mixed[:, 2 * R_WIDTH:3 * R_WIDTH]
    xw = mixed[:, 3 * R_WIDTH:3 * R_WIDTH + W_LORA]
    xa = mixed[:, 3 * R_WIDTH + W_LORA:3 * R_WIDTH + W_LORA + A_LORA]
    xg = mixed[:, 3 * R_WIDTH + W_LORA + A_LORA:]
    w = w0 + _mm(jnp.tanh(xw), w2)
    logw = LOG_DECAY_SCALE * _sigmoid(w)
    a = _sigmoid(a0 + _mm(xa, a2))
    g = _mm(_sigmoid(xg), g2)
    kk0 = k * r_kk
    k2 = k * (1.0 + (a - 1.0) * r_ka)
    return r, k2, v, logw, a, g, kk0


def _rwkv_tile(pr_ref, mu_ref, w0_ref, w2_ref, a0_ref, a2_ref, g2_ref, kk_ref, ka_ref, rk_ref,
               gnw_ref, gnb_ref, seg_ref, s_scr, carry_scr, tt):
    pr = pr_ref[...]
    row = _iota2(pr.shape, 0)
    prev = jnp.where(row == 0, carry_scr[...], pltpu.roll(pr, 1, axis=0))
    carry_scr[...] = pr[tt - 1:tt, :]

    r, k2, v, logw, a, g, kk0 = _rwkv_mix(pr, prev, mu_ref[...], w0_ref[...], w2_ref[...], a0_ref[...],
                                          a2_ref[...], g2_ref[...], kk_ref[...], ka_ref[...])
    yield
    ri = _iota2((CHUNK, GROUP_W), 0)
    ci = _iota2((CHUNK, GROUP_W), 1)
    src = ci % R_HEAD_DIM
    lane_head = ci // R_HEAD_DIM
    low_incl = src <= ri
    low_strict = src < ri
    eye = (src == ri).astype(F32)
    rr = _iota2((GROUP_W, GROUP_W), 0)
    cc = _iota2((GROUP_W, GROUP_W), 1)
    same_head = (rr // R_HEAD_DIM) == (cc // R_HEAD_DIM)
    eye_bd = (rr == cc).astype(F32)
    seg = seg_ref[...]

    def seg_sum(x, mm=_mm):
        return jnp.concatenate([mm(x[:, q * GROUP_W:(q + 1) * GROUP_W], seg) for q in range(R_GROUPS)], axis=1)
    tril = (_iota2((CHUNK, CHUNK), 1) <= _iota2((CHUNK, CHUNK), 0)).astype(F32)

    nc = tt // CHUNK
    cum = jnp.concatenate([_mm_mask_lhs(tril, logw[c * CHUNK:(c + 1) * CHUNK]) for c in range(nc)], axis=0)
    yield
    cum_last = jnp.concatenate(
        [jnp.broadcast_to(cum[(c + 1) * CHUNK - 1:(c + 1) * CHUNK, :], (CHUNK, R_WIDTH)) for c in range(nc)], axis=0)
    d_inv = jnp.exp(-cum)
    d_end = jnp.exp(cum_last - cum)
    d_last = jnp.exp(cum_last)
    kk = kk0 / jnp.maximum(jnp.sqrt(seg_sum(kk0 * kk0)), 1e-12)
    yield
    kka = kk * a
    at = kk * jnp.exp(cum - logw)
    rt = r * jnp.exp(cum)
    bt = kka * d_inv
    kt = k2 * d_inv
    bh = kka * d_end
    kh = k2 * d_end

    items = [(c, p) for c in range(nc) for p in range(R_GROUPS)]
    blk = lambda x, c, p: x[c * CHUNK:(c + 1) * CHUNK, p * GROUP_W:(p + 1) * GROUP_W]
    each = lambda f, *ls: [f(*xs) for xs in zip(*ls)]
    cat0 = lambda *xs: jnp.concatenate(xs, axis=0)
    cat1 = lambda *xs: jnp.concatenate(xs, axis=1)

    def bd(x):
        return cat0(*[jnp.where(lane_head == j, x, 0.0) for j in range(GROUP_HEADS)])

    a_ = [blk(at, c, p) for c, p in items]
    b_ = [blk(bt, c, p) for c, p in items]
    bh_ = [blk(bh, c, p) for c, p in items]
    r_ = [blk(rt, c, p) for c, p in items]
    k_ = [blk(kt, c, p) for c, p in items]
    kh_ = [blk(kh, c, p) for c, p in items]
    v_ = [blk(v, c, p) for c, p in items]
    vbd = [bd(x) for x in v_]
    pq = each(lambda a1, r1, b1, k1: _mm(cat0(a1, r1), cat0(bd(b1), bd(k1)), "nt"), a_, r_, b_, k_)
    yield
    t_ab = [jnp.where(low_strict, x[:CHUNK, :GROUP_W], 0.0) for x in pq]
    t_ak = [jnp.where(low_strict, x[:CHUNK, GROUP_W:], 0.0) for x in pq]
    m_rb = [jnp.where(low_incl, x[CHUNK:, :GROUP_W], 0.0) for x in pq]
    m_rk = [jnp.where(low_incl, x[CHUNK:, GROUP_W:], 0.0) for x in pq]
    q_pow = each(lambda t: _mm(t, bd(t)), t_ab)
    yield
    x_inv = [eye - t for t in t_ab]
    for _ in range(4):
        z = each(lambda x, q: _mm(cat0(x, q), bd(q)), x_inv, q_pow)
        x_inv = [x + zz[:CHUNK] for x, zz in zip(x_inv, z)]
        q_pow = [zz[CHUNK:] for zz in z]
        yield
    x_inv = each(lambda x, q: x + _mm(x, bd(q)), x_inv, q_pow)
    yield
    tv = each(_mm, t_ak, vbd)
    yield
    xw = each(lambda x, a1, t: _mm(x, cat1(bd(a1), bd(t))), x_inv, a_, tv)
    yield
    mw = each(lambda m1, x: _mm(m1, cat1(bd(x[:, :GROUP_W]), bd(x[:, GROUP_W:]))), m_rb, xw)
    yield
    q_eff = [r1 - m1[:, :GROUP_W] for r1, m1 in zip(r_, mw)]
    y_loc = each(lambda m1, vb, m2: _mm(m1, vb) - m2[:, GROUP_W:], m_rk, vbd, mw)
    yield
    g_bd = [eye_bd * blk(d_last, c, p)[0:1] - jnp.where(same_head, _mm(x[:, :GROUP_W], b1, "tn"), 0.0)
            for (c, p), x, b1 in zip(items, xw, bh_)]
    yield
    h_full = each(lambda vv, x, k1, b1: _mm(cat0(vv, x[:, GROUP_W:]), cat0(k1, -b1), "tn"), v_, xw, kh_, bh_)
    yield
    h_pair = []
    for x in h_full:
        acc = x[:CHUNK]
        for j in range(1, GROUP_HEADS):
            acc = jnp.where(lane_head == j, x[j * CHUNK:(j + 1) * CHUNK], acc)
        h_pair.append(acc)

    state = [s_scr[p] for p in range(R_GROUPS)]
    ys = []
    for i, (c, p) in enumerate(items):
        ys.append(_mm(q_eff[i], bd(state[p]), "nt") + y_loc[i])
        state[p] = _mm(state[p], g_bd[i]) + h_pair[i]
        if p == R_GROUPS - 1:
            yield
    for p in range(R_GROUPS):
        s_scr[p] = state[p]

    y_all = cat0(*[cat1(*ys[c * R_GROUPS:(c + 1) * R_GROUPS]) for c in range(nc)])
    yc = y_all - seg_sum(y_all) * (1.0 / R_HEAD_DIM)
    yield
    var = seg_sum(yc * yc) * (1.0 / R_HEAD_DIM)
    yield
    bonus = seg_sum(r * k2 * rk_ref[...], _mm_mask_rhs)
    return (yc * lax.rsqrt(var + R_GN_EPS) * gnw_ref[...] + gnb_ref[...] + bonus * v) * g


N_RWKV_PARAMS = 12


SEQ_WAYS = 2
SEQ_LAG = 4


def _staggered(gens, lag):
    out = [None] * len(gens)
    live = set(range(len(gens)))
    step = 0
    while live:
        for j in sorted(live):
            if step >= j * lag:
                try:
                    next(gens[j])
                except StopIteration as stop:
                    out[j] = stop.value
                    live.discard(j)
        step += 1
    return out


def _rwkv_seq_body(*refs, tt):
    pr_ref, params = refs[0], refs[1:1 + N_RWKV_PARAMS]
    y_ref, s_out_ref, s_scr, carry_scr = refs[1 + N_RWKV_PARAMS:]
    t_idx = pl.program_id(1)

    @pl.when(t_idx == 0)
    def _():
        s_scr[...] = jnp.zeros_like(s_scr)
        carry_scr[...] = jnp.zeros_like(carry_scr)

    ys = _staggered([_rwkv_tile(pr_ref.at[w], *params, s_scr.at[w], carry_scr.at[w], tt)
                     for w in range(SEQ_WAYS)], SEQ_LAG)
    for w in range(SEQ_WAYS):
        y_ref[w] = ys[w]

    @pl.when(t_idx == pl.num_programs(1) - 1)
    def _():
        for w in range(SEQ_WAYS):
            for p in range(R_GROUPS):
                for j in range(GROUP_HEADS):
                    s_out_ref[w, 0, GROUP_HEADS * p + j] = s_scr[w, p][:, j * R_HEAD_DIM:(j + 1) * R_HEAD_DIM]


def _rwkv_seq(pr, p, nb, nt, tt):
    m = pr.shape[1]
    row = lambda n: _const_spec((1, n))
    return pl.pallas_call(
        functools.partial(_rwkv_seq_body, tt=tt),
        grid=(nb // SEQ_WAYS, nt),
        in_specs=[pl.BlockSpec((SEQ_WAYS, tt, R_SHIFT_COLS), lambda b, t: (0, b * nt + t, 0)),
                  row(R_SHIFT_COLS), row(R_WIDTH), _const_spec((W_LORA, R_WIDTH)), row(R_WIDTH),
                  _const_spec((A_LORA, R_WIDTH)), _const_spec((G_LORA, R_WIDTH)), row(R_WIDTH), row(R_WIDTH),
                  row(R_WIDTH), row(R_WIDTH), row(R_WIDTH), _const_spec((GROUP_W, GROUP_W))],
        out_specs=[pl.BlockSpec((SEQ_WAYS, tt, R_WIDTH), lambda b, t: (0, b * nt + t, 0)),
                   pl.BlockSpec((SEQ_WAYS, 1, R_HEADS, R_HEAD_DIM, R_HEAD_DIM), lambda b, t: (0, b, 0, 0, 0))],
        out_shape=[jax.ShapeDtypeStruct((SEQ_WAYS, m, R_WIDTH), F32),
                   jax.ShapeDtypeStruct((SEQ_WAYS, nb // SEQ_WAYS, R_HEADS, R_HEAD_DIM, R_HEAD_DIM), F32)],
        scratch_shapes=[pltpu.VMEM((SEQ_WAYS, R_GROUPS, R_HEAD_DIM, GROUP_W), F32),
                        pltpu.VMEM((SEQ_WAYS, 1, R_SHIFT_COLS), F32)],
        compiler_params=pltpu.CompilerParams(dimension_semantics=("arbitrary", "arbitrary"),
                                             vmem_limit_bytes=VMEM_LIMIT),
        name="rwkv_seq",
    )(pr, p["mu"], p["w0"], p["w2"], p["a0"], p["a2"], p["g2"], p["kk"], p["ka"], p["rk"],
      p["gn_w"], p["gn_b"], p["seg"])


def _rwkv_step_body(pr_ref, prev_ref, s_ref, mu_ref, w0_ref, w2_ref, a0_ref, a2_ref, g2_ref, kk_ref, ka_ref,
                    rk_ref, gnw_ref, gnb_ref, seg_ref, y_ref, s_out_ref,
                    kk_scr, kka_scr, dec_scr, k_scr, v_scr, r_scr, yt_scr, g_scr, vrow_scr, bonus_scr):
    h = pl.program_id(0)
    seg = seg_ref[...]

    def seg_sum(x, mm=_mm):
        return jnp.concatenate([mm(x[:, q * GROUP_W:(q + 1) * GROUP_W], seg) for q in range(R_GROUPS)], axis=1)

    @pl.when(h == 0)
    def _():
        r, k2, v, logw, a, g, kk0 = _rwkv_mix(pr_ref[...], prev_ref[...], mu_ref[...], w0_ref[...], w2_ref[...],
                                              a0_ref[...], a2_ref[...], g2_ref[...], kk_ref[...], ka_ref[...])
        kk = kk0 / jnp.maximum(jnp.sqrt(seg_sum(kk0 * kk0)), 1e-12)
        kk_scr[...] = kk.T
        kka_scr[...] = (kk * a).T
        dec_scr[...] = jnp.exp(logw).T
        k_scr[...] = k2.T
        v_scr[...] = v.T
        r_scr[...] = r.T
        g_scr[...] = g
        vrow_scr[...] = v
        bonus_scr[...] = seg_sum(r * k2 * rk_ref[...], _mm_mask_rhs)

    base = pl.multiple_of(h * R_HEAD_DIM, R_HEAD_DIM)
    hs = pl.ds(base, R_HEAD_DIM)
    kk_t, kka_t, dec_t, k_t, r_t = kk_scr[hs, :], kka_scr[hs, :], dec_scr[hs, :], k_scr[hs, :], r_scr[hs, :]
    for i in range(R_HEAD_DIM):
        s0 = s_ref[0, i]
        s_kk = jnp.sum(s0 * kk_t, axis=0, keepdims=True)
        s1 = s0 * dec_t - s_kk * kka_t + v_scr[pl.ds(base + i, 1), :] * k_t
        s_out_ref[0, i] = s1
        yt_scr[pl.ds(base + i, 1), :] = jnp.sum(s1 * r_t, axis=0, keepdims=True)

    @pl.when(h == pl.num_programs(0) - 1)
    def _():
        y = yt_scr[...].T
        yc = y - seg_sum(y) * (1.0 / R_HEAD_DIM)
        var = seg_sum(yc * yc) * (1.0 / R_HEAD_DIM)
        y_ref[...] = (yc * lax.rsqrt(var + R_GN_EPS) * gnw_ref[...] + gnb_ref[...]
                      + bonus_scr[...] * vrow_scr[...]) * g_scr[...]


def _rwkv_step(pr, prev, s0_t, p):
    nb = pr.shape[0]
    row = lambda n: _const_spec((1, n))
    state_spec = pl.BlockSpec((1, R_HEAD_DIM, R_HEAD_DIM, nb), lambda h: (h, 0, 0, 0))
    tr = pltpu.VMEM((R_WIDTH, nb), F32)
    rw = pltpu.VMEM((nb, R_WIDTH), F32)
    return pl.pallas_call(
        _rwkv_step_body,
        grid=(R_HEADS,),
        in_specs=[_const_spec((nb, R_SHIFT_COLS)), _const_spec((nb, R_SHIFT_COLS)), state_spec,
                  row(R_SHIFT_COLS), row(R_WIDTH), _const_spec((W_LORA, R_WIDTH)), row(R_WIDTH),
                  _const_spec((A_LORA, R_WIDTH)), _const_spec((G_LORA, R_WIDTH)), row(R_WIDTH), row(R_WIDTH),
                  row(R_WIDTH), row(R_WIDTH), row(R_WIDTH), _const_spec((GROUP_W, GROUP_W))],
        out_specs=[pl.BlockSpec((nb, R_WIDTH), lambda h: (0, 0)), state_spec],
        out_shape=[jax.ShapeDtypeStruct((nb, R_WIDTH), F32),
                   jax.ShapeDtypeStruct(s0_t.shape, F32)],
        scratch_shapes=[tr, tr, tr, tr, tr, tr, tr, rw, rw, rw],
        compiler_params=pltpu.CompilerParams(dimension_semantics=("arbitrary",), vmem_limit_bytes=VMEM_LIMIT),
        name="rwkv_step",
    )(pr, prev, s0_t, p["mu"], p["w0"], p["w2"], p["a0"], p["a2"], p["g2"], p["kk"], p["ka"], p["rk"],
      p["gn_w"], p["gn_b"], p["seg"])


GATE_LANE0 = 3 * M_WIDTH


def _gate_lanes(pm_ref):
    gates = pm_ref[:, GATE_LANE0:GATE_LANE0 + 2 * M_HEADS]
    return jnp.concatenate([gates, jnp.zeros((gates.shape[0], LANES - 2 * M_HEADS), F32)], axis=1)


def _head_norm_rows(x, eps):
    xc = x - jnp.mean(x, axis=1, keepdims=True)
    var = jnp.mean(xc * xc, axis=1, keepdims=True)
    return xc * lax.rsqrt(var + eps)


def _cummax_rows(x):
    n = x.shape[0]
    row = _iota2(x.shape, 0)
    d = 1
    while d < n:
        if d < SUBLANES:
            shifted = jnp.where(row < d, -jnp.inf, pltpu.roll(x, d, axis=0))
        else:
            shifted = jnp.concatenate([jnp.full((d, x.shape[1]), -jnp.inf, x.dtype), x[:n - d]], axis=0)
        x = jnp.maximum(x, shifted)
        d *= 2
    return x


def _mlstm_seq_body(pm_ref, cw_ref, cb_ref, wq_ref, wk_ref, ib_ref, fb_ref, gnw_ref, skip_ref,
                    y_ref, c_out_ref, n_out_ref, m_out_ref, c_scr, n_scr, m_scr, carry_scr, *, tt):
    t_idx = pl.program_id(1)

    @pl.when(t_idx == 0)
    def _():
        c_scr[...] = jnp.zeros_like(c_scr)
        n_scr[...] = jnp.zeros_like(n_scr)
        m_scr[...] = jnp.zeros_like(m_scr)
        carry_scr[...] = jnp.zeros_like(carry_scr)

    xm = pm_ref[:, 0:M_WIDTH]
    carry = carry_scr[...]
    carry_scr[...] = xm[tt - SUBLANES:tt, :]
    row8 = _iota2((SUBLANES, M_WIDTH), 0)
    cw = cw_ref[...]
    xc = cb_ref[...] + xm * cw[CONV_W - 1:CONV_W, :]
    for s in range(1, CONV_W):
        rolled = pltpu.roll(xm, s, axis=0)
        top = jnp.where(row8 < s, pltpu.roll(carry, s, axis=0), rolled[0:SUBLANES])
        shifted = jnp.concatenate([top, rolled[SUBLANES:]], axis=0)
        xc = xc + shifted * cw[CONV_W - 1 - s:CONV_W - s, :]
    xc = xc * _sigmoid(xc)

    gt = _gate_lanes(pm_ref)
    li_all = gt + ib_ref[...]
    lf_all = _log_sigmoid(gt + fb_ref[...])

    ri = _iota2((CHUNK, CHUNK), 0)
    ci = _iota2((CHUNK, CHUNK), 1)
    causal = ci <= ri
    tril = causal.astype(F32)
    gnw = gnw_ref[...]
    skip = skip_ref[...]

    ones_cl = jnp.ones((CHUNK, LANES), BF16)
    ones_ll = jnp.ones((LANES, LANES), BF16)
    nc = tt // CHUNK
    rows = lambda c: slice(c * CHUNK, (c + 1) * CHUNK)
    lanes = lambda h: slice(h * M_HEAD_DIM, (h + 1) * M_HEAD_DIM)

    bc_all = jnp.concatenate([_mm_mask_lhs(tril, lf_all[rows(c)]) for c in range(nc)], axis=0)
    bc_al = pltpu.roll(bc_all, LANES - M_HEADS, axis=1)
    u_all = li_all - bc_al
    cm_all = jnp.concatenate([_cummax_rows(u_all[rows(c)]) for c in range(nc)], axis=0)
    u_t = u_all.T

    items = [(c, h) for c in range(nc) for h in range(M_HEADS)]
    rep = lambda x, c, h: jnp.broadcast_to(x[rows(c), h:h + 1], (CHUNK, LANES))
    xc_b = [xc[rows(c), lanes(h)] for c, h in items]
    q = [_mm(x, wq_ref[h]) * (M_HEAD_DIM ** -0.5) for (c, h), x in zip(items, xc_b)]
    k = [_mm(x, wk_ref[h]) for (c, h), x in zip(items, xc_b)]
    vv = [pm_ref[rows(c), M_WIDTH + h * M_HEAD_DIM:M_WIDTH + (h + 1) * M_HEAD_DIM] for c, h in items]
    cm_r = [rep(cm_all, c, h) for c, h in items]
    bc_r = [rep(bc_al, c, h) for c, h in items]
    u_r = [rep(u_all, c, h) for c, h in items]

    m_run = [m_scr[h][0:1, 0:1] for h in range(M_HEADS)]
    m_prev, m_cap, m_end = [], [], []
    for i, (c, h) in enumerate(items):
        m_prev.append(m_run[h])
        m_cap.append(jnp.maximum(m_run[h], cm_r[i]))
        m_end.append(m_cap[i][CHUNK - 1:CHUNK, 0:1])
        m_run[h] = bc_r[i][CHUNK - 1:CHUNK, 0:1] + m_end[i]

    w_st = [jnp.exp(mp - mc) for mp, mc in zip(m_prev, m_cap)]
    floor = [jnp.exp(-(b + mc)) for b, mc in zip(bc_r, m_cap)]
    w_in = [jnp.exp(jnp.where(causal, u_t[h:h + 1, rows(c)] - mc[:, :CHUNK], -jnp.inf))
            for (c, h), mc in zip(items, m_cap)]
    s = [_mm(qq, kk, "nt") * w for qq, kk, w in zip(q, k, w_in)]
    sv = [_mm(ss, v1) for ss, v1 in zip(s, vv)]
    s_sum = [_mm(ss, ones_cl) for ss in s]
    we = [jnp.exp(u - me) for u, me in zip(u_r, m_end)]
    ge = [jnp.exp(mp - me) for mp, me in zip(m_prev, m_end)]
    kv = [_mm(kk, w1 * v1, "tn") for kk, w1, v1 in zip(k, we, vv)]
    k_sum = [jnp.sum(w1 * kk, axis=0, keepdims=True) for w1, kk in zip(we, k)]

    c_run = [c_scr[h] for h in range(M_HEADS)]
    n_run = [n_scr[h][0:1, :] for h in range(M_HEADS)]
    num, den = [], []
    for i, (c, h) in enumerate(items):
        num.append(w_st[i] * _mm(q[i], c_run[h]) + sv[i])
        den.append(w_st[i] * _mm(q[i] * n_run[h], ones_ll) + s_sum[i])
        c_run[h] = ge[i] * c_run[h] + kv[i]
        n_run[h] = ge[i] * n_run[h] + k_sum[i]
    for h in range(M_HEADS):
        c_scr[h] = c_run[h]
        n_scr[h] = jnp.broadcast_to(n_run[h], (SUBLANES, M_HEAD_DIM))
        m_scr[h] = jnp.broadcast_to(m_run[h], (SUBLANES, LANES))

    for i, (c, h) in enumerate(items):
        hh = num[i] / jnp.maximum(jnp.abs(den[i]), floor[i])
        hc = hh - _mm(hh, ones_ll) * (1.0 / M_HEAD_DIM)
        var = _mm(hc * hc, ones_ll) * (1.0 / M_HEAD_DIM)
        hn = hc * lax.rsqrt(var + M_GN_EPS) * gnw[:, lanes(h)] + skip[:, lanes(h)] * xc_b[i]
        o = pm_ref[rows(c), 2 * M_WIDTH + h * M_HEAD_DIM:2 * M_WIDTH + (h + 1) * M_HEAD_DIM]
        y_ref[rows(c), lanes(h)] = _sigmoid(o) * hn

    @pl.when(t_idx == pl.num_programs(1) - 1)
    def _():
        c_out_ref[0] = c_scr[...]
        n_out_ref[0] = n_scr[...]
        m_out_ref[0] = m_scr[...]


def _mlstm_seq(pm, p, nb, nt, tt):
    m = pm.shape[0]
    row = lambda n: _const_spec((1, n))
    return pl.pallas_call(
        functools.partial(_mlstm_seq_body, tt=tt),
        grid=(nb, nt),
        in_specs=[pl.BlockSpec((tt, M_COLS), lambda b, t: (b * nt + t, 0)),
                  _const_spec((CONV_W, M_WIDTH)), row(M_WIDTH),
                  _const_spec((M_HEADS, M_HEAD_DIM, M_HEAD_DIM)), _const_spec((M_HEADS, M_HEAD_DIM, M_HEAD_DIM)),
                  row(LANES), row(LANES), row(M_WIDTH), row(M_WIDTH)],
        out_specs=[pl.BlockSpec((tt, M_WIDTH), lambda b, t: (b * nt + t, 0)),
                   pl.BlockSpec((1, M_HEADS, M_HEAD_DIM, M_HEAD_DIM), lambda b, t: (b, 0, 0, 0)),
                   pl.BlockSpec((1, M_HEADS, SUBLANES, M_HEAD_DIM), lambda b, t: (b, 0, 0, 0)),
                   pl.BlockSpec((1, M_HEADS, SUBLANES, LANES), lambda b, t: (b, 0, 0, 0))],
        out_shape=[jax.ShapeDtypeStruct((m, M_WIDTH), F32),
                   jax.ShapeDtypeStruct((nb, M_HEADS, M_HEAD_DIM, M_HEAD_DIM), F32),
                   jax.ShapeDtypeStruct((nb, M_HEADS, SUBLANES, M_HEAD_DIM), F32),
                   jax.ShapeDtypeStruct((nb, M_HEADS, SUBLANES, LANES), F32)],
        scratch_shapes=[pltpu.VMEM((M_HEADS, M_HEAD_DIM, M_HEAD_DIM), F32),
                        pltpu.VMEM((M_HEADS, SUBLANES, M_HEAD_DIM), F32),
                        pltpu.VMEM((M_HEADS, SUBLANES, LANES), F32),
                        pltpu.VMEM((SUBLANES, M_WIDTH), F32)],
        compiler_params=pltpu.CompilerParams(dimension_semantics=("arbitrary", "arbitrary"),
                                             vmem_limit_bytes=VMEM_LIMIT),
        name="mlstm_seq",
    )(pm, p["conv_w"], p["conv_b"], p["wq"], p["wk"], p["i_b"], p["f_b"], p["gn_w"], p["skip"])


def _mlstm_step_body(pm_ref, conv_ref, c_ref, n_ref, m_ref, cw_ref, cb_ref, wq_ref, wk_ref, ib_ref, fb_ref,
                     gnw_ref, skip_ref, y_ref, c_out_ref, n_out_ref, m_out_ref, conv_out_ref, hv_scr, *, bb):
    xm = pm_ref[:, 0:M_WIDTH]
    cw = cw_ref[...]
    xc = cb_ref[...] + xm * cw[CONV_W - 1:CONV_W, :]
    for j in range(CONV_W - 1):
        xc = xc + conv_ref[j] * cw[j:j + 1, :]
    xc = xc * _sigmoid(xc)
    for j in range(CONV_W - 2):
        conv_out_ref[j] = conv_ref[j + 1]
    conv_out_ref[CONV_W - 2] = xm

    gt = _gate_lanes(pm_ref)
    li = (gt + ib_ref[...])[:, 0:M_HEADS]
    lf = _log_sigmoid(gt + fb_ref[...])[:, M_HEADS:2 * M_HEADS]
    m_prev = m_ref[...]
    g_st = lf + m_prev
    m_t = jnp.maximum(g_st, (lf - lf) + li)
    w_in = jnp.exp(((lf - lf) + li) - m_t)
    w_st = jnp.exp(g_st - m_t)
    floor = jnp.exp(-m_t)
    m_out_ref[...] = m_t

    ri = _iota2((M_HEAD_DIM, M_HEAD_DIM), 0)
    ci = _iota2((M_HEAD_DIM, M_HEAD_DIM), 1)
    eye = ri == ci
    heads = [slice(h * M_HEAD_DIM, (h + 1) * M_HEAD_DIM) for h in range(M_HEADS)]
    q = [_mm(xc[:, hs], wq_ref[h]) * (M_HEAD_DIM ** -0.5) for h, hs in enumerate(heads)]
    k = [_mm(xc[:, hs], wk_ref[h]) for h, hs in enumerate(heads)]
    vv = [pm_ref[:, M_WIDTH + h * M_HEAD_DIM:M_WIDTH + (h + 1) * M_HEAD_DIM] for h in range(M_HEADS)]
    s, inv = [], []
    for h, hs in enumerate(heads):
        n0 = n_ref[:, hs]
        qk = jnp.sum(q[h] * k[h], axis=1, keepdims=True)
        qn = jnp.sum(q[h] * n0, axis=1, keepdims=True)
        s.append(qk * w_in[:, h:h + 1])
        den = w_st[:, h:h + 1] * qn + s[h]
        inv.append(1.0 / jnp.maximum(jnp.abs(den), floor[:, h:h + 1]))
        n_out_ref[:, hs] = w_st[:, h:h + 1] * n0 + w_in[:, h:h + 1] * k[h]
    items = [(h, b) for h in range(M_HEADS) for b in range(bb)]
    c0 = [c_ref[b, h] for h, b in items]
    q_col = [jnp.sum(jnp.where(eye, q[h][b:b + 1], 0.0), axis=1, keepdims=True) for h, b in items]
    k_col = [jnp.sum(jnp.where(eye, k[h][b:b + 1], 0.0), axis=1, keepdims=True) for h, b in items]
    qc = [jnp.sum(c * qq, axis=0, keepdims=True) for c, qq in zip(c0, q_col)]
    for i, (h, b) in enumerate(items):
        c_out_ref[b, h] = (w_st[b:b + 1, h:h + 1] * c0[i]
                           + (w_in[b:b + 1, h:h + 1] * k_col[i]) * vv[h][b:b + 1])
    for h, hs in enumerate(heads):
        qc_h = jnp.concatenate(qc[h * bb:(h + 1) * bb], axis=0)
        hv_scr[:, hs] = (w_st[:, h:h + 1] * qc_h + s[h] * vv[h]) * inv[h]
    hv = hv_scr[...]
    gnw = gnw_ref[...]
    skip = skip_ref[...]
    outs = []
    for h in range(M_HEADS):
        hs = slice(h * M_HEAD_DIM, (h + 1) * M_HEAD_DIM)
        outs.append(_head_norm_rows(hv[:, hs], M_GN_EPS) * gnw[:, hs] + skip[:, hs] * xc[:, hs])
    o = pm_ref[:, 2 * M_WIDTH:3 * M_WIDTH]
    y_ref[...] = _sigmoid(o) * jnp.concatenate(outs, axis=1)


def _mlstm_step(pm, conv0, c0, n0, m0, p, bb):
    nb = pm.shape[0]
    row = lambda n: _const_spec((1, n))
    c_spec = pl.BlockSpec((bb, M_HEADS, M_HEAD_DIM, M_HEAD_DIM), lambda i: (i, 0, 0, 0))
    n_spec = pl.BlockSpec((bb, M_WIDTH), lambda i: (i, 0))
    m_spec = pl.BlockSpec((bb, M_HEADS), lambda i: (i, 0))
    conv_spec = pl.BlockSpec((CONV_W - 1, bb, M_WIDTH), lambda i: (0, i, 0))
    return pl.pallas_call(
        functools.partial(_mlstm_step_body, bb=bb),
        grid=(nb // bb,),
        in_specs=[pl.BlockSpec((bb, M_COLS), lambda i: (i, 0)), conv_spec, c_spec, n_spec, m_spec,
                  _const_spec((CONV_W, M_WIDTH)), row(M_WIDTH),
                  _const_spec((M_HEADS, M_HEAD_DIM, M_HEAD_DIM)), _const_spec((M_HEADS, M_HEAD_DIM, M_HEAD_DIM)),
                  row(LANES), row(LANES), row(M_WIDTH), row(M_WIDTH)],
        out_specs=[pl.BlockSpec((bb, M_WIDTH), lambda i: (i, 0)), c_spec, n_spec, m_spec, conv_spec],
        out_shape=[jax.ShapeDtypeStruct((nb, M_WIDTH), F32),
                   jax.ShapeDtypeStruct(c0.shape, F32),
                   jax.ShapeDtypeStruct(n0.shape, F32),
                   jax.ShapeDtypeStruct(m0.shape, F32),
                   jax.ShapeDtypeStruct(conv0.shape, F32)],
        scratch_shapes=[pltpu.VMEM((bb, M_WIDTH), F32)],
        compiler_params=pltpu.CompilerParams(dimension_semantics=("arbitrary",), vmem_limit_bytes=VMEM_LIMIT),
        name="mlstm_step",
    )(pm, conv0, c0, n0, m0, p["conv_w"], p["conv_b"], p["wq"], p["wk"], p["i_b"], p["f_b"],
      p["gn_w"], p["skip"])


FFN_BLOCK = 1024


def _out_body(x_ref, pg_ref, yr_ref, ym_ref, rup_ref, mup_ref, wout_ref, gffn_ref, w1_ref, w2_ref, gfin_ref,
              y_ref):
    up_r = jnp.dot(yr_ref[...].astype(BF16), rup_ref[...], preferred_element_type=F32)
    up_m = jnp.dot(ym_ref[...].astype(BF16), mup_ref[...], preferred_element_type=F32)
    merged = _sigmoid(pg_ref[:, 0:D_MODEL]) * up_r + _sigmoid(pg_ref[:, D_MODEL:]) * up_m
    x1 = x_ref[...] + jnp.dot(merged.astype(BF16), wout_ref[...], preferred_element_type=F32)
    hn = _rms(x1, gffn_ref[...]).astype(BF16)
    x2 = x1
    for j in range(D_FF // FFN_BLOCK):
        cols = slice(j * FFN_BLOCK, (j + 1) * FFN_BLOCK)
        hid = jnp.maximum(jnp.dot(hn, w1_ref[:, cols], preferred_element_type=F32), 0.0)
        x2 = x2 + jnp.dot((hid * hid).astype(BF16), w2_ref[cols, :], preferred_element_type=F32)
    y_ref[...] = _rms(x2, gfin_ref[...])


def _out(x2, pg, yr, ym, p, tm):
    m = x2.shape[0]
    tile = lambda n: pl.BlockSpec((tm, n), lambda i: (i, 0))
    return pl.pallas_call(
        _out_body,
        grid=(m // tm,),
        in_specs=[tile(D_MODEL), tile(GATE_COLS), tile(R_WIDTH), tile(M_WIDTH),
                  _const_spec((R_WIDTH, D_MODEL)), _const_spec((M_WIDTH, D_MODEL)),
                  _const_spec((D_MODEL, D_MODEL)), _const_spec((1, D_MODEL)),
                  _const_spec((D_MODEL, D_FF)), _const_spec((D_FF, D_MODEL)), _const_spec((1, D_MODEL))],
        out_specs=tile(D_MODEL),
        out_shape=jax.ShapeDtypeStruct((m, D_MODEL), F32),
        compiler_params=pltpu.CompilerParams(dimension_semantics=("arbitrary",), vmem_limit_bytes=VMEM_LIMIT),
        name="merge_ffn",
    )(x2, pg, yr, ym, p["r_up"], p["m_up"], p["w_out"], p["g_ffn"], p["w1"], p["w2"], p["g_fin"])


PROJ_TM = 512
RWKV_TT = 512
MLSTM_TT = 512
OUT_TM = 512
STEP_BB = 16


def _head_segments():
    head = jnp.arange(GROUP_W) // R_HEAD_DIM
    return (head[:, None] == head[None, :]).astype(BF16)


def _pad_lanes(v, start):
    out = jnp.zeros((1, LANES), F32)
    return lax.dynamic_update_slice(out, v.reshape(1, -1), (0, start))


def kernel(x_prompt, x_sample, state_rwkv_shift, state_rwkv_wkv, state_mlstm_C, state_mlstm_n, state_mlstm_m, state_mlstm_conv, norm_mix_g, w_in, r_mu, r_w0, r_w2, r_a0, r_a2, r_g2, r_kk, r_ka, r_rk, r_gn_w, r_gn_b, r_up, m_conv_w, m_conv_b, m_wq, m_wk, m_i_b, m_f_b, m_gn_w, m_skip, m_up, gate_b, w_out, norm_ffn_g, ffn_w1, ffn_w2, norm_final_g):
    nbp, seq, _ = x_prompt.shape
    nbs = x_sample.shape[0]
    w = w_in[0]
    wt = jnp.transpose(w)
    w_rm = wt[:RM_COLS].astype(BF16)
    w_g = wt[RM_COLS:].astype(BF16)
    g_mix = norm_mix_g[0].reshape(1, D_MODEL)
    gb = gate_b[0].reshape(1, GATE_COLS)
    rp = dict(mu=r_mu[0].reshape(1, -1), w0=r_w0[0].reshape(1, -1), w2=r_w2[0].astype(BF16),
              a0=r_a0[0].reshape(1, -1), a2=r_a2[0].astype(BF16), g2=r_g2[0].astype(BF16),
              kk=r_kk[0].reshape(1, -1), ka=r_ka[0].reshape(1, -1), rk=r_rk[0].reshape(1, -1),
              gn_w=r_gn_w[0].reshape(1, -1), gn_b=r_gn_b[0].reshape(1, -1), seg=_head_segments())
    mp = dict(conv_w=m_conv_w[0], conv_b=m_conv_b[0].reshape(1, -1), wq=m_wq[0].astype(BF16),
              wk=m_wk[0].astype(BF16), i_b=_pad_lanes(m_i_b[0], 0), f_b=_pad_lanes(m_f_b[0], M_HEADS),
              gn_w=m_gn_w[0].reshape(1, -1), skip=m_skip[0].reshape(1, -1))
    op = dict(r_up=r_up[0].astype(BF16), m_up=m_up[0].astype(BF16), w_out=w_out[0].astype(BF16),
              g_ffn=norm_ffn_g[0].reshape(1, -1), w1=ffn_w1[0].astype(BF16), w2=ffn_w2[0].astype(BF16),
              g_fin=norm_final_g.reshape(1, -1))

    xp = x_prompt.reshape(nbp * seq, D_MODEL)
    pr, pm, pg = _proj(xp, g_mix, w_rm, w_g, gb, PROJ_TM)
    y_r, wkv_p = _rwkv_seq(pr.reshape(SEQ_WAYS, -1, R_SHIFT_COLS), rp, nbp, seq // RWKV_TT, RWKV_TT)
    y_r = y_r.reshape(nbp * seq, R_WIDTH)
    wkv_p = wkv_p.reshape(nbp, R_HEADS, R_HEAD_DIM, R_HEAD_DIM)
    y_m, c_p, n_p, m_p = _mlstm_seq(pm, mp, nbp, seq // MLSTM_TT, MLSTM_TT)
    y_p = _out(xp, pg, y_r, y_m, op, OUT_TM).reshape(nbp, seq, D_MODEL)
    conv_p = pm.reshape(nbp, seq, M_COLS)[:, seq - (CONV_W - 1):, :M_WIDTH]

    xn_s, xn_last, pr_s, pm_s, pg_s, prev_s = _proj_step(
        x_sample[:, 0], state_rwkv_shift[0], x_prompt[:, seq - 1], g_mix, w_rm, w_g, gb)
    yr_s, wkv_t = _rwkv_step(pr_s, prev_s, jnp.transpose(state_rwkv_wkv[0], (1, 2, 3, 0)), rp)
    wkv_s = jnp.transpose(wkv_t, (3, 0, 1, 2))
    conv0 = jnp.transpose(state_mlstm_conv[0], (1, 0, 2))
    ym_s, c_s, n_s, m_s, conv_s = _mlstm_step(pm_s, conv0, state_mlstm_C[0],
                                              state_mlstm_n[0].reshape(nbs, M_WIDTH),
                                              state_mlstm_m[0], mp, STEP_BB)
    y_s = _out(x_sample[:, 0], pg_s, yr_s, ym_s, op, nbs).reshape(nbs, 1, D_MODEL)

    return (y_p, y_s,
            xn_last[None], wkv_p[None], c_p[None], n_p[:, :, 0, :][None], m_p[:, :, 0, 0][None], conv_p[None],
            xn_s[None], wkv_s[None], c_s[None], n_s.reshape(nbs, M_HEADS, M_HEAD_DIM)[None], m_s[None],
            jnp.transpose(conv_s, (1, 0, 2))[None])
```

```python
import functools
import math

import jax
import jax.numpy as jnp
from jax import lax
from jax.experimental import pallas as pl
from jax.experimental.pallas import tpu as pltpu

F32 = jnp.float32
BF16 = jnp.bfloat16

D_MODEL = 1024
R_HEADS = 8
R_HEAD_DIM = 64
R_WIDTH = R_HEADS * R_HEAD_DIM
GROUP_HEADS = 4
R_GROUPS = R_HEADS // GROUP_HEADS
GROUP_W = GROUP_HEADS * R_HEAD_DIM
W_LORA = 64
A_LORA = 64
G_LORA = 128
R_GN_EPS = 64e-5
M_HEADS = 4
M_HEAD_DIM = 128
M_WIDTH = M_HEADS * M_HEAD_DIM
CONV_W = 4
M_GN_EPS = 1e-5
D_FF = 4 * D_MODEL
RMS_EPS = 1e-6
R_SHIFT_COLS = 3 * R_WIDTH + W_LORA + A_LORA + G_LORA
M_COLS = 3 * M_WIDTH + 2 * M_HEADS
GATE_COLS = 2 * D_MODEL

LANES = 128
SUBLANES = 8
CHUNK = 64
RM_COLS = R_SHIFT_COLS + M_COLS
VMEM_LIMIT = 56 * 1024 * 1024


def _mm(a, b, dims="nn"):
    ca = 1 if dims[0] == "n" else 0
    cb = 0 if dims[1] == "n" else 1
    dn = (((ca,), (cb,)), ((), ()))
    return lax.dot_general(a.astype(BF16), b.astype(BF16), dn, preferred_element_type=F32)


def _mm_mask_lhs(mask, b):
    b_hi = b.astype(BF16)
    return _mm(mask, b_hi) + _mm(mask, b - b_hi.astype(F32))


def _mm_mask_rhs(a, mask):
    a_hi = a.astype(BF16)
    return _mm(a_hi, mask) + _mm(a - a_hi.astype(F32), mask)


def _rms(x, g):
    return x * lax.rsqrt(jnp.mean(x * x, axis=-1, keepdims=True) + RMS_EPS) * g


def _sigmoid(x):
    return 1.0 / (1.0 + jnp.exp(-x))


def _log_sigmoid(x):
    return jnp.minimum(x, 0.0) - jnp.log(1.0 + jnp.exp(-jnp.abs(x)))


def _iota2(shape, axis):
    return lax.broadcasted_iota(jnp.int32, shape, axis)


def _const_spec(shape):
    nd = len(shape)
    return pl.BlockSpec(shape, lambda *_: (0,) * nd, pipeline_mode=pl.Buffered(1))


def _proj_cols(xb, wt_ref, lo, hi):
    return lax.dot_general(xb, wt_ref[lo:hi, :], (((1,), (1,)), ((), ())), preferred_element_type=F32)


RM_CHUNK = 512
RM_CHUNKS = -(-RM_COLS // RM_CHUNK)
G_CHUNKS = RM_CHUNKS + 1
G_CHUNK = GATE_COLS // G_CHUNKS


def _proj_body(x_ref, xs_ref, sh_ref, xl_ref, g_ref, wa_ref, wb_ref, gb_ref,
               pr_ref, pm_ref, pg_ref, xns_ref, xnl_ref, prs_ref, pms_ref, pgs_ref, prevs_ref,
               wrm_scr, wg_scr):
    s = pl.program_id(0)

    @pl.when(s < RM_CHUNKS)
    def _():
        wrm_scr[pl.ds(pl.multiple_of(s * RM_CHUNK, RM_CHUNK), RM_CHUNK), :] = wa_ref[...].astype(BF16)

    @pl.when(s < G_CHUNKS)
    def _():
        wg_scr[pl.ds(pl.multiple_of(s * G_CHUNK, G_CHUNK), G_CHUNK), :] = wb_ref[...].astype(BF16)

    def project(xb, pr, pm, pg):
        pr[...] = _proj_cols(xb, wrm_scr, 0, R_SHIFT_COLS)
        pm[...] = _proj_cols(xb, wrm_scr, R_SHIFT_COLS, RM_COLS)
        pg[...] = _proj_cols(xb, wg_scr, 0, GATE_COLS) + gb_ref[...]

    @pl.when(s >= G_CHUNKS)
    def _():
        project(_rms(x_ref[...], g_ref[...]).astype(BF16), pr_ref, pm_ref, pg_ref)

    @pl.when(s == pl.num_programs(0) - 1)
    def _():
        xn = _rms(xs_ref[...], g_ref[...])
        xns_ref[...] = xn
        xnl_ref[...] = _rms(xl_ref[...], g_ref[...])
        project(xn.astype(BF16), prs_ref, pms_ref, pgs_ref)
        prevs_ref[...] = _proj_cols(sh_ref[...].astype(BF16), wrm_scr, 0, R_SHIFT_COLS)


def _proj(x2, xs, shift0, xlast, g, wt, gate_b, tm):
    m, ms, ml = x2.shape[0], xs.shape[0], xlast.shape[0]
    tile = lambda n: pl.BlockSpec((tm, n), lambda s: (jnp.maximum(s - G_CHUNKS, 0), 0))
    whole = lambda rows, cols: pl.BlockSpec((rows, cols), lambda s: (0, 0))
    f32 = lambda rows, cols: jax.ShapeDtypeStruct((rows, cols), F32)
    return pl.pallas_call(
        _proj_body,
        grid=(G_CHUNKS + m // tm,),
        in_specs=[tile(D_MODEL), _const_spec((ms, D_MODEL)), _const_spec((ms, D_MODEL)), _const_spec((ml, D_MODEL)),
                  _const_spec((1, D_MODEL)),
                  pl.BlockSpec((RM_CHUNK, D_MODEL), lambda s: (jnp.minimum(s, RM_CHUNKS - 1), 0)),
                  pl.BlockSpec((pl.Element(G_CHUNK), pl.Element(D_MODEL)),
                               lambda s: (pl.multiple_of(RM_COLS + G_CHUNK * jnp.minimum(s, G_CHUNKS - 1), SUBLANES),
                                          0)),
                  _const_spec((1, GATE_COLS))],
        out_specs=[tile(R_SHIFT_COLS), tile(M_COLS), tile(GATE_COLS),
                   whole(ms, D_MODEL), whole(ml, D_MODEL), whole(ms, R_SHIFT_COLS), whole(ms, M_COLS),
                   whole(ms, GATE_COLS), whole(ms, R_SHIFT_COLS)],
        out_shape=[f32(m, R_SHIFT_COLS), f32(m, M_COLS), f32(m, GATE_COLS),
                   f32(ms, D_MODEL), f32(ml, D_MODEL), f32(ms, R_SHIFT_COLS), f32(ms, M_COLS),
                   f32(ms, GATE_COLS), f32(ms, R_SHIFT_COLS)],
        scratch_shapes=[pltpu.VMEM((RM_CHUNKS * RM_CHUNK, D_MODEL), BF16), pltpu.VMEM((GATE_COLS, D_MODEL), BF16)],
        compiler_params=pltpu.CompilerParams(dimension_semantics=("arbitrary",), vmem_limit_bytes=VMEM_LIMIT),
        name="proj",
    )(x2, xs, shift0, xlast, g, wt, wt, gate_b)


LOG_DECAY_SCALE = -math.exp(-0.5)


def _rwkv_mix(pr, prev, mu, w0, w2, a0, a2, g2, r_kk, r_ka):
    mixed = pr + (prev - pr) * mu
    r = mixed[:, 0:R_WIDTH]
    k = mixed[:, R_WIDTH:2 * R_WIDTH]
    v = mixed[:, 2 * R_WIDTH:3 * R_WIDTH]
    xw = mixed[:, 3 * R_WIDTH:3 * R_WIDTH + W_LORA]
    xa = mixed[:, 3 * R_WIDTH + W_LORA:3 * R_WIDTH + W_LORA + A_LORA]
    xg = mixed[:, 3 * R_WIDTH + W_LORA + A_LORA:]
    w = w0 + _mm(jnp.tanh(xw), w2)
    logw = LOG_DECAY_SCALE * _sigmoid(w)
    a = _sigmoid(a0 + _mm(xa, a2))
    g = _mm(_sigmoid(xg), g2)
    kk0 = k * r_kk
    k2 = k * (1.0 + (a - 1.0) * r_ka)
    return r, k2, v, logw, a, g, kk0


def _rwkv_tile(pr_ref, mu_ref, w0_ref, w2_ref, a0_ref, a2_ref, g2_ref, kk_ref, ka_ref, rk_ref,
               gnw_ref, gnb_ref, seg_ref, s_scr, carry_scr, tt):
    pr = pr_ref[...]
    row = _iota2(pr.shape, 0)
    prev = jnp.where(row == 0, carry_scr[...], pltpu.roll(pr, 1, axis=0))
    carry_scr[...] = pr[tt - 1:tt, :]

    r, k2, v, logw, a, g, kk0 = _rwkv_mix(pr, prev, mu_ref[...], w0_ref[...], w2_ref[...], a0_ref[...],
                                          a2_ref[...], g2_ref[...], kk_ref[...], ka_ref[...])
    yield
    ri = _iota2((CHUNK, GROUP_W), 0)
    ci = _iota2((CHUNK, GROUP_W), 1)
    src = ci % R_HEAD_DIM
    lane_head = ci // R_HEAD_DIM
    low_incl = src <= ri
    low_strict = src < ri
    eye = (src == ri).astype(F32)
    rr = _iota2((GROUP_W, GROUP_W), 0)
    cc = _iota2((GROUP_W, GROUP_W), 1)
    same_head = (rr // R_HEAD_DIM) == (cc // R_HEAD_DIM)
    eye_bd = (rr == cc).astype(F32)
    seg = seg_ref[...]

    def seg_sum(x, mm=_mm):
        return jnp.concatenate([mm(x[:, q * GROUP_W:(q + 1) * GROUP_W], seg) for q in range(R_GROUPS)], axis=1)
    tril = (_iota2((CHUNK, CHUNK), 1) <= _iota2((CHUNK, CHUNK), 0)).astype(F32)

    nc = tt // CHUNK
    cum = jnp.concatenate([_mm_mask_lhs(tril, logw[c * CHUNK:(c + 1) * CHUNK]) for c in range(nc)], axis=0)
    yield
    cum_last = jnp.concatenate(
        [jnp.broadcast_to(cum[(c + 1) * CHUNK - 1:(c + 1) * CHUNK, :], (CHUNK, R_WIDTH)) for c in range(nc)], axis=0)
    d_inv = jnp.exp(-cum)
    d_end = jnp.exp(cum_last - cum)
    d_last = jnp.exp(cum_last)
    kk = kk0 / jnp.maximum(jnp.sqrt(seg_sum(kk0 * kk0)), 1e-12)
    yield
    kka = kk * a
    at = kk * jnp.exp(cum - logw)
    rt = r * jnp.exp(cum)
    bt = kka * d_inv
    kt = k2 * d_inv
    bh = kka * d_end
    kh = k2 * d_end

    items = [(c, p) for c in range(nc) for p in range(R_GROUPS)]
    blk = lambda x, c, p: x[c * CHUNK:(c + 1) * CHUNK, p * GROUP_W:(p + 1) * GROUP_W]
    each = lambda f, *ls: [f(*xs) for xs in zip(*ls)]
    cat0 = lambda *xs: jnp.concatenate(xs, axis=0)
    cat1 = lambda *xs: jnp.concatenate(xs, axis=1)

    def bd(x):
        return cat0(*[jnp.where(lane_head == j, x, 0.0) for j in range(GROUP_HEADS)])

    a_ = [blk(at, c, p) for c, p in items]
    b_ = [blk(bt, c, p) for c, p in items]
    bh_ = [blk(bh, c, p) for c, p in items]
    r_ = [blk(rt, c, p) for c, p in items]
    k_ = [blk(kt, c, p) for c, p in items]
    kh_ = [blk(kh, c, p) for c, p in items]
    v_ = [blk(v, c, p) for c, p in items]
    vbd = [bd(x) for x in v_]
    pq = each(lambda a1, r1, b1, k1: _mm(cat0(a1, r1), cat0(bd(b1), bd(k1)), "nt"), a_, r_, b_, k_)
    yield
    t_ab = [jnp.where(low_strict, x[:CHUNK, :GROUP_W], 0.0) for x in pq]
    t_ak = [jnp.where(low_strict, x[:CHUNK, GROUP_W:], 0.0) for x in pq]
    m_rb = [jnp.where(low_incl, x[CHUNK:, :GROUP_W], 0.0) for x in pq]
    m_rk = [jnp.where(low_incl, x[CHUNK:, GROUP_W:], 0.0) for x in pq]
    q_pow = each(lambda t: _mm(t, bd(t)), t_ab)
    yield
    x_inv = [eye - t for t in t_ab]
    for _ in range(4):
        z = each(lambda x, q: _mm(cat0(x, q), bd(q)), x_inv, q_pow)
        x_inv = [x + zz[:CHUNK] for x, zz in zip(x_inv, z)]
        q_pow = [zz[CHUNK:] for zz in z]
        yield
    x_inv = each(lambda x, q: x + _mm(x, bd(q)), x_inv, q_pow)
    yield
    tv = each(_mm, t_ak, vbd)
    yield
    xw = each(lambda x, a1, t: _mm(x, cat1(bd(a1), bd(t))), x_inv, a_, tv)
    yield
    mw = each(lambda m1, x: _mm(m1, cat1(bd(x[:, :GROUP_W]), bd(x[:, GROUP_W:]))), m_rb, xw)
    yield
    q_eff = [r1 - m1[:, :GROUP_W] for r1, m1 in zip(r_, mw)]
    y_loc = each(lambda m1, vb, m2: _mm(m1, vb) - m2[:, GROUP_W:], m_rk, vbd, mw)
    yield
    g_bd = [eye_bd * blk(d_last, c, p)[0:1] - jnp.where(same_head, _mm(x[:, :GROUP_W], b1, "tn"), 0.0)
            for (c, p), x, b1 in zip(items, xw, bh_)]
    yield
    h_full = each(lambda vv, x, k1, b1: _mm(cat0(vv, x[:, GROUP_W:]), cat0(k1, -b1), "tn"), v_, xw, kh_, bh_)
    yield
    h_pair = []
    for x in h_full:
        acc = x[:CHUNK]
        for j in range(1, GROUP_HEADS):
            acc = jnp.where(lane_head == j, x[j * CHUNK:(j + 1) * CHUNK], acc)
        h_pair.append(acc)

    state = [s_scr[p] for p in range(R_GROUPS)]
    ys = []
    for i, (c, p) in enumerate(items):
        ys.append(_mm(q_eff[i], bd(state[p]), "nt") + y_loc[i])
        state[p] = _mm(state[p], g_bd[i]) + h_pair[i]
        if p == R_GROUPS - 1:
            yield
    for p in range(R_GROUPS):
        s_scr[p] = state[p]

    y_all = cat0(*[cat1(*ys[c * R_GROUPS:(c + 1) * R_GROUPS]) for c in range(nc)])
    yc = y_all - seg_sum(y_all) * (1.0 / R_HEAD_DIM)
    yield
    var = seg_sum(yc * yc) * (1.0 / R_HEAD_DIM)
    yield
    bonus = seg_sum(r * k2 * rk_ref[...], _mm_mask_rhs)
    return (yc * lax.rsqrt(var + R_GN_EPS) * gnw_ref[...] + gnb_ref[...] + bonus * v) * g


N_RWKV_PARAMS = 12


SEQ_WAYS = 2
SEQ_LAG = 4


def _staggered(gens, lag):
    out = [None] * len(gens)
    live = set(range(len(gens)))
    step = 0
    while live:
        for j in sorted(live):
            if step >= j * lag:
                try:
                    next(gens[j])
                except StopIteration as stop:
                    out[j] = stop.value
                    live.discard(j)
        step += 1
    return out


def _rwkv_seq_body(*refs, tt):
    pr_ref, params = refs[0], refs[1:1 + N_RWKV_PARAMS]
    y_ref, s_out_ref, s_scr, carry_scr = refs[1 + N_RWKV_PARAMS:]
    t_idx = pl.program_id(1)

    @pl.when(t_idx == 0)
    def _():
        s_scr[...] = jnp.zeros_like(s_scr)
        carry_scr[...] = jnp.zeros_like(carry_scr)

    ys = _staggered([_rwkv_tile(pr_ref.at[w], *params, s_scr.at[w], carry_scr.at[w], tt)
                     for w in range(SEQ_WAYS)], SEQ_LAG)
    for w in range(SEQ_WAYS):
        y_ref[w] = ys[w]

    @pl.when(t_idx == pl.num_programs(1) - 1)
    def _():
        for w in range(SEQ_WAYS):
            for p in range(R_GROUPS):
                for j in range(GROUP_HEADS):
                    s_out_ref[w, 0, GROUP_HEADS * p + j] = s_scr[w, p][:, j * R_HEAD_DIM:(j + 1) * R_HEAD_DIM]


def _rwkv_seq(pr, p, nb, nt, tt):
    m = pr.shape[1]
    row = lambda n: _const_spec((1, n))
    return pl.pallas_call(
        functools.partial(_rwkv_seq_body, tt=tt),
        grid=(nb // SEQ_WAYS, nt),
        in_specs=[pl.BlockSpec((SEQ_WAYS, tt, R_SHIFT_COLS), lambda b, t: (0, b * nt + t, 0)),
                  row(R_SHIFT_COLS), row(R_WIDTH), _const_spec((W_LORA, R_WIDTH)), row(R_WIDTH),
                  _const_spec((A_LORA, R_WIDTH)), _const_spec((G_LORA, R_WIDTH)), row(R_WIDTH), row(R_WIDTH),
                  row(R_WIDTH), row(R_WIDTH), row(R_WIDTH), _const_spec((GROUP_W, GROUP_W))],
        out_specs=[pl.BlockSpec((SEQ_WAYS, tt, R_WIDTH), lambda b, t: (0, b * nt + t, 0)),
                   pl.BlockSpec((SEQ_WAYS, 1, R_HEADS, R_HEAD_DIM, R_HEAD_DIM), lambda b, t: (0, b, 0, 0, 0))],
        out_shape=[jax.ShapeDtypeStruct((SEQ_WAYS, m, R_WIDTH), F32),
                   jax.ShapeDtypeStruct((SEQ_WAYS, nb // SEQ_WAYS, R_HEADS, R_HEAD_DIM, R_HEAD_DIM), F32)],
        scratch_shapes=[pltpu.VMEM((SEQ_WAYS, R_GROUPS, R_HEAD_DIM, GROUP_W), F32),
                        pltpu.VMEM((SEQ_WAYS, 1, R_SHIFT_COLS), F32)],
        compiler_params=pltpu.CompilerParams(dimension_semantics=("arbitrary", "arbitrary"),
                                             vmem_limit_bytes=VMEM_LIMIT),
        name="rwkv_seq",
    )(pr, p["mu"], p["w0"], p["w2"], p["a0"], p["a2"], p["g2"], p["kk"], p["ka"], p["rk"],
      p["gn_w"], p["gn_b"], p["seg"])


def _rwkv_step_body(pr_ref, prev_ref, s_ref, mu_ref, w0_ref, w2_ref, a0_ref, a2_ref, g2_ref, kk_ref, ka_ref,
                    rk_ref, gnw_ref, gnb_ref, seg_ref, y_ref, s_out_ref,
                    kk_scr, kka_scr, dec_scr, k_scr, v_scr, r_scr, yt_scr, g_scr, vrow_scr, bonus_scr):
    h = pl.program_id(0)
    seg = seg_ref[...]

    def seg_sum(x, mm=_mm):
        return jnp.concatenate([mm(x[:, q * GROUP_W:(q + 1) * GROUP_W], seg) for q in range(R_GROUPS)], axis=1)

    @pl.when(h == 0)
    def _():
        r, k2, v, logw, a, g, kk0 = _rwkv_mix(pr_ref[...], prev_ref[...], mu_ref[...], w0_ref[...], w2_ref[...],
                                              a0_ref[...], a2_ref[...], g2_ref[...], kk_ref[...], ka_ref[...])
        kk = kk0 / jnp.maximum(jnp.sqrt(seg_sum(kk0 * kk0)), 1e-12)
        kk_scr[...] = kk.T
        kka_scr[...] = (kk * a).T
        dec_scr[...] = jnp.exp(logw).T
        k_scr[...] = k2.T
        v_scr[...] = v.T
        r_scr[...] = r.T
        g_scr[...] = g
        vrow_scr[...] = v
        bonus_scr[...] = seg_sum(r * k2 * rk_ref[...], _mm_mask_rhs)

    base = pl.multiple_of(h * R_HEAD_DIM, R_HEAD_DIM)
    hs = pl.ds(base, R_HEAD_DIM)
    kk_t, kka_t, dec_t, k_t, r_t = kk_scr[hs, :], kka_scr[hs, :], dec_scr[hs, :], k_scr[hs, :], r_scr[hs, :]
    for i in range(R_HEAD_DIM):
        s0 = s_ref[0, i]
        s_kk = jnp.sum(s0 * kk_t, axis=0, keepdims=True)
        s1 = s0 * dec_t - s_kk * kka_t + v_scr[pl.ds(base + i, 1), :] * k_t
        s_out_ref[0, i] = s1
        yt_scr[pl.ds(base + i, 1), :] = jnp.sum(s1 * r_t, axis=0, keepdims=True)

    @pl.when(h == pl.num_programs(0) - 1)
    def _():
        y = yt_scr[...].T
        yc = y - seg_sum(y) * (1.0 / R_HEAD_DIM)
        var = seg_sum(yc * yc) * (1.0 / R_HEAD_DIM)
        y_ref[...] = (yc * lax.rsqrt(var + R_GN_EPS) * gnw_ref[...] + gnb_ref[...]
                      + bonus_scr[...] * vrow_scr[...]) * g_scr[...]


def _rwkv_step(pr, prev, s0_t, p):
    nb = pr.shape[0]
    row = lambda n: _const_spec((1, n))
    state_spec = pl.BlockSpec((1, R_HEAD_DIM, R_HEAD_DIM, nb), lambda h: (h, 0, 0, 0))
    tr = pltpu.VMEM((R_WIDTH, nb), F32)
    rw = pltpu.VMEM((nb, R_WIDTH), F32)
    return pl.pallas_call(
        _rwkv_step_body,
        grid=(R_HEADS,),
        in_specs=[_const_spec((nb, R_SHIFT_COLS)), _const_spec((nb, R_SHIFT_COLS)), state_spec,
                  row(R_SHIFT_COLS), row(R_WIDTH), _const_spec((W_LORA, R_WIDTH)), row(R_WIDTH),
                  _const_spec((A_LORA, R_WIDTH)), _const_spec((G_LORA, R_WIDTH)), row(R_WIDTH), row(R_WIDTH),
                  row(R_WIDTH), row(R_WIDTH), row(R_WIDTH), _const_spec((GROUP_W, GROUP_W))],
        out_specs=[pl.BlockSpec((nb, R_WIDTH), lambda h: (0, 0)), state_spec],
        out_shape=[jax.ShapeDtypeStruct((nb, R_WIDTH), F32),
                   jax.ShapeDtypeStruct(s0_t.shape, F32)],
        scratch_shapes=[tr, tr, tr, tr, tr, tr, tr, rw, rw, rw],
        compiler_params=pltpu.CompilerParams(dimension_semantics=("arbitrary",), vmem_limit_bytes=VMEM_LIMIT),
        name="rwkv_step",
    )(pr, prev, s0_t, p["mu"], p["w0"], p["w2"], p["a0"], p["a2"], p["g2"], p["kk"], p["ka"], p["rk"],
      p["gn_w"], p["gn_b"], p["seg"])


GATE_LANE0 = 3 * M_WIDTH


def _gate_lanes(pm_ref):
    gates = pm_ref[:, GATE_LANE0:GATE_LANE0 + 2 * M_HEADS]
    return jnp.concatenate([gates, jnp.zeros((gates.shape[0], LANES - 2 * M_HEADS), F32)], axis=1)


def _head_norm_rows(x, eps):
    xc = x - jnp.mean(x, axis=1, keepdims=True)
    var = jnp.mean(xc * xc, axis=1, keepdims=True)
    return xc * lax.rsqrt(var + eps)


def _cummax_rows(x):
    n = x.shape[0]
    row = _iota2(x.shape, 0)
    d = 1
    while d < n:
        if d < SUBLANES:
            shifted = jnp.where(row < d, -jnp.inf, pltpu.roll(x, d, axis=0))
        else:
            shifted = jnp.concatenate([jnp.full((d, x.shape[1]), -jnp.inf, x.dtype), x[:n - d]], axis=0)
        x = jnp.maximum(x, shifted)
        d *= 2
    return x


def _mlstm_seq_body(pm_ref, cw_ref, cb_ref, wq_ref, wk_ref, ib_ref, fb_ref, gnw_ref, skip_ref,
                    y_ref, c_out_ref, n_out_ref, m_out_ref, c_scr, n_scr, m_scr, carry_scr, *, tt):
    t_idx = pl.program_id(1)

    @pl.when(t_idx == 0)
    def _():
        c_scr[...] = jnp.zeros_like(c_scr)
        n_scr[...] = jnp.zeros_like(n_scr)
        m_scr[...] = jnp.zeros_like(m_scr)
        carry_scr[...] = jnp.zeros_like(carry_scr)

    xm = pm_ref[:, 0:M_WIDTH]
    carry = carry_scr[...]
    carry_scr[...] = xm[tt - SUBLANES:tt, :]
    row8 = _iota2((SUBLANES, M_WIDTH), 0)
    cw = cw_ref[...]
    xc = cb_ref[...] + xm * cw[CONV_W - 1:CONV_W, :]
    for s in range(1, CONV_W):
        rolled = pltpu.roll(xm, s, axis=0)
        top = jnp.where(row8 < s, pltpu.roll(carry, s, axis=0), rolled[0:SUBLANES])
        shifted = jnp.concatenate([top, rolled[SUBLANES:]], axis=0)
        xc = xc + shifted * cw[CONV_W - 1 - s:CONV_W - s, :]
    xc = xc * _sigmoid(xc)

    gt = _gate_lanes(pm_ref)
    li_all = gt + ib_ref[...]
    lf_all = _log_sigmoid(gt + fb_ref[...])

    ri = _iota2((CHUNK, CHUNK), 0)
    ci = _iota2((CHUNK, CHUNK), 1)
    causal = ci <= ri
    tril = causal.astype(F32)
    gnw = gnw_ref[...]
    skip = skip_ref[...]

    ones_cl = jnp.ones((CHUNK, LANES), BF16)
    ones_ll = jnp.ones((LANES, LANES), BF16)
    nc = tt // CHUNK
    rows = lambda c: slice(c * CHUNK, (c + 1) * CHUNK)
    lanes = lambda h: slice(h * M_HEAD_DIM, (h + 1) * M_HEAD_DIM)

    bc_all = jnp.concatenate([_mm_mask_lhs(tril, lf_all[rows(c)]) for c in range(nc)], axis=0)
    bc_al = pltpu.roll(bc_all, LANES - M_HEADS, axis=1)
    u_all = li_all - bc_al
    cm_all = jnp.concatenate([_cummax_rows(u_all[rows(c)]) for c in range(nc)], axis=0)
    u_t = u_all.T

    items = [(c, h) for c in range(nc) for h in range(M_HEADS)]
    rep = lambda x, c, h: jnp.broadcast_to(x[rows(c), h:h + 1], (CHUNK, LANES))
    xc_b = [xc[rows(c), lanes(h)] for c, h in items]
    q = [_mm(x, wq_ref[h]) * (M_HEAD_DIM ** -0.5) for (c, h), x in zip(items, xc_b)]
    k = [_mm(x, wk_ref[h]) for (c, h), x in zip(items, xc_b)]
    vv = [pm_ref[rows(c), M_WIDTH + h * M_HEAD_DIM:M_WIDTH + (h + 1) * M_HEAD_DIM] for c, h in items]
    cm_r = [rep(cm_all, c, h) for c, h in items]
    bc_r = [rep(bc_al, c, h) for c, h in items]
    u_r = [rep(u_all, c, h) for c, h in items]

    m_run = [m_scr[h][0:1, 0:1] for h in range(M_HEADS)]
    m_prev, m_cap, m_end = [], [], []
    for i, (c, h) in enumerate(items):
        m_prev.append(m_run[h])
        m_cap.append(jnp.maximum(m_run[h], cm_r[i]))
        m_end.append(m_cap[i][CHUNK - 1:CHUNK, 0:1])
        m_run[h] = bc_r[i][CHUNK - 1:CHUNK, 0:1] + m_end[i]

    w_st = [jnp.exp(mp - mc) for mp, mc in zip(m_prev, m_cap)]
    floor = [jnp.exp(-(b + mc)) for b, mc in zip(bc_r, m_cap)]
    w_in = [jnp.exp(jnp.where(causal, u_t[h:h + 1, rows(c)] - mc[:, :CHUNK], -jnp.inf))
            for (c, h), mc in zip(items, m_cap)]
    s = [_mm(qq, kk, "nt") * w for qq, kk, w in zip(q, k, w_in)]
    sv = [_mm(ss, v1) for ss, v1 in zip(s, vv)]
    s_sum = [_mm(ss, ones_cl) for ss in s]
    we = [jnp.exp(u - me) for u, me in zip(u_r, m_end)]
    ge = [jnp.exp(mp - me) for mp, me in zip(m_prev, m_end)]
    kv = [_mm(kk, w1 * v1, "tn") for kk, w1, v1 in zip(k, we, vv)]
    k_sum = [jnp.sum(w1 * kk, axis=0, keepdims=True) for w1, kk in zip(we, k)]

    c_run = [c_scr[h] for h in range(M_HEADS)]
    n_run = [n_scr[h][0:1, :] for h in range(M_HEADS)]
    num, den = [], []
    for i, (c, h) in enumerate(items):
        num.append(w_st[i] * _mm(q[i], c_run[h]) + sv[i])
        den.append(w_st[i] * _mm(q[i] * n_run[h], ones_ll) + s_sum[i])
        c_run[h] = ge[i] * c_run[h] + kv[i]
        n_run[h] = ge[i] * n_run[h] + k_sum[i]
    for h in range(M_HEADS):
        c_scr[h] = c_run[h]
        n_scr[h] = jnp.broadcast_to(n_run[h], (SUBLANES, M_HEAD_DIM))
        m_scr[h] = jnp.broadcast_to(m_run[h], (SUBLANES, LANES))

    for i, (c, h) in enumerate(items):
        hh = num[i] / jnp.maximum(jnp.abs(den[i]), floor[i])
        hc = hh - _mm(hh, ones_ll) * (1.0 / M_HEAD_DIM)
        var = _mm(hc * hc, ones_ll) * (1.0 / M_HEAD_DIM)
        hn = hc * lax.rsqrt(var + M_GN_EPS) * gnw[:, lanes(h)] + skip[:, lanes(h)] * xc_b[i]
        o = pm_ref[rows(c), 2 * M_WIDTH + h * M_HEAD_DIM:2 * M_WIDTH + (h + 1) * M_HEAD_DIM]
        y_ref[rows(c), lanes(h)] = _sigmoid(o) * hn

    @pl.when(t_idx == pl.num_programs(1) - 1)
    def _():
        c_out_ref[0] = c_scr[...]
        n_out_ref[0] = n_scr[...]
        m_out_ref[0] = m_scr[...]


def _mlstm_seq(pm, p, nb, nt, tt):
    m = pm.shape[0]
    row = lambda n: _const_spec((1, n))
    return pl.pallas_call(
        functools.partial(_mlstm_seq_body, tt=tt),
        grid=(nb, nt),
        in_specs=[pl.BlockSpec((tt, M_COLS), lambda b, t: (b * nt + t, 0)),
                  _const_spec((CONV_W, M_WIDTH)), row(M_WIDTH),
                  _const_spec((M_HEADS, M_HEAD_DIM, M_HEAD_DIM)), _const_spec((M_HEADS, M_HEAD_DIM, M_HEAD_DIM)),
                  row(LANES), row(LANES), row(M_WIDTH), row(M_WIDTH)],
        out_specs=[pl.BlockSpec((tt, M_WIDTH), lambda b, t: (b * nt + t, 0)),
                   pl.BlockSpec((1, M_HEADS, M_HEAD_DIM, M_HEAD_DIM), lambda b, t: (b, 0, 0, 0)),
                   pl.BlockSpec((1, M_HEADS, SUBLANES, M_HEAD_DIM), lambda b, t: (b, 0, 0, 0)),
                   pl.BlockSpec((1, M_HEADS, SUBLANES, LANES), lambda b, t: (b, 0, 0, 0))],
        out_shape=[jax.ShapeDtypeStruct((m, M_WIDTH), F32),
                   jax.ShapeDtypeStruct((nb, M_HEADS, M_HEAD_DIM, M_HEAD_DIM), F32),
                   jax.ShapeDtypeStruct((nb, M_HEADS, SUBLANES, M_HEAD_DIM), F32),
                   jax.ShapeDtypeStruct((nb, M_HEADS, SUBLANES, LANES), F32)],
        scratch_shapes=[pltpu.VMEM((M_HEADS, M_HEAD_DIM, M_HEAD_DIM), F32),
                        pltpu.VMEM((M_HEADS, SUBLANES, M_HEAD_DIM), F32),
                        pltpu.VMEM((M_HEADS, SUBLANES, LANES), F32),
                        pltpu.VMEM((SUBLANES, M_WIDTH), F32)],
        compiler_params=pltpu.CompilerParams(dimension_semantics=("arbitrary", "arbitrary"),
                                             vmem_limit_bytes=VMEM_LIMIT),
        name="mlstm_seq",
    )(pm, p["conv_w"], p["conv_b"], p["wq"], p["wk"], p["i_b"], p["f_b"], p["gn_w"], p["skip"])


def _mlstm_step_body(pm_ref, conv_ref, c_ref, n_ref, m_ref, cw_ref, cb_ref, wq_ref, wk_ref, ib_ref, fb_ref,
                     gnw_ref, skip_ref, y_ref, c_out_ref, n_out_ref, m_out_ref, conv_out_ref, hv_scr, *, bb):
    xm = pm_ref[:, 0:M_WIDTH]
    cw = cw_ref[...]
    xc = cb_ref[...] + xm * cw[CONV_W - 1:CONV_W, :]
    for j in range(CONV_W - 1):
        xc = xc + conv_ref[j] * cw[j:j + 1, :]
    xc = xc * _sigmoid(xc)
    for j in range(CONV_W - 2):
        conv_out_ref[j] = conv_ref[j + 1]
    conv_out_ref[CONV_W - 2] = xm

    gt = _gate_lanes(pm_ref)
    li = (gt + ib_ref[...])[:, 0:M_HEADS]
    lf = _log_sigmoid(gt + fb_ref[...])[:, M_HEADS:2 * M_HEADS]
    m_prev = m_ref[...]
    g_st = lf + m_prev
    m_t = jnp.maximum(g_st, (lf - lf) + li)
    w_in = jnp.exp(((lf - lf) + li) - m_t)
    w_st = jnp.exp(g_st - m_t)
    floor = jnp.exp(-m_t)
    m_out_ref[...] = m_t

    ri = _iota2((M_HEAD_DIM, M_HEAD_DIM), 0)
    ci = _iota2((M_HEAD_DIM, M_HEAD_DIM), 1)
    eye = ri == ci
    heads = [slice(h * M_HEAD_DIM, (h + 1) * M_HEAD_DIM) for h in range(M_HEADS)]
    q = [_mm(xc[:, hs], wq_ref[h]) * (M_HEAD_DIM ** -0.5) for h, hs in enumerate(heads)]
    k = [_mm(xc[:, hs], wk_ref[h]) for h, hs in enumerate(heads)]
    vv = [pm_ref[:, M_WIDTH + h * M_HEAD_DIM:M_WIDTH + (h + 1) * M_HEAD_DIM] for h in range(M_HEADS)]
    s, inv = [], []
    for h, hs in enumerate(heads):
        n0 = n_ref[:, hs]
        qk = jnp.sum(q[h] * k[h], axis=1, keepdims=True)
        qn = jnp.sum(q[h] * n0, axis=1, keepdims=True)
        s.append(qk * w_in[:, h:h + 1])
        den = w_st[:, h:h + 1] * qn + s[h]
        inv.append(1.0 / jnp.maximum(jnp.abs(den), floor[:, h:h + 1]))
        n_out_ref[:, hs] = w_st[:, h:h + 1] * n0 + w_in[:, h:h + 1] * k[h]
    items = [(h, b) for h in range(M_HEADS) for b in range(bb)]
    c0 = [c_ref[b, h] for h, b in items]
    q_col = [jnp.sum(jnp.where(eye, q[h][b:b + 1], 0.0), axis=1, keepdims=True) for h, b in items]
    k_col = [jnp.sum(jnp.where(eye, k[h][b:b + 1], 0.0), axis=1, keepdims=True) for h, b in items]
    qc = [jnp.sum(c * qq, axis=0, keepdims=True) for c, qq in zip(c0, q_col)]
    for i, (h, b) in enumerate(items):
        c_out_ref[b, h] = (w_st[b:b + 1, h:h + 1] * c0[i]
                           + (w_in[b:b + 1, h:h + 1] * k_col[i]) * vv[h][b:b + 1])
    for h, hs in enumerate(heads):
        qc_h = jnp.concatenate(qc[h * bb:(h + 1) * bb], axis=0)
        hv_scr[:, hs] = (w_st[:, h:h + 1] * qc_h + s[h] * vv[h]) * inv[h]
    hv = hv_scr[...]
    gnw = gnw_ref[...]
    skip = skip_ref[...]
    outs = []
    for h in range(M_HEADS):
        hs = slice(h * M_HEAD_DIM, (h + 1) * M_HEAD_DIM)
        outs.append(_head_norm_rows(hv[:, hs], M_GN_EPS) * gnw[:, hs] + skip[:, hs] * xc[:, hs])
    o = pm_ref[:, 2 * M_WIDTH:3 * M_WIDTH]
    y_ref[...] = _sigmoid(o) * jnp.concatenate(outs, axis=1)


def _mlstm_step(pm, conv0, c0, n0, m0, p, bb):
    nb = pm.shape[0]
    row = lambda n: _const_spec((1, n))
    c_spec = pl.BlockSpec((bb, M_HEADS, M_HEAD_DIM, M_HEAD_DIM), lambda i: (i, 0, 0, 0))
    n_spec = pl.BlockSpec((bb, M_WIDTH), lambda i: (i, 0))
    m_spec = pl.BlockSpec((bb, M_HEADS), lambda i: (i, 0))
    conv_spec = pl.BlockSpec((CONV_W - 1, bb, M_WIDTH), lambda i: (0, i, 0))
    return pl.pallas_call(
        functools.partial(_mlstm_step_body, bb=bb),
        grid=(nb // bb,),
        in_specs=[pl.BlockSpec((bb, M_COLS), lambda i: (i, 0)), conv_spec, c_spec, n_spec, m_spec,
                  _const_spec((CONV_W, M_WIDTH)), row(M_WIDTH),
                  _const_spec((M_HEADS, M_HEAD_DIM, M_HEAD_DIM)), _const_spec((M_HEADS, M_HEAD_DIM, M_HEAD_DIM)),
                  row(LANES), row(LANES), row(M_WIDTH), row(M_WIDTH)],
        out_specs=[pl.BlockSpec((bb, M_WIDTH), lambda i: (i, 0)), c_spec, n_spec, m_spec, conv_spec],
        out_shape=[jax.ShapeDtypeStruct((nb, M_WIDTH), F32),
                   jax.ShapeDtypeStruct(c0.shape, F32),
                   jax.ShapeDtypeStruct(n0.shape, F32),
                   jax.ShapeDtypeStruct(m0.shape, F32),
                   jax.ShapeDtypeStruct(conv0.shape, F32)],
        scratch_shapes=[pltpu.VMEM((bb, M_WIDTH), F32)],
        compiler_params=pltpu.CompilerParams(dimension_semantics=("arbitrary",), vmem_limit_bytes=VMEM_LIMIT),
        name="mlstm_step",
    )(pm, conv0, c0, n0, m0, p["conv_w"], p["conv_b"], p["wq"], p["wk"], p["i_b"], p["f_b"],
      p["gn_w"], p["skip"])


FFN_BLOCK = 1024


W_CHUNKS = 16


def _merge_ffn_rows(x_ref, pg_ref, yr, ym, rup, mup, wout, gffn_ref, w1, w2, gfin_ref):
    up_r = jnp.dot(yr.astype(BF16), rup[...], preferred_element_type=F32)
    up_m = jnp.dot(ym.astype(BF16), mup[...], preferred_element_type=F32)
    merged = _sigmoid(pg_ref[:, 0:D_MODEL]) * up_r + _sigmoid(pg_ref[:, D_MODEL:]) * up_m
    x1 = x_ref[...] + jnp.dot(merged.astype(BF16), wout[...], preferred_element_type=F32)
    hn = _rms(x1, gffn_ref[...]).astype(BF16)
    x2 = x1
    for j in range(D_FF // FFN_BLOCK):
        cols = slice(j * FFN_BLOCK, (j + 1) * FFN_BLOCK)
        hid = jnp.maximum(jnp.dot(hn, w1[:, cols], preferred_element_type=F32), 0.0)
        x2 = x2 + jnp.dot((hid * hid).astype(BF16), w2[cols, :], preferred_element_type=F32)
    return _rms(x2, gfin_ref[...])


def _out_body(x_ref, pg_ref, yr_ref, ym_ref, xs_ref, pgs_ref, yrs_ref, yms_ref,
              rup_ref, mup_ref, wout_ref, w1_ref, w2_ref, gffn_ref, gfin_ref,
              y_ref, ys_ref, rup_scr, mup_scr, wout_scr, w1_scr, w2_scr):
    s = pl.program_id(0)

    @pl.when(s < W_CHUNKS)
    def _():
        for src, dst in ((rup_ref, rup_scr), (mup_ref, mup_scr), (wout_ref, wout_scr), (w1_ref, w1_scr),
                         (w2_ref, w2_scr)):
            rows = src.shape[0]
            dst[pl.ds(pl.multiple_of(s * rows, rows), rows), :] = src[...].astype(BF16)

    weights = (rup_scr, mup_scr, wout_scr, gffn_ref, w1_scr, w2_scr, gfin_ref)

    @pl.when(s >= W_CHUNKS)
    def _():
        y_ref[...] = _merge_ffn_rows(x_ref, pg_ref, yr_ref[...], ym_ref[...], *weights)

    @pl.when(s == pl.num_programs(0) - 1)
    def _():
        ys_ref[...] = _merge_ffn_rows(xs_ref, pgs_ref, yrs_ref[...], yms_ref[...], *weights)


def _out(x2, pg, yr, ym, xs, pgs, yrs, yms, p, tm):
    m, ms = x2.shape[0], xs.shape[0]
    tile = lambda n: pl.BlockSpec((tm, n), lambda s: (jnp.maximum(s - W_CHUNKS, 0), 0))
    whole = lambda a: _const_spec(a.shape)
    chunk = lambda a: pl.BlockSpec((a.shape[0] // W_CHUNKS, a.shape[1]), lambda s: (jnp.minimum(s, W_CHUNKS - 1), 0))
    weights = (p["r_up"], p["m_up"], p["w_out"], p["w1"], p["w2"])
    return pl.pallas_call(
        _out_body,
        grid=(W_CHUNKS + m // tm,),
        in_specs=[tile(D_MODEL), tile(GATE_COLS), tile(R_WIDTH), tile(M_WIDTH),
                  whole(xs), whole(pgs), whole(yrs), whole(yms)]
                 + [chunk(w) for w in weights] + [_const_spec((1, D_MODEL)), _const_spec((1, D_MODEL))],
        out_specs=[tile(D_MODEL), pl.BlockSpec((ms, D_MODEL), lambda s: (0, 0))],
        out_shape=[jax.ShapeDtypeStruct((m, D_MODEL), F32), jax.ShapeDtypeStruct((ms, D_MODEL), F32)],
        scratch_shapes=[pltpu.VMEM(w.shape, BF16) for w in weights],
        compiler_params=pltpu.CompilerParams(dimension_semantics=("arbitrary",), vmem_limit_bytes=VMEM_LIMIT),
        name="merge_ffn",
    )(x2, pg, yr, ym, xs, pgs, yrs, yms, *weights, p["g_ffn"], p["g_fin"])


PROJ_TM = 512
RWKV_TT = 512
MLSTM_TT = 1024
OUT_TM = 512
STEP_BB = 16


def _head_segments():
    head = jnp.arange(GROUP_W) // R_HEAD_DIM
    return (head[:, None] == head[None, :]).astype(BF16)


def _pad_lanes(v, start):
    out = jnp.zeros((1, LANES), F32)
    return lax.dynamic_update_slice(out, v.reshape(1, -1), (0, start))


def kernel(x_prompt, x_sample, state_rwkv_shift, state_rwkv_wkv, state_mlstm_C, state_mlstm_n, state_mlstm_m, state_mlstm_conv, norm_mix_g, w_in, r_mu, r_w0, r_w2, r_a0, r_a2, r_g2, r_kk, r_ka, r_rk, r_gn_w, r_gn_b, r_up, m_conv_w, m_conv_b, m_wq, m_wk, m_i_b, m_f_b, m_gn_w, m_skip, m_up, gate_b, w_out, norm_ffn_g, ffn_w1, ffn_w2, norm_final_g):
    nbp, seq, _ = x_prompt.shape
    nbs = x_sample.shape[0]
    w = w_in[0]
    wt = jnp.transpose(w)
    g_mix = norm_mix_g[0].reshape(1, D_MODEL)
    gb = gate_b[0].reshape(1, GATE_COLS)
    rp = dict(mu=r_mu[0].reshape(1, -1), w0=r_w0[0].reshape(1, -1), w2=r_w2[0].astype(BF16),
              a0=r_a0[0].reshape(1, -1), a2=r_a2[0].astype(BF16), g2=r_g2[0].astype(BF16),
              kk=r_kk[0].reshape(1, -1), ka=r_ka[0].reshape(1, -1), rk=r_rk[0].reshape(1, -1),
              gn_w=r_gn_w[0].reshape(1, -1), gn_b=r_gn_b[0].reshape(1, -1), seg=_head_segments())
    mp = dict(conv_w=m_conv_w[0], conv_b=m_conv_b[0].reshape(1, -1), wq=m_wq[0].astype(BF16),
              wk=m_wk[0].astype(BF16), i_b=_pad_lanes(m_i_b[0], 0), f_b=_pad_lanes(m_f_b[0], M_HEADS),
              gn_w=m_gn_w[0].reshape(1, -1), skip=m_skip[0].reshape(1, -1))
    op = dict(r_up=r_up[0], m_up=m_up[0], w_out=w_out[0], g_ffn=norm_ffn_g[0].reshape(1, -1),
              w1=ffn_w1[0], w2=ffn_w2[0], g_fin=norm_final_g.reshape(1, -1))

    xp = x_prompt.reshape(nbp * seq, D_MODEL)
    pr, pm, pg, xn_s, xn_last, pr_s, pm_s, pg_s, prev_s = _proj(
        xp, x_sample[:, 0], state_rwkv_shift[0], x_prompt[:, seq - 1], g_mix, wt, gb, PROJ_TM)
    y_r, wkv_p = _rwkv_seq(pr.reshape(SEQ_WAYS, -1, R_SHIFT_COLS), rp, nbp, seq // RWKV_TT, RWKV_TT)
    y_r = y_r.reshape(nbp * seq, R_WIDTH)
    wkv_p = wkv_p.reshape(nbp, R_HEADS, R_HEAD_DIM, R_HEAD_DIM)
    y_m, c_p, n_p, m_p = _mlstm_seq(pm, mp, nbp, seq // MLSTM_TT, MLSTM_TT)
    conv_p = pm.reshape(nbp, seq, M_COLS)[:, seq - (CONV_W - 1):, :M_WIDTH]

    yr_s, wkv_t = _rwkv_step(pr_s, prev_s, jnp.transpose(state_rwkv_wkv[0], (1, 2, 3, 0)), rp)
    wkv_s = jnp.transpose(wkv_t, (3, 0, 1, 2))
    conv0 = jnp.transpose(state_mlstm_conv[0], (1, 0, 2))
    ym_s, c_s, n_s, m_s, conv_s = _mlstm_step(pm_s, conv0, state_mlstm_C[0],
                                              state_mlstm_n[0].reshape(nbs, M_WIDTH),
                                              state_mlstm_m[0], mp, STEP_BB)

    y_p, y_s = _out(xp, pg, y_r, y_m, x_sample[:, 0], pg_s, yr_s, ym_s, op, OUT_TM)

    return (y_p.reshape(nbp, seq, D_MODEL), y_s.reshape(nbs, 1, D_MODEL),
            xn_last[None], wkv_p[None], c_p[None], n_p[:, :, 0, :][None], m_p[:, :, 0, 0][None], conv_p[None],
            xn_s[None], wkv_s[None], c_s[None], n_s.reshape(nbs, M_HEADS, M_HEAD_DIM)[None], m_s[None],
            jnp.transpose(conv_s, (1, 0, 2))[None])
```

```python
import functools
import math

import jax
import jax.numpy as jnp
from jax import lax
from jax.experimental import pallas as pl
from jax.experimental.pallas import tpu as pltpu

F32 = jnp.float32
BF16 = jnp.bfloat16

D_MODEL = 1024
R_HEADS = 8
R_HEAD_DIM = 64
R_WIDTH = R_HEADS * R_HEAD_DIM
GROUP_HEADS = 4
R_GROUPS = R_HEADS // GROUP_HEADS
GROUP_W = GROUP_HEADS * R_HEAD_DIM
W_LORA = 64
A_LORA = 64
G_LORA = 128
R_GN_EPS = 64e-5
M_HEADS = 4
M_HEAD_DIM = 128
M_WIDTH = M_HEADS * M_HEAD_DIM
CONV_W = 4
M_GN_EPS = 1e-5
D_FF = 4 * D_MODEL
RMS_EPS = 1e-6
R_SHIFT_COLS = 3 * R_WIDTH + W_LORA + A_LORA + G_LORA
M_COLS = 3 * M_WIDTH + 2 * M_HEADS
GATE_COLS = 2 * D_MODEL

LANES = 128
SUBLANES = 8
CHUNK = 64
RM_COLS = R_SHIFT_COLS + M_COLS
VMEM_LIMIT = 56 * 1024 * 1024


def _mm(a, b, dims="nn"):
    ca = 1 if dims[0] == "n" else 0
    cb = 0 if dims[1] == "n" else 1
    dn = (((ca,), (cb,)), ((), ()))
    return lax.dot_general(a.astype(BF16), b.astype(BF16), dn, preferred_element_type=F32)


def _mm_mask_lhs(mask, b):
    b_hi = b.astype(BF16)
    return _mm(mask, b_hi) + _mm(mask, b - b_hi.astype(F32))


def _mm_mask_rhs(a, mask):
    a_hi = a.astype(BF16)
    return _mm(a_hi, mask) + _mm(a - a_hi.astype(F32), mask)


def _rms(x, g):
    return x * lax.rsqrt(jnp.mean(x * x, axis=-1, keepdims=True) + RMS_EPS) * g


def _sigmoid(x):
    return 1.0 / (1.0 + jnp.exp(-x))


def _log_sigmoid(x):
    return jnp.minimum(x, 0.0) - jnp.log(1.0 + jnp.exp(-jnp.abs(x)))


def _iota2(shape, axis):
    return lax.broadcasted_iota(jnp.int32, shape, axis)


def _const_spec(shape):
    nd = len(shape)
    return pl.BlockSpec(shape, lambda *_: (0,) * nd, pipeline_mode=pl.Buffered(1))


def _proj_cols(xb, wt_ref, lo, hi):
    return lax.dot_general(xb, wt_ref[lo:hi, :], (((1,), (1,)), ((), ())), preferred_element_type=F32)


def _proj_body(x_ref, g_ref, wrm_ref, wg_ref, gb_ref, pr_ref, pm_ref, pg_ref):
    xb = _rms(x_ref[...], g_ref[...]).astype(BF16)
    pr_ref[...] = _proj_cols(xb, wrm_ref, 0, R_SHIFT_COLS)
    pm_ref[...] = _proj_cols(xb, wrm_ref, R_SHIFT_COLS, RM_COLS)
    pg_ref[...] = _proj_cols(xb, wg_ref, 0, GATE_COLS) + gb_ref[...]


def _proj(x2, g, w_rm, w_g, gate_b, tm):
    m = x2.shape[0]
    return pl.pallas_call(
        _proj_body,
        grid=(m // tm,),
        in_specs=[pl.BlockSpec((tm, D_MODEL), lambda i: (i, 0)),
                  _const_spec((1, D_MODEL)),
                  _const_spec((RM_COLS, D_MODEL)), _const_spec((GATE_COLS, D_MODEL)),
                  _const_spec((1, GATE_COLS))],
        out_specs=[pl.BlockSpec((tm, R_SHIFT_COLS), lambda i: (i, 0)),
                   pl.BlockSpec((tm, M_COLS), lambda i: (i, 0)),
                   pl.BlockSpec((tm, GATE_COLS), lambda i: (i, 0))],
        out_shape=[jax.ShapeDtypeStruct((m, R_SHIFT_COLS), F32),
                   jax.ShapeDtypeStruct((m, M_COLS), F32),
                   jax.ShapeDtypeStruct((m, GATE_COLS), F32)],
        compiler_params=pltpu.CompilerParams(dimension_semantics=("arbitrary",), vmem_limit_bytes=VMEM_LIMIT),
        name="proj",
    )(x2, g, w_rm, w_g, gate_b)


def _proj_step_body(xs_ref, sh_ref, xl_ref, g_ref, wrm_ref, wg_ref, gb_ref,
                    xns_ref, xnl_ref, pr_ref, pm_ref, pg_ref, prev_ref):
    g = g_ref[...]
    xn = _rms(xs_ref[...], g)
    xns_ref[...] = xn
    xnl_ref[...] = _rms(xl_ref[...], g)
    xb = xn.astype(BF16)
    pr_ref[...] = _proj_cols(xb, wrm_ref, 0, R_SHIFT_COLS)
    pm_ref[...] = _proj_cols(xb, wrm_ref, R_SHIFT_COLS, RM_COLS)
    pg_ref[...] = _proj_cols(xb, wg_ref, 0, GATE_COLS) + gb_ref[...]
    prev_ref[...] = _proj_cols(sh_ref[...].astype(BF16), wrm_ref, 0, R_SHIFT_COLS)


def _proj_step(xs, shift0, xlast, g, w_rm, w_g, gate_b):
    nb = xs.shape[0]
    nl = xlast.shape[0]
    return pl.pallas_call(
        _proj_step_body,
        out_shape=[jax.ShapeDtypeStruct((nb, D_MODEL), F32),
                   jax.ShapeDtypeStruct((nl, D_MODEL), F32),
                   jax.ShapeDtypeStruct((nb, R_SHIFT_COLS), F32),
                   jax.ShapeDtypeStruct((nb, M_COLS), F32),
                   jax.ShapeDtypeStruct((nb, GATE_COLS), F32),
                   jax.ShapeDtypeStruct((nb, R_SHIFT_COLS), F32)],
        compiler_params=pltpu.CompilerParams(vmem_limit_bytes=VMEM_LIMIT),
        name="proj_step",
    )(xs, shift0, xlast, g, w_rm, w_g, gate_b)


LOG_DECAY_SCALE = -math.exp(-0.5)


def _rwkv_mix(pr, prev, mu, w0, w2, a0, a2, g2, r_kk, r_ka):
    mixed = pr + (prev - pr) * mu
    r = mixed[:, 0:R_WIDTH]
    k = mixed[:, R_WIDTH:2 * R_WIDTH]
    v = mixed[:, 2 * R_WIDTH:3 * R_WIDTH]
    xw = mixed[:, 3 * R_WIDTH:3 * R_WIDTH + W_LORA]
    xa = mixed[:, 3 * R_WIDTH + W_LORA:3 * R_WIDTH + W_LORA + A_LORA]
    xg = mixed[:, 3 * R_WIDTH + W_LORA + A_LORA:]
    w = w0 + _mm(jnp.tanh(xw), w2)
    logw = LOG_DECAY_SCALE * _sigmoid(w)
    a = _sigmoid(a0 + _mm(xa, a2))
    g = _mm(_sigmoid(xg), g2)
    kk0 = k * r_kk
    k2 = k * (1.0 + (a - 1.0) * r_ka)
    return r, k2, v, logw, a, g, kk0


def _rwkv_tile(pr_ref, mu_ref, w0_ref, w2_ref, a0_ref, a2_ref, g2_ref, kk_ref, ka_ref, rk_ref,
               gnw_ref, gnb_ref, seg_ref, s_scr, carry_scr, ops_scr, fv_scr, tt):
    pr = pr_ref[...]
    row = _iota2(pr.shape, 0)
    prev = jnp.where(row == 0, carry_scr[...], pltpu.roll(pr, 1, axis=0))
    carry_scr[...] = pr[tt - 1:tt, :]

    r, k2, v, logw, a, g, kk0 = _rwkv_mix(pr, prev, mu_ref[...], w0_ref[...], w2_ref[...], a0_ref[...],
                                          a2_ref[...], g2_ref[...], kk_ref[...], ka_ref[...])
    yield
    ri = _iota2((CHUNK, GROUP_W), 0)
    ci = _iota2((CHUNK, GROUP_W), 1)
    src = ci % R_HEAD_DIM
    lane_head = ci // R_HEAD_DIM
    low_incl = src <= ri
    low_strict = src < ri
    eye = (src == ri).astype(F32)
    rr = _iota2((GROUP_W, GROUP_W), 0)
    cc = _iota2((GROUP_W, GROUP_W), 1)
    same_head = (rr // R_HEAD_DIM) == (cc // R_HEAD_DIM)
    eye_bd = (rr == cc).astype(F32)
    seg = seg_ref[...]

    def seg_sum(x, mm=_mm):
        return jnp.concatenate([mm(x[:, q * GROUP_W:(q + 1) * GROUP_W], seg) for q in range(R_GROUPS)], axis=1)
    tril = (_iota2((CHUNK, CHUNK), 1) <= _iota2((CHUNK, CHUNK), 0)).astype(F32)

    nc = tt // CHUNK
    cum = jnp.concatenate([_mm_mask_lhs(tril, logw[c * CHUNK:(c + 1) * CHUNK]) for c in range(nc)], axis=0)
    yield
    cum_last = jnp.concatenate(
        [jnp.broadcast_to(cum[(c + 1) * CHUNK - 1:(c + 1) * CHUNK, :], (CHUNK, R_WIDTH)) for c in range(nc)], axis=0)
    d_inv = jnp.exp(-cum)
    d_end = jnp.exp(cum_last - cum)
    d_last = jnp.exp(cum_last)
    kk = kk0 / jnp.maximum(jnp.sqrt(seg_sum(kk0 * kk0)), 1e-12)
    yield
    kka = kk * a
    for j, x in enumerate((kk * jnp.exp(cum - logw), kka * d_inv, k2 * d_inv, kka * d_end, k2 * d_end)):
        ops_scr[:, j * R_WIDTH:(j + 1) * R_WIDTH] = x.astype(BF16)
    for j, x in enumerate((r * jnp.exp(cum), v, g, r * k2 * rk_ref[...])):
        fv_scr[:, j * R_WIDTH:(j + 1) * R_WIDTH] = x

    items = [(c, p) for c in range(nc) for p in range(R_GROUPS)]
    blk = lambda x, c, p: x[c * CHUNK:(c + 1) * CHUNK, p * GROUP_W:(p + 1) * GROUP_W]
    ops = lambda j, c, p: ops_scr[c * CHUNK:(c + 1) * CHUNK, j * R_WIDTH + p * GROUP_W:j * R_WIDTH + (p + 1) * GROUP_W]
    fvb = lambda j, c, p: fv_scr[c * CHUNK:(c + 1) * CHUNK, j * R_WIDTH + p * GROUP_W:j * R_WIDTH + (p + 1) * GROUP_W]
    each = lambda f, *ls: [f(*xs) for xs in zip(*ls)]
    cat0 = lambda *xs: jnp.concatenate(xs, axis=0)
    cat1 = lambda *xs: jnp.concatenate(xs, axis=1)

    def bd(x):
        x = x.astype(F32)
        return cat0(*[jnp.where(lane_head == j, x, 0.0) for j in range(GROUP_HEADS)])

    a_ = [ops(0, c, p) for c, p in items]
    b_ = [ops(1, c, p) for c, p in items]
    k_ = [ops(2, c, p) for c, p in items]
    bh_ = [ops(3, c, p) for c, p in items]
    kh_ = [ops(4, c, p) for c, p in items]
    r_ = [fvb(0, c, p) for c, p in items]
    v_ = [fvb(1, c, p) for c, p in items]
    vbd = [bd(x) for x in v_]
    pq = each(lambda a1, r1, b1, k1: _mm(cat0(a1, r1), cat0(bd(b1), bd(k1)), "nt"), a_, r_, b_, k_)
    yield
    t_ab = [jnp.where(low_strict, x[:CHUNK, :GROUP_W], 0.0) for x in pq]
    t_ak = [jnp.where(low_strict, x[:CHUNK, GROUP_W:], 0.0) for x in pq]
    m_rb = [jnp.where(low_incl, x[CHUNK:, :GROUP_W], 0.0) for x in pq]
    m_rk = [jnp.where(low_incl, x[CHUNK:, GROUP_W:], 0.0) for x in pq]
    q_pow = each(lambda t: _mm(t, bd(t)), t_ab)
    yield
    x_inv = [eye - t for t in t_ab]
    for _ in range(4):
        z = each(lambda x, q: _mm(cat0(x, q), bd(q)), x_inv, q_pow)
        x_inv = [x + zz[:CHUNK] for x, zz in zip(x_inv, z)]
        q_pow = [zz[CHUNK:] for zz in z]
        yield
    x_inv = each(lambda x, q: x + _mm(x, bd(q)), x_inv, q_pow)
    yield
    tv = each(_mm, t_ak, vbd)
    yield
    xw = each(lambda x, a1, t: _mm(x, cat1(bd(a1), bd(t))), x_inv, a_, tv)
    yield
    mw = each(lambda m1, x: _mm(m1, cat1(bd(x[:, :GROUP_W]), bd(x[:, GROUP_W:]))), m_rb, xw)
    yield
    q_eff = [r1 - m1[:, :GROUP_W] for r1, m1 in zip(r_, mw)]
    y_loc = each(lambda m1, vb, m2: _mm(m1, vb) - m2[:, GROUP_W:], m_rk, vbd, mw)
    yield
    g_bd = [eye_bd * blk(d_last, c, p)[0:1] - jnp.where(same_head, _mm(x[:, :GROUP_W], b1, "tn"), 0.0)
            for (c, p), x, b1 in zip(items, xw, bh_)]
    yield
    h_full = each(lambda vv, x, k1, b1: _mm(cat0(vv, x[:, GROUP_W:]), cat0(k1, -b1), "tn"), v_, xw, kh_, bh_)
    yield
    h_pair = []
    for x in h_full:
        acc = x[:CHUNK]
        for j in range(1, GROUP_HEADS):
            acc = jnp.where(lane_head == j, x[j * CHUNK:(j + 1) * CHUNK], acc)
        h_pair.append(acc)

    state = [s_scr[p] for p in range(R_GROUPS)]
    ys = []
    for i, (c, p) in enumerate(items):
        ys.append(_mm(q_eff[i], bd(state[p]), "nt") + y_loc[i])
        state[p] = _mm(state[p], g_bd[i]) + h_pair[i]
        if p == R_GROUPS - 1:
            yield
    for p in range(R_GROUPS):
        s_scr[p] = state[p]

    y_all = cat0(*[cat1(*ys[c * R_GROUPS:(c + 1) * R_GROUPS]) for c in range(nc)])
    yc = y_all - seg_sum(y_all) * (1.0 / R_HEAD_DIM)
    yield
    var = seg_sum(yc * yc) * (1.0 / R_HEAD_DIM)
    yield
    v_all, g_all, rkk = [fv_scr[:, j * R_WIDTH:(j + 1) * R_WIDTH] for j in (1, 2, 3)]
    bonus = seg_sum(rkk, _mm_mask_rhs)
    return (yc * lax.rsqrt(var + R_GN_EPS) * gnw_ref[...] + gnb_ref[...] + bonus * v_all) * g_all


N_RWKV_PARAMS = 12


SEQ_WAYS = 2
SEQ_LAG = 4


def _staggered(gens, lag):
    out = [None] * len(gens)
    live = set(range(len(gens)))
    step = 0
    while live:
        for j in sorted(live):
            if step >= j * lag:
                try:
                    next(gens[j])
                except StopIteration as stop:
                    out[j] = stop.value
                    live.discard(j)
        step += 1
    return out


def _rwkv_seq_body(*refs, tt):
    pr_ref, params = refs[0], refs[1:1 + N_RWKV_PARAMS]
    y_ref, s_out_ref, s_scr, carry_scr, ops_scr, fv_scr = refs[1 + N_RWKV_PARAMS:]
    t_idx = pl.program_id(1)

    @pl.when(t_idx == 0)
    def _():
        s_scr[...] = jnp.zeros_like(s_scr)
        carry_scr[...] = jnp.zeros_like(carry_scr)

    ys = _staggered([_rwkv_tile(pr_ref.at[w], *params, s_scr.at[w], carry_scr.at[w], ops_scr.at[w],
                                fv_scr.at[w], tt) for w in range(SEQ_WAYS)], SEQ_LAG)
    for w in range(SEQ_WAYS):
        y_ref[w] = ys[w]

    @pl.when(t_idx == pl.num_programs(1) - 1)
    def _():
        for w in range(SEQ_WAYS):
            for p in range(R_GROUPS):
                for j in range(GROUP_HEADS):
                    s_out_ref[w, 0, GROUP_HEADS * p + j] = s_scr[w, p][:, j * R_HEAD_DIM:(j + 1) * R_HEAD_DIM]


def _rwkv_seq(pr, p, nb, nt, tt):
    m = pr.shape[1]
    row = lambda n: _const_spec((1, n))
    return pl.pallas_call(
        functools.partial(_rwkv_seq_body, tt=tt),
        grid=(nb // SEQ_WAYS, nt),
        in_specs=[pl.BlockSpec((SEQ_WAYS, tt, R_SHIFT_COLS), lambda b, t: (0, b * nt + t, 0)),
                  row(R_SHIFT_COLS), row(R_WIDTH), _const_spec((W_LORA, R_WIDTH)), row(R_WIDTH),
                  _const_spec((A_LORA, R_WIDTH)), _const_spec((G_LORA, R_WIDTH)), row(R_WIDTH), row(R_WIDTH),
                  row(R_WIDTH), row(R_WIDTH), row(R_WIDTH), _const_spec((GROUP_W, GROUP_W))],
        out_specs=[pl.BlockSpec((SEQ_WAYS, tt, R_WIDTH), lambda b, t: (0, b * nt + t, 0)),
                   pl.BlockSpec((SEQ_WAYS, 1, R_HEADS, R_HEAD_DIM, R_HEAD_DIM), lambda b, t: (0, b, 0, 0, 0))],
        out_shape=[jax.ShapeDtypeStruct((SEQ_WAYS, m, R_WIDTH), F32),
                   jax.ShapeDtypeStruct((SEQ_WAYS, nb // SEQ_WAYS, R_HEADS, R_HEAD_DIM, R_HEAD_DIM), F32)],
        scratch_shapes=[pltpu.VMEM((SEQ_WAYS, R_GROUPS, R_HEAD_DIM, GROUP_W), F32),
                        pltpu.VMEM((SEQ_WAYS, 1, R_SHIFT_COLS), F32),
                        pltpu.VMEM((SEQ_WAYS, tt, 5 * R_WIDTH), BF16),
                        pltpu.VMEM((SEQ_WAYS, tt, 4 * R_WIDTH), F32)],
        compiler_params=pltpu.CompilerParams(dimension_semantics=("arbitrary", "arbitrary"),
                                             vmem_limit_bytes=VMEM_LIMIT),
        name="rwkv_seq",
    )(pr, p["mu"], p["w0"], p["w2"], p["a0"], p["a2"], p["g2"], p["kk"], p["ka"], p["rk"],
      p["gn_w"], p["gn_b"], p["seg"])


def _rwkv_step_body(pr_ref, prev_ref, s_ref, mu_ref, w0_ref, w2_ref, a0_ref, a2_ref, g2_ref, kk_ref, ka_ref,
                    rk_ref, gnw_ref, gnb_ref, seg_ref, y_ref, s_out_ref,
                    kk_scr, kka_scr, dec_scr, k_scr, v_scr, r_scr, yt_scr, g_scr, vrow_scr, bonus_scr):
    h = pl.program_id(0)
    seg = seg_ref[...]

    def seg_sum(x, mm=_mm):
        return jnp.concatenate([mm(x[:, q * GROUP_W:(q + 1) * GROUP_W], seg) for q in range(R_GROUPS)], axis=1)

    @pl.when(h == 0)
    def _():
        r, k2, v, logw, a, g, kk0 = _rwkv_mix(pr_ref[...], prev_ref[...], mu_ref[...], w0_ref[...], w2_ref[...],
                                              a0_ref[...], a2_ref[...], g2_ref[...], kk_ref[...], ka_ref[...])
        kk = kk0 / jnp.maximum(jnp.sqrt(seg_sum(kk0 * kk0)), 1e-12)
        kk_scr[...] = kk.T
        kka_scr[...] = (kk * a).T
        dec_scr[...] = jnp.exp(logw).T
        k_scr[...] = k2.T
        v_scr[...] = v.T
        r_scr[...] = r.T
        g_scr[...] = g
        vrow_scr[...] = v
        bonus_scr[...] = seg_sum(r * k2 * rk_ref[...], _mm_mask_rhs)

    base = pl.multiple_of(h * R_HEAD_DIM, R_HEAD_DIM)
    hs = pl.ds(base, R_HEAD_DIM)
    kk_t, kka_t, dec_t, k_t, r_t = kk_scr[hs, :], kka_scr[hs, :], dec_scr[hs, :], k_scr[hs, :], r_scr[hs, :]
    for i in range(R_HEAD_DIM):
        s0 = s_ref[0, i]
        s_kk = jnp.sum(s0 * kk_t, axis=0, keepdims=True)
        s1 = s0 * dec_t - s_kk * kka_t + v_scr[pl.ds(base + i, 1), :] * k_t
        s_out_ref[0, i] = s1
        yt_scr[pl.ds(base + i, 1), :] = jnp.sum(s1 * r_t, axis=0, keepdims=True)

    @pl.when(h == pl.num_programs(0) - 1)
    def _():
        y = yt_scr[...].T
        yc = y - seg_sum(y) * (1.0 / R_HEAD_DIM)
        var = seg_sum(yc * yc) * (1.0 / R_HEAD_DIM)
        y_ref[...] = (yc * lax.rsqrt(var + R_GN_EPS) * gnw_ref[...] + gnb_ref[...]
                      + bonus_scr[...] * vrow_scr[...]) * g_scr[...]


def _rwkv_step(pr, prev, s0_t, p):
    nb = pr.shape[0]
    row = lambda n: _const_spec((1, n))
    state_spec = pl.BlockSpec((1, R_HEAD_DIM, R_HEAD_DIM, nb), lambda h: (h, 0, 0, 0))
    tr = pltpu.VMEM((R_WIDTH, nb), F32)
    rw = pltpu.VMEM((nb, R_WIDTH), F32)
    return pl.pallas_call(
        _rwkv_step_body,
        grid=(R_HEADS,),
        in_specs=[_const_spec((nb, R_SHIFT_COLS)), _const_spec((nb, R_SHIFT_COLS)), state_spec,
                  row(R_SHIFT_COLS), row(R_WIDTH), _const_spec((W_LORA, R_WIDTH)), row(R_WIDTH),
                  _const_spec((A_LORA, R_WIDTH)), _const_spec((G_LORA, R_WIDTH)), row(R_WIDTH), row(R_WIDTH),
                  row(R_WIDTH), row(R_WIDTH), row(R_WIDTH), _const_spec((GROUP_W, GROUP_W))],
        out_specs=[pl.BlockSpec((nb, R_WIDTH), lambda h: (0, 0)), state_spec],
        out_shape=[jax.ShapeDtypeStruct((nb, R_WIDTH), F32),
                   jax.ShapeDtypeStruct(s0_t.shape, F32)],
        scratch_shapes=[tr, tr, tr, tr, tr, tr, tr, rw, rw, rw],
        compiler_params=pltpu.CompilerParams(dimension_semantics=("arbitrary",), vmem_limit_bytes=VMEM_LIMIT),
        name="rwkv_step",
    )(pr, prev, s0_t, p["mu"], p["w0"], p["w2"], p["a0"], p["a2"], p["g2"], p["kk"], p["ka"], p["rk"],
      p["gn_w"], p["gn_b"], p["seg"])


GATE_LANE0 = 3 * M_WIDTH


def _gate_lanes(pm_ref):
    gates = pm_ref[:, GATE_LANE0:GATE_LANE0 + 2 * M_HEADS]
    return jnp.concatenate([gates, jnp.zeros((gates.shape[0], LANES - 2 * M_HEADS), F32)], axis=1)


def _head_norm_rows(x, eps):
    xc = x - jnp.mean(x, axis=1, keepdims=True)
    var = jnp.mean(xc * xc, axis=1, keepdims=True)
    return xc * lax.rsqrt(var + eps)


def _cummax_rows(x):
    n = x.shape[0]
    row = _iota2(x.shape, 0)
    d = 1
    while d < n:
        if d < SUBLANES:
            shifted = jnp.where(row < d, -jnp.inf, pltpu.roll(x, d, axis=0))
        else:
            shifted = jnp.concatenate([jnp.full((d, x.shape[1]), -jnp.inf, x.dtype), x[:n - d]], axis=0)
        x = jnp.maximum(x, shifted)
        d *= 2
    return x


def _mlstm_seq_body(pm_ref, cw_ref, cb_ref, wq_ref, wk_ref, ib_ref, fb_ref, gnw_ref, skip_ref,
                    y_ref, c_out_ref, n_out_ref, m_out_ref, c_scr, n_scr, m_scr, carry_scr, *, tt):
    t_idx = pl.program_id(1)

    @pl.when(t_idx == 0)
    def _():
        c_scr[...] = jnp.zeros_like(c_scr)
        n_scr[...] = jnp.zeros_like(n_scr)
        m_scr[...] = jnp.zeros_like(m_scr)
        carry_scr[...] = jnp.zeros_like(carry_scr)

    xm = pm_ref[:, 0:M_WIDTH]
    carry = carry_scr[...]
    carry_scr[...] = xm[tt - SUBLANES:tt, :]
    row8 = _iota2((SUBLANES, M_WIDTH), 0)
    cw = cw_ref[...]
    xc = cb_ref[...] + xm * cw[CONV_W - 1:CONV_W, :]
    for s in range(1, CONV_W):
        rolled = pltpu.roll(xm, s, axis=0)
        top = jnp.where(row8 < s, pltpu.roll(carry, s, axis=0), rolled[0:SUBLANES])
        shifted = jnp.concatenate([top, rolled[SUBLANES:]], axis=0)
        xc = xc + shifted * cw[CONV_W - 1 - s:CONV_W - s, :]
    xc = xc * _sigmoid(xc)

    gt = _gate_lanes(pm_ref)
    li_all = gt + ib_ref[...]
    lf_all = _log_sigmoid(gt + fb_ref[...])

    ri = _iota2((CHUNK, CHUNK), 0)
    ci = _iota2((CHUNK, CHUNK), 1)
    causal = ci <= ri
    tril = causal.astype(F32)
    gnw = gnw_ref[...]
    skip = skip_ref[...]

    ones_cl = jnp.ones((CHUNK, LANES), BF16)
    ones_ll = jnp.ones((LANES, LANES), BF16)
    nc = tt // CHUNK
    rows = lambda c: slice(c * CHUNK, (c + 1) * CHUNK)
    lanes = lambda h: slice(h * M_HEAD_DIM, (h + 1) * M_HEAD_DIM)

    bc_all = jnp.concatenate([_mm_mask_lhs(tril, lf_all[rows(c)]) for c in range(nc)], axis=0)
    bc_al = pltpu.roll(bc_all, LANES - M_HEADS, axis=1)
    u_all = li_all - bc_al
    cm_all = jnp.concatenate([_cummax_rows(u_all[rows(c)]) for c in range(nc)], axis=0)
    u_t = u_all.T

    items = [(c, h) for c in range(nc) for h in range(M_HEADS)]
    rep = lambda x, c, h: jnp.broadcast_to(x[rows(c), h:h + 1], (CHUNK, LANES))
    xc_b = [xc[rows(c), lanes(h)] for c, h in items]
    q = [_mm(x, wq_ref[h]) * (M_HEAD_DIM ** -0.5) for (c, h), x in zip(items, xc_b)]
    k = [_mm(x, wk_ref[h]) for (c, h), x in zip(items, xc_b)]
    vv = [pm_ref[rows(c), M_WIDTH + h * M_HEAD_DIM:M_WIDTH + (h + 1) * M_HEAD_DIM] for c, h in items]
    cm_r = [rep(cm_all, c, h) for c, h in items]
    bc_r = [rep(bc_al, c, h) for c, h in items]
    u_r = [rep(u_all, c, h) for c, h in items]

    m_run = [m_scr[h][0:1, 0:1] for h in range(M_HEADS)]
    m_prev, m_cap, m_end = [], [], []
    for i, (c, h) in enumerate(items):
        m_prev.append(m_run[h])
        m_cap.append(jnp.maximum(m_run[h], cm_r[i]))
        m_end.append(m_cap[i][CHUNK - 1:CHUNK, 0:1])
        m_run[h] = bc_r[i][CHUNK - 1:CHUNK, 0:1] + m_end[i]

    w_st = [jnp.exp(mp - mc) for mp, mc in zip(m_prev, m_cap)]
    floor = [jnp.exp(-(b + mc)) for b, mc in zip(bc_r, m_cap)]
    w_in = [jnp.exp(jnp.where(causal, u_t[h:h + 1, rows(c)] - mc[:, :CHUNK], -jnp.inf))
            for (c, h), mc in zip(items, m_cap)]
    s = [_mm(qq, kk, "nt") * w for qq, kk, w in zip(q, k, w_in)]
    sv = [_mm(ss, v1) for ss, v1 in zip(s, vv)]
    s_sum = [_mm(ss, ones_cl) for ss in s]
    we = [jnp.exp(u - me) for u, me in zip(u_r, m_end)]
    ge = [jnp.exp(mp - me) for mp, me in zip(m_prev, m_end)]
    kv = [_mm(kk, w1 * v1, "tn") for kk, w1, v1 in zip(k, we, vv)]
    k_sum = [jnp.sum(w1 * kk, axis=0, keepdims=True) for w1, kk in zip(we, k)]

    c_run = [c_scr[h] for h in range(M_HEADS)]
    n_run = [n_scr[h][0:1, :] for h in range(M_HEADS)]
    num, den = [], []
    for i, (c, h) in enumerate(items):
        num.append(w_st[i] * _mm(q[i], c_run[h]) + sv[i])
        den.append(w_st[i] * _mm(q[i] * n_run[h], ones_ll) + s_sum[i])
        c_run[h] = ge[i] * c_run[h] + kv[i]
        n_run[h] = ge[i] * n_run[h] + k_sum[i]
    for h in range(M_HEADS):
        c_scr[h] = c_run[h]
        n_scr[h] = jnp.broadcast_to(n_run[h], (SUBLANES, M_HEAD_DIM))
        m_scr[h] = jnp.broadcast_to(m_run[h], (SUBLANES, LANES))

    for i, (c, h) in enumerate(items):
        hh = num[i] / jnp.maximum(jnp.abs(den[i]), floor[i])
        hc = hh - _mm(hh, ones_ll) * (1.0 / M_HEAD_DIM)
        var = _mm(hc * hc, ones_ll) * (1.0 / M_HEAD_DIM)
        hn = hc * lax.rsqrt(var + M_GN_EPS) * gnw[:, lanes(h)] + skip[:, lanes(h)] * xc_b[i]
        o = pm_ref[rows(c), 2 * M_WIDTH + h * M_HEAD_DIM:2 * M_WIDTH + (h + 1) * M_HEAD_DIM]
        y_ref[rows(c), lanes(h)] = _sigmoid(o) * hn

    @pl.when(t_idx == pl.num_programs(1) - 1)
    def _():
        c_out_ref[0] = c_scr[...]
        n_out_ref[0] = n_scr[...]
        m_out_ref[0] = m_scr[...]


def _mlstm_seq(pm, p, nb, nt, tt):
    m = pm.shape[0]
    row = lambda n: _const_spec((1, n))
    return pl.pallas_call(
        functools.partial(_mlstm_seq_body, tt=tt),
        grid=(nb, nt),
        in_specs=[pl.BlockSpec((tt, M_COLS), lambda b, t: (b * nt + t, 0)),
                  _const_spec((CONV_W, M_WIDTH)), row(M_WIDTH),
                  _const_spec((M_HEADS, M_HEAD_DIM, M_HEAD_DIM)), _const_spec((M_HEADS, M_HEAD_DIM, M_HEAD_DIM)),
                  row(LANES), row(LANES), row(M_WIDTH), row(M_WIDTH)],
        out_specs=[pl.BlockSpec((tt, M_WIDTH), lambda b, t: (b * nt + t, 0)),
                   pl.BlockSpec((1, M_HEADS, M_HEAD_DIM, M_HEAD_DIM), lambda b, t: (b, 0, 0, 0)),
                   pl.BlockSpec((1, M_HEADS, SUBLANES, M_HEAD_DIM), lambda b, t: (b, 0, 0, 0)),
                   pl.BlockSpec((1, M_HEADS, SUBLANES, LANES), lambda b, t: (b, 0, 0, 0))],
        out_shape=[jax.ShapeDtypeStruct((m, M_WIDTH), F32),
                   jax.ShapeDtypeStruct((nb, M_HEADS, M_HEAD_DIM, M_HEAD_DIM), F32),
                   jax.ShapeDtypeStruct((nb, M_HEADS, SUBLANES, M_HEAD_DIM), F32),
                   jax.ShapeDtypeStruct((nb, M_HEADS, SUBLANES, LANES), F32)],
        scratch_shapes=[pltpu.VMEM((M_HEADS, M_HEAD_DIM, M_HEAD_DIM), F32),
                        pltpu.VMEM((M_HEADS, SUBLANES, M_HEAD_DIM), F32),
                        pltpu.VMEM((M_HEADS, SUBLANES, LANES), F32),
                        pltpu.VMEM((SUBLANES, M_WIDTH), F32)],
        compiler_params=pltpu.CompilerParams(dimension_semantics=("arbitrary", "arbitrary"),
                                             vmem_limit_bytes=VMEM_LIMIT),
        name="mlstm_seq",
    )(pm, p["conv_w"], p["conv_b"], p["wq"], p["wk"], p["i_b"], p["f_b"], p["gn_w"], p["skip"])


def _mlstm_step_body(pm_ref, conv_ref, c_ref, n_ref, m_ref, cw_ref, cb_ref, wq_ref, wk_ref, ib_ref, fb_ref,
                     gnw_ref, skip_ref, y_ref, c_out_ref, n_out_ref, m_out_ref, conv_out_ref, hv_scr, *, bb):
    xm = pm_ref[:, 0:M_WIDTH]
    cw = cw_ref[...]
    xc = cb_ref[...] + xm * cw[CONV_W - 1:CONV_W, :]
    for j in range(CONV_W - 1):
        xc = xc + conv_ref[j] * cw[j:j + 1, :]
    xc = xc * _sigmoid(xc)
    for j in range(CONV_W - 2):
        conv_out_ref[j] = conv_ref[j + 1]
    conv_out_ref[CONV_W - 2] = xm

    gt = _gate_lanes(pm_ref)
    li = (gt + ib_ref[...])[:, 0:M_HEADS]
    lf = _log_sigmoid(gt + fb_ref[...])[:, M_HEADS:2 * M_HEADS]
    m_prev = m_ref[...]
    g_st = lf + m_prev
    m_t = jnp.maximum(g_st, (lf - lf) + li)
    w_in = jnp.exp(((lf - lf) + li) - m_t)
    w_st = jnp.exp(g_st - m_t)
    floor = jnp.exp(-m_t)
    m_out_ref[...] = m_t

    ri = _iota2((M_HEAD_DIM, M_HEAD_DIM), 0)
    ci = _iota2((M_HEAD_DIM, M_HEAD_DIM), 1)
    eye = ri == ci
    heads = [slice(h * M_HEAD_DIM, (h + 1) * M_HEAD_DIM) for h in range(M_HEADS)]
    q = [_mm(xc[:, hs], wq_ref[h]) * (M_HEAD_DIM ** -0.5) for h, hs in enumerate(heads)]
    k = [_mm(xc[:, hs], wk_ref[h]) for h, hs in enumerate(heads)]
    vv = [pm_ref[:, M_WIDTH + h * M_HEAD_DIM:M_WIDTH + (h + 1) * M_HEAD_DIM] for h in range(M_HEADS)]
    s, inv = [], []
    for h, hs in enumerate(heads):
        n0 = n_ref[:, hs]
        qk = jnp.sum(q[h] * k[h], axis=1, keepdims=True)
        qn = jnp.sum(q[h] * n0, axis=1, keepdims=True)
        s.append(qk * w_in[:, h:h + 1])
        den = w_st[:, h:h + 1] * qn + s[h]
        inv.append(1.0 / jnp.maximum(jnp.abs(den), floor[:, h:h + 1]))
        n_out_ref[:, hs] = w_st[:, h:h + 1] * n0 + w_in[:, h:h + 1] * k[h]
    items = [(h, b) for h in range(M_HEADS) for b in range(bb)]
    c0 = [c_ref[b, h] for h, b in items]
    q_col = [jnp.sum(jnp.where(eye, q[h][b:b + 1], 0.0), axis=1, keepdims=True) for h, b in items]
    k_col = [jnp.sum(jnp.where(eye, k[h][b:b + 1], 0.0), axis=1, keepdims=True) for h, b in items]
    qc = [jnp.sum(c * qq, axis=0, keepdims=True) for c, qq in zip(c0, q_col)]
    for i, (h, b) in enumerate(items):
        c_out_ref[b, h] = (w_st[b:b + 1, h:h + 1] * c0[i]
                           + (w_in[b:b + 1, h:h + 1] * k_col[i]) * vv[h][b:b + 1])
    for h, hs in enumerate(heads):
        qc_h = jnp.concatenate(qc[h * bb:(h + 1) * bb], axis=0)
        hv_scr[:, hs] = (w_st[:, h:h + 1] * qc_h + s[h] * vv[h]) * inv[h]
    hv = hv_scr[...]
    gnw = gnw_ref[...]
    skip = skip_ref[...]
    outs = []
    for h in range(M_HEADS):
        hs = slice(h * M_HEAD_DIM, (h + 1) * M_HEAD_DIM)
        outs.append(_head_norm_rows(hv[:, hs], M_GN_EPS) * gnw[:, hs] + skip[:, hs] * xc[:, hs])
    o = pm_ref[:, 2 * M_WIDTH:3 * M_WIDTH]
    y_ref[...] = _sigmoid(o) * jnp.concatenate(outs, axis=1)


def _mlstm_step(pm, conv0, c0, n0, m0, p, bb):
    nb = pm.shape[0]
    row = lambda n: _const_spec((1, n))
    c_spec = pl.BlockSpec((bb, M_HEADS, M_HEAD_DIM, M_HEAD_DIM), lambda i: (i, 0, 0, 0))
    n_spec = pl.BlockSpec((bb, M_WIDTH), lambda i: (i, 0))
    m_spec = pl.BlockSpec((bb, M_HEADS), lambda i: (i, 0))
    conv_spec = pl.BlockSpec((CONV_W - 1, bb, M_WIDTH), lambda i: (0, i, 0))
    return pl.pallas_call(
        functools.partial(_mlstm_step_body, bb=bb),
        grid=(nb // bb,),
        in_specs=[pl.BlockSpec((bb, M_COLS), lambda i: (i, 0)), conv_spec, c_spec, n_spec, m_spec,
                  _const_spec((CONV_W, M_WIDTH)), row(M_WIDTH),
                  _const_spec((M_HEADS, M_HEAD_DIM, M_HEAD_DIM)), _const_spec((M_HEADS, M_HEAD_DIM, M_HEAD_DIM)),
                  row(LANES), row(LANES), row(M_WIDTH), row(M_WIDTH)],
        out_specs=[pl.BlockSpec((bb, M_WIDTH), lambda i: (i, 0)), c_spec, n_spec, m_spec, conv_spec],
        out_shape=[jax.ShapeDtypeStruct((nb, M_WIDTH), F32),
                   jax.ShapeDtypeStruct(c0.shape, F32),
                   jax.ShapeDtypeStruct(n0.shape, F32),
                   jax.ShapeDtypeStruct(m0.shape, F32),
                   jax.ShapeDtypeStruct(conv0.shape, F32)],
        scratch_shapes=[pltpu.VMEM((bb, M_WIDTH), F32)],
        compiler_params=pltpu.CompilerParams(dimension_semantics=("arbitrary",), vmem_limit_bytes=VMEM_LIMIT),
        name="mlstm_step",
    )(pm, conv0, c0, n0, m0, p["conv_w"], p["conv_b"], p["wq"], p["wk"], p["i_b"], p["f_b"],
      p["gn_w"], p["skip"])


FFN_BLOCK = 1024


W_CHUNKS = 16


def _merge_ffn_rows(x_ref, pg_ref, yr, ym, rup, mup, wout, gffn_ref, w1, w2, gfin_ref):
    up_r = jnp.dot(yr.astype(BF16), rup[...], preferred_element_type=F32)
    up_m = jnp.dot(ym.astype(BF16), mup[...], preferred_element_type=F32)
    merged = _sigmoid(pg_ref[:, 0:D_MODEL]) * up_r + _sigmoid(pg_ref[:, D_MODEL:]) * up_m
    x1 = x_ref[...] + jnp.dot(merged.astype(BF16), wout[...], preferred_element_type=F32)
    hn = _rms(x1, gffn_ref[...]).astype(BF16)
    x2 = x1
    for j in range(D_FF // FFN_BLOCK):
        cols = slice(j * FFN_BLOCK, (j + 1) * FFN_BLOCK)
        hid = jnp.maximum(jnp.dot(hn, w1[:, cols], preferred_element_type=F32), 0.0)
        x2 = x2 + jnp.dot((hid * hid).astype(BF16), w2[cols, :], preferred_element_type=F32)
    return _rms(x2, gfin_ref[...])


def _out_body(x_ref, pg_ref, yr_ref, ym_ref, xs_ref, pgs_ref, yrs_ref, yms_ref,
              rup_ref, mup_ref, wout_ref, w1_ref, w2_ref, gffn_ref, gfin_ref,
              y_ref, ys_ref, rup_scr, mup_scr, wout_scr, w1_scr, w2_scr):
    s = pl.program_id(0)

    @pl.when(s < W_CHUNKS)
    def _():
        for src, dst in ((rup_ref, rup_scr), (mup_ref, mup_scr), (wout_ref, wout_scr), (w1_ref, w1_scr),
                         (w2_ref, w2_scr)):
            rows = src.shape[0]
            dst[pl.ds(pl.multiple_of(s * rows, rows), rows), :] = src[...].astype(BF16)

    weights = (rup_scr, mup_scr, wout_scr, gffn_ref, w1_scr, w2_scr, gfin_ref)

    @pl.when(s >= W_CHUNKS)
    def _():
        y_ref[...] = _merge_ffn_rows(x_ref, pg_ref, yr_ref[...], ym_ref[...], *weights)

    @pl.when(s == pl.num_programs(0) - 1)
    def _():
        ys_ref[...] = _merge_ffn_rows(xs_ref, pgs_ref, yrs_ref[...], yms_ref[...], *weights)


def _out(x2, pg, yr, ym, xs, pgs, yrs, yms, p, tm):
    m, ms = x2.shape[0], xs.shape[0]
    tile = lambda n: pl.BlockSpec((tm, n), lambda s: (jnp.maximum(s - W_CHUNKS, 0), 0))
    whole = lambda a: _const_spec(a.shape)
    chunk = lambda a: pl.BlockSpec((a.shape[0] // W_CHUNKS, a.shape[1]), lambda s: (jnp.minimum(s, W_CHUNKS - 1), 0))
    weights = (p["r_up"], p["m_up"], p["w_out"], p["w1"], p["w2"])
    return pl.pallas_call(
        _out_body,
        grid=(W_CHUNKS + m // tm,),
        in_specs=[tile(D_MODEL), tile(GATE_COLS), tile(R_WIDTH), tile(M_WIDTH),
                  whole(xs), whole(pgs), whole(yrs), whole(yms)]
                 + [chunk(w) for w in weights] + [_const_spec((1, D_MODEL)), _const_spec((1, D_MODEL))],
        out_specs=[tile(D_MODEL), pl.BlockSpec((ms, D_MODEL), lambda s: (0, 0))],
        out_shape=[jax.ShapeDtypeStruct((m, D_MODEL), F32), jax.ShapeDtypeStruct((ms, D_MODEL), F32)],
        scratch_shapes=[pltpu.VMEM(w.shape, BF16) for w in weights],
        compiler_params=pltpu.CompilerParams(dimension_semantics=("arbitrary",), vmem_limit_bytes=VMEM_LIMIT),
        name="merge_ffn",
    )(x2, pg, yr, ym, xs, pgs, yrs, yms, *weights, p["g_ffn"], p["g_fin"])


PROJ_TM = 512
RWKV_TT = 512
MLSTM_TT = 1024
OUT_TM = 512
STEP_BB = 16


def _head_segments():
    head = jnp.arange(GROUP_W) // R_HEAD_DIM
    return (head[:, None] == head[None, :]).astype(BF16)


def _pad_lanes(v, start):
    out = jnp.zeros((1, LANES), F32)
    return lax.dynamic_update_slice(out, v.reshape(1, -1), (0, start))


def kernel(x_prompt, x_sample, state_rwkv_shift, state_rwkv_wkv, state_mlstm_C, state_mlstm_n, state_mlstm_m, state_mlstm_conv, norm_mix_g, w_in, r_mu, r_w0, r_w2, r_a0, r_a2, r_g2, r_kk, r_ka, r_rk, r_gn_w, r_gn_b, r_up, m_conv_w, m_conv_b, m_wq, m_wk, m_i_b, m_f_b, m_gn_w, m_skip, m_up, gate_b, w_out, norm_ffn_g, ffn_w1, ffn_w2, norm_final_g):
    nbp, seq, _ = x_prompt.shape
    nbs = x_sample.shape[0]
    w = w_in[0]
    wt = jnp.transpose(w)
    w_rm = wt[:RM_COLS].astype(BF16)
    w_g = wt[RM_COLS:].astype(BF16)
    g_mix = norm_mix_g[0].reshape(1, D_MODEL)
    gb = gate_b[0].reshape(1, GATE_COLS)
    rp = dict(mu=r_mu[0].reshape(1, -1), w0=r_w0[0].reshape(1, -1), w2=r_w2[0].astype(BF16),
              a0=r_a0[0].reshape(1, -1), a2=r_a2[0].astype(BF16), g2=r_g2[0].astype(BF16),
              kk=r_kk[0].reshape(1, -1), ka=r_ka[0].reshape(1, -1), rk=r_rk[0].reshape(1, -1),
              gn_w=r_gn_w[0].reshape(1, -1), gn_b=r_gn_b[0].reshape(1, -1), seg=_head_segments())
    mp = dict(conv_w=m_conv_w[0], conv_b=m_conv_b[0].reshape(1, -1), wq=m_wq[0].astype(BF16),
              wk=m_wk[0].astype(BF16), i_b=_pad_lanes(m_i_b[0], 0), f_b=_pad_lanes(m_f_b[0], M_HEADS),
              gn_w=m_gn_w[0].reshape(1, -1), skip=m_skip[0].reshape(1, -1))
    op = dict(r_up=r_up[0], m_up=m_up[0], w_out=w_out[0], g_ffn=norm_ffn_g[0].reshape(1, -1),
              w1=ffn_w1[0], w2=ffn_w2[0], g_fin=norm_final_g.reshape(1, -1))

    xp = x_prompt.reshape(nbp * seq, D_MODEL)
    pr, pm, pg = _proj(xp, g_mix, w_rm, w_g, gb, PROJ_TM)
    y_r, wkv_p = _rwkv_seq(pr.reshape(SEQ_WAYS, -1, R_SHIFT_COLS), rp, nbp, seq // RWKV_TT, RWKV_TT)
    y_r = y_r.reshape(nbp * seq, R_WIDTH)
    wkv_p = wkv_p.reshape(nbp, R_HEADS, R_HEAD_DIM, R_HEAD_DIM)
    y_m, c_p, n_p, m_p = _mlstm_seq(pm, mp, nbp, seq // MLSTM_TT, MLSTM_TT)
    conv_p = pm.reshape(nbp, seq, M_COLS)[:, seq - (CONV_W - 1):, :M_WIDTH]

    xn_s, xn_last, pr_s, pm_s, pg_s, prev_s = _proj_step(
        x_sample[:, 0], state_rwkv_shift[0], x_prompt[:, seq - 1], g_mix, w_rm, w_g, gb)
    yr_s, wkv_t = _rwkv_step(pr_s, prev_s, jnp.transpose(state_rwkv_wkv[0], (1, 2, 3, 0)), rp)
    wkv_s = jnp.transpose(wkv_t, (3, 0, 1, 2))
    conv0 = jnp.transpose(state_mlstm_conv[0], (1, 0, 2))
    ym_s, c_s, n_s, m_s, conv_s = _mlstm_step(pm_s, conv0, state_mlstm_C[0],
                                              state_mlstm_n[0].reshape(nbs, M_WIDTH),
                                              state_mlstm_m[0], mp, STEP_BB)

    y_p, y_s = _out(xp, pg, y_r, y_m, x_sample[:, 0], pg_s, yr_s, ym_s, op, OUT_TM)

    return (y_p.reshape(nbp, seq, D_MODEL), y_s.reshape(nbs, 1, D_MODEL),
            xn_last[None], wkv_p[None], c_p[None], n_p[:, :, 0, :][None], m_p[:, :, 0, 0][None], conv_p[None],
            xn_s[None], wkv_s[None], c_s[None], n_s.reshape(nbs, M_HEADS, M_HEAD_DIM)[None], m_s[None],
            jnp.transpose(conv_s, (1, 0, 2))[None])
```
